```python
import jax, jax.numpy as jnp
from jax import lax
import numpy as np

D_MODEL = 1024
BATCH = 4
SEQ = 4096
DEPTH = 2

CHUNK = 64
EPS = 1e-6
ROPE_BASE = 10000.0
RET_HEADS = 4
RET_DIM = 128
RET_VDIM = 128
ATT_HEADS = 8
ATT_DIM = 64
ATT_LEFT_CHUNKS = 8
REL_CLIP = 128
REL_SIZE = CHUNK - 1 + REL_CLIP + 1
SB_HEADS = 16
SB_DIM = 64
SB_BLOCK = 128
N_EXPERTS = 16
N_GROUPS = 4
TOP_K = 2
D_EXPERT = 512

N_EVEN = (DEPTH + 1) // 2
N_ODD = DEPTH // 2
RET_W = RET_HEADS * RET_DIM
RET_VW = RET_HEADS * RET_VDIM
ATT_W = ATT_HEADS * ATT_DIM
EVEN_IN = 2 * RET_W + 2 * RET_VW + 3 * ATT_W
EVEN_MIX = RET_VW + ATT_W
EVEN_SPLITS = (RET_W, 2 * RET_W, 2 * RET_W + RET_VW, 2 * RET_W + 2 * RET_VW,
               2 * RET_W + 2 * RET_VW + ATT_W, 2 * RET_W + 2 * RET_VW + 2 * ATT_W)
SB_W = SB_HEADS * SB_DIM
ODD_IN = 3 * SB_W

kernel_name = "hybrid_retention_chunkattn_stickbreaking_grouped_moe"


def rms_norm(x, gain):
    xf = x.astype(jnp.float32)
    y = xf * lax.rsqrt(jnp.mean(xf * xf, axis=-1, keepdims=True) + EPS)
    return (y * gain.astype(jnp.float32)).astype(x.dtype)


def head_rms(x):
    xf = x.astype(jnp.float32)
    return xf * lax.rsqrt(jnp.mean(xf * xf, axis=-1, keepdims=True) + EPS)


def rotary(x, pos):
    d = x.shape[-1]
    inv = ROPE_BASE ** (-jnp.arange(0, d, 2, dtype=jnp.float32) / d)
    ang = pos.astype(jnp.float32)[:, None] * inv[None, :]
    cos = jnp.cos(ang)[None, :, None, :]
    sin = jnp.sin(ang)[None, :, None, :]
    x1, x2 = jnp.split(x.astype(jnp.float32), 2, axis=-1)
    return jnp.concatenate([x1 * cos - x2 * sin, x1 * sin + x2 * cos], axis=-1).astype(x.dtype)


def retention(q, k, v):
    B_, S, H, dk = q.shape
    dv = v.shape[-1]
    nc = S // CHUNK
    log_g = jnp.log(1.0 - 2.0 ** (-5.0 - jnp.arange(H, dtype=jnp.float32)))
    idx = jnp.arange(CHUNK, dtype=jnp.float32)
    intra_dec = jnp.exp(log_g[:, None, None] * jnp.abs(idx[:, None] - idx[None, :]))
    q_dec = jnp.exp(log_g[:, None] * (idx[None, :] + 1.0))
    k_dec = jnp.exp(log_g[:, None] * (CHUNK - 1.0 - idx[None, :]))
    chunk_dec = jnp.exp(log_g * CHUNK)
    qc = q.astype(jnp.float32).reshape(B_, nc, CHUNK, H, dk)
    kc = k.astype(jnp.float32).reshape(B_, nc, CHUNK, H, dk)
    vc = v.astype(jnp.float32).reshape(B_, nc, CHUNK, H, dv)
    scores = jnp.einsum('bnihd,bnjhd->bnhij', qc, kc) * intra_dec
    o_intra = jnp.einsum('bnhij,bnjhe->bnihe', scores, vc)
    kv = jnp.einsum('bnjhd,hj,bnjhe->bnhde', kc, k_dec, vc)

    def step(state, kv_n):
        return state * chunk_dec[None, :, None, None] + kv_n, state

    init = jnp.zeros((B_, H, dk, dv), jnp.float32)
    _, prev = lax.scan(step, init, jnp.moveaxis(kv, 1, 0))
    prev = jnp.moveaxis(prev, 0, 1)
    o_cross = jnp.einsum('bnihd,hi,bnhde->bnihe', qc, q_dec, prev)
    return (o_intra + o_cross).reshape(B_, S, H, dv)


def chunk_attention(q, k, v, rel_bias):
    B_, S, H, d = q.shape
    nc = S // CHUNK
    band = ATT_LEFT_CHUNKS + 1

    def gather_band(t):
        tc = t.reshape(B_, nc, CHUNK, H, d)
        tp = jnp.pad(tc, ((0, 0), (ATT_LEFT_CHUNKS, 0), (0, 0), (0, 0), (0, 0)))
        return jnp.concatenate([tp[:, j:j + nc] for j in range(band)], axis=2)

    qc = q.reshape(B_, nc, CHUNK, H, d)
    kb = gather_band(k)
    vb = gather_band(v)
    kk = np.arange(band * CHUNK)
    dist = ATT_LEFT_CHUNKS * CHUNK + np.arange(CHUNK)[:, None] - kk[None, :]
    rel_idx = np.clip(dist, -(CHUNK - 1), REL_CLIP) + (CHUNK - 1)
    bias = rel_bias[:, rel_idx].astype(jnp.float32)
    valid = (np.arange(nc)[:, None] - ATT_LEFT_CHUNKS + kk[None, :] // CHUNK) >= 0
    s = jnp.einsum('bnihd,bnkhd->bnhik', qc, kb).astype(jnp.float32) * (d ** -0.5)
    s = s + bias[None, None]
    s = jnp.where(valid[None, :, None, None, :], s, -jnp.inf)
    p = jax.nn.softmax(s, axis=-1)
    o = jnp.einsum('bnhik,bnkhd->bnihd', p, vb.astype(jnp.float32))
    return o.reshape(B_, S, H, d).astype(q.dtype)


def stick_breaking(q, k, v):
    B_, S, H, d = q.shape
    scale = d ** -0.5
    outs = []
    for blk in range(S // SB_BLOCK):
        q0 = blk * SB_BLOCK
        kend = q0 + SB_BLOCK
        z = jnp.einsum('bihd,bjhd->bhij', q[:, q0:kend], k[:, :kend]).astype(jnp.float32) * scale
        causal = jnp.arange(kend)[None, :] < (q0 + jnp.arange(SB_BLOCK))[:, None]
        log_beta = jax.nn.log_sigmoid(z)
        log_1mb = jnp.where(causal, jax.nn.log_sigmoid(-z), 0.0)
        acc = lax.cumsum(log_1mb, axis=3, reverse=True) - log_1mb
        a = jnp.where(causal, jnp.exp(log_beta + acc), 0.0)
        outs.append(jnp.einsum('bhij,bjhd->bihd', a, v[:, :kend].astype(jnp.float32)))
    return jnp.concatenate(outs, axis=1).astype(q.dtype)


def even_mixer(h, w_in, w_out, q_norm_g, k_norm_g, rel_bias, pos):
    B_, S, _ = h.shape
    proj = h @ w_in
    rq, rk, rv, rg, aq, ak, av = jnp.split(proj, EVEN_SPLITS, axis=-1)
    rq = rotary(rq.reshape(B_, S, RET_HEADS, RET_DIM), pos)
    rk = rotary(rk.reshape(B_, S, RET_HEADS, RET_DIM), pos) * (RET_DIM ** -0.5)
    ret = retention(rq, rk, rv.reshape(B_, S, RET_HEADS, RET_VDIM))
    ret = head_rms(ret).reshape(B_, S, RET_VW).astype(h.dtype) * jax.nn.silu(rg)
    aq = rms_norm(aq.reshape(B_, S, ATT_HEADS, ATT_DIM), q_norm_g)
    ak = rms_norm(ak.reshape(B_, S, ATT_HEADS, ATT_DIM), k_norm_g)
    att = chunk_attention(aq, ak, av.reshape(B_, S, ATT_HEADS, ATT_DIM), rel_bias)
    att = att.reshape(B_, S, ATT_W)
    return jnp.concatenate([ret, att], axis=-1) @ w_out


def odd_mixer(h, w_in, w_out):
    B_, S, _ = h.shape
    q, k, v = jnp.split(h @ w_in, 3, axis=-1)
    shp = (B_, S, SB_HEADS, SB_DIM)
    o = stick_breaking(q.reshape(shp), k.reshape(shp), v.reshape(shp))
    return o.reshape(B_, S, SB_W) @ w_out


def grouped_moe(h, router_w, router_b, w_gate, w_up, w_down):
    B_, S, D = h.shape
    t = h.reshape(B_ * S, D)
    scores = jax.nn.sigmoid((t @ router_w).astype(jnp.float32))
    sel = scores + router_b.astype(jnp.float32)
    per_group = N_EXPERTS // N_GROUPS
    group_score = lax.top_k(sel.reshape(-1, N_GROUPS, per_group), TOP_K)[0].sum(-1)
    best_group = jnp.argmax(group_score, axis=-1)
    in_group = (jnp.arange(N_EXPERTS) // per_group)[None, :] == best_group[:, None]
    _, top_idx = lax.top_k(jnp.where(in_group, sel, -jnp.inf), TOP_K)
    top_s = jnp.take_along_axis(scores, top_idx, axis=-1)
    top_w = top_s / jnp.sum(top_s, axis=-1, keepdims=True)
    combine = jnp.einsum('nk,nke->ne', top_w, jax.nn.one_hot(top_idx, N_EXPERTS, dtype=jnp.float32))
    y = jnp.zeros((t.shape[0], D), jnp.float32)
    for e in range(N_EXPERTS):
        he = jax.nn.silu(t @ w_gate[e]) * (t @ w_up[e])
        y = y + combine[:, e:e + 1] * (he @ w_down[e]).astype(jnp.float32)
    return y.reshape(B_, S, D).astype(h.dtype)


def setup_inputs(seed: int = 0) -> dict:
    key = jax.random.key(seed)
    ks = jax.random.split(key, 20)

    def nrm(k, shape, s):
        return jax.random.normal(k, shape, jnp.float32) * s

    return {
        "x": nrm(ks[0], (BATCH, SEQ, D_MODEL), 1.0),
        "c": nrm(ks[1], (BATCH, D_MODEL), 1.0),
        "ada_w": nrm(ks[2], (DEPTH, D_MODEL, 6 * D_MODEL), 0.5 * D_MODEL ** -0.5),
        "ada_b": nrm(ks[3], (DEPTH, 6 * D_MODEL), 0.02),
        "norm1_g": 1.0 + nrm(ks[4], (DEPTH, D_MODEL), 0.02),
        "norm2_g": 1.0 + nrm(ks[5], (DEPTH, D_MODEL), 0.02),
        "even_w_in": nrm(ks[6], (N_EVEN, D_MODEL, EVEN_IN), D_MODEL ** -0.5),
        "even_w_out": nrm(ks[7], (N_EVEN, EVEN_MIX, D_MODEL), EVEN_MIX ** -0.5),
        "att_q_norm_g": 1.0 + nrm(ks[8], (N_EVEN, ATT_DIM), 0.02),
        "att_k_norm_g": 1.0 + nrm(ks[9], (N_EVEN, ATT_DIM), 0.02),
        "att_rel_bias": nrm(ks[10], (N_EVEN, ATT_HEADS, REL_SIZE), 0.2),
        "odd_w_in": nrm(ks[11], (N_ODD, D_MODEL, ODD_IN), D_MODEL ** -0.5),
        "odd_w_out": nrm(ks[12], (N_ODD, SB_W, D_MODEL), SB_W ** -0.5),
        "router_w": nrm(ks[13], (D_MODEL, N_EXPERTS), D_MODEL ** -0.5),
        "router_b": nrm(ks[14], (N_EXPERTS,), 0.01),
        "exp_w_gate": nrm(ks[15], (DEPTH, N_EXPERTS, D_MODEL, D_EXPERT), D_MODEL ** -0.5),
        "exp_w_up": nrm(ks[16], (DEPTH, N_EXPERTS, D_MODEL, D_EXPERT), D_MODEL ** -0.5),
        "exp_w_down": nrm(ks[17], (DEPTH, N_EXPERTS, D_EXPERT, D_MODEL), D_EXPERT ** -0.5),
    }


def reference(x, c, ada_w, ada_b, norm1_g, norm2_g, even_w_in, even_w_out,
              att_q_norm_g, att_k_norm_g, att_rel_bias, odd_w_in, odd_w_out,
              router_w, router_b, exp_w_gate, exp_w_up, exp_w_down):
    S = x.shape[1]
    pos = jnp.arange(S, dtype=jnp.int32)
    c_act = jax.nn.silu(c)
    for layer in range(DEPTH):
        mod = (c_act @ ada_w[layer] + ada_b[layer])[:, None, :]
        sh1, sc1, g1, sh2, sc2, g2 = jnp.split(mod, 6, axis=-1)
        h = rms_norm(x, norm1_g[layer]) * (1.0 + sc1) + sh1
        i = layer // 2
        if layer % 2 == 0:
            mix = even_mixer(h, even_w_in[i], even_w_out[i], att_q_norm_g[i],
                             att_k_norm_g[i], att_rel_bias[i], pos)
        else:
            mix = odd_mixer(h, odd_w_in[i], odd_w_out[i])
        x = x + g1 * mix
        h = rms_norm(x, norm2_g[layer]) * (1.0 + sc2) + sh2
        x = x + g2 * grouped_moe(h, router_w, router_b, exp_w_gate[layer],
                                 exp_w_up[layer], exp_w_down[layer])
    return x
```

```python
import functools

import numpy as np
import jax
import jax.numpy as jnp
from jax import lax
from jax.experimental import pallas as pl
from jax.experimental.pallas import tpu as pltpu

F32 = jnp.float32
BF16 = jnp.bfloat16

D_MODEL = 1024
CHUNK = 64
EPS = 1e-6
ROPE_BASE = 10000.0
RET_HEADS = 4
RET_DIM = 128
ATT_HEADS = 8
ATT_DIM = 64
ATT_LEFT_CHUNKS = 8
REL_CLIP = 128
SB_HEADS = 16
SB_DIM = 64
N_EXPERTS = 16
N_GROUPS = 4
D_EXPERT = 512
HALF = 512

VMEM_LIMIT = 48 * 1024 * 1024
LANES = 128

ROW_TILE = 512
RET_TILE = 256
ATT_QB = 128
ATT_NB = ATT_LEFT_CHUNKS * CHUNK // ATT_QB + 1
SB_BLK = 128
SB_EXIT = -120.0

PAIR_A = (0, 0, 0, 1, 1, 2)
PAIR_B = (1, 2, 3, 2, 3, 3)
N_CLASS = N_GROUPS * len(PAIR_A)
CLASS_ROWS = 32
MOE_TILE = 256
EXT = D_MODEL + LANES


def _cparams(sem):
    return pltpu.CompilerParams(dimension_semantics=sem, vmem_limit_bytes=VMEM_LIMIT)


def _mod_kernel(c_ref, w_ref, b_ref, o_ref):
    c = c_ref[...]
    ca = c * jax.nn.sigmoid(c)
    o_ref[0] = jnp.dot(ca.astype(BF16), w_ref[0].astype(BF16), preferred_element_type=F32) + b_ref[0]


def _modulation(c, ada_w, ada_b):
    depth, _, width = ada_w.shape
    bsz = c.shape[0]
    rows = 8
    cp = jnp.zeros((rows, D_MODEL), F32).at[:bsz].set(c)
    tn = 1536
    out = pl.pallas_call(
        _mod_kernel,
        grid=(depth, width // tn),
        in_specs=[pl.BlockSpec((rows, D_MODEL), lambda l, j: (0, 0)),
                  pl.BlockSpec((1, D_MODEL, tn), lambda l, j: (l, 0, j)),
                  pl.BlockSpec((1, 1, tn), lambda l, j: (l, 0, j))],
        out_specs=pl.BlockSpec((1, rows, tn), lambda l, j: (l, 0, j)),
        out_shape=jax.ShapeDtypeStruct((depth, rows, width), F32),
        compiler_params=_cparams(("arbitrary", "arbitrary")),
        name="adaln_mod",
    )(cp, ada_w, ada_b.reshape(depth, 1, width))
    return out[:, :bsz]


def _norm_mod(x, g, sc, sh):
    ms = jnp.mean(x * x, axis=-1, keepdims=True)
    return (x * lax.rsqrt(ms + EPS) * g) * (1.0 + sc) + sh


def _proj_kernel(x_ref, g_ref, sc_ref, sh_ref, w_ref, o_ref):
    h = _norm_mod(x_ref[...], g_ref[...], sc_ref[0], sh_ref[0]).astype(BF16)
    for n0 in range(0, o_ref.shape[1], HALF):
        o_ref[:, n0:n0 + HALF] = jnp.dot(h, w_ref[:, n0:n0 + HALF],
                                         preferred_element_type=F32).astype(BF16)


def _proj_res_kernel(x_ref, y_ref, gt_ref, g_ref, sc_ref, sh_ref, w_ref, xo_ref, o_ref):
    x = x_ref[...] + gt_ref[0] * y_ref[...]
    xo_ref[...] = x
    h = _norm_mod(x, g_ref[...], sc_ref[0], sh_ref[0]).astype(BF16)
    for n0 in range(0, o_ref.shape[1], HALF):
        o_ref[:, n0:n0 + HALF] = jnp.dot(h, w_ref[:, n0:n0 + HALF],
                                         preferred_element_type=F32).astype(BF16)


def _project(x, gain, sc, sh, w, seq, y=None, gate=None):
    n = x.shape[0]
    nout = w.shape[1]
    tpb = seq // ROW_TILE
    row = pl.BlockSpec((ROW_TILE, D_MODEL), lambda i: (i, 0))
    per_b = pl.BlockSpec((1, 1, D_MODEL), lambda i: (i // tpb, 0, 0))
    gspec = pl.BlockSpec((1, D_MODEL), lambda i: (0, 0))
    wspec = pl.BlockSpec((D_MODEL, nout), lambda i: (0, 0))
    ospec = pl.BlockSpec((ROW_TILE, nout), lambda i: (i, 0))
    oshape = jax.ShapeDtypeStruct((n, nout), BF16)
    if y is None:
        return pl.pallas_call(
            _proj_kernel, grid=(n // ROW_TILE,),
            in_specs=[row, gspec, per_b, per_b, wspec], out_specs=ospec, out_shape=oshape,
            compiler_params=_cparams(("arbitrary",)), name="norm_proj",
        )(x, gain.reshape(1, D_MODEL), sc, sh, w)
    return pl.pallas_call(
        _proj_res_kernel, grid=(n // ROW_TILE,),
        in_specs=[row, row, per_b, gspec, per_b, per_b, wspec],
        out_specs=[row, ospec],
        out_shape=[jax.ShapeDtypeStruct((n, D_MODEL), F32), oshape],
        compiler_params=_cparams(("arbitrary",)), name="res_norm_proj",
    )(x, y, gate, gain.reshape(1, D_MODEL), sc, sh, w)


def _ret_tables(seq):
    inv = ROPE_BASE ** (-np.arange(0, RET_DIM, 2, dtype=np.float64) / RET_DIM)
    ang = np.arange(seq, dtype=np.float64)[:, None] * inv[None, :]
    cos = np.concatenate([np.cos(ang), np.cos(ang)], axis=1)
    sin = np.concatenate([-np.sin(ang), np.sin(ang)], axis=1)
    log_g = np.log(1.0 - 2.0 ** (-5.0 - np.arange(RET_HEADS, dtype=np.float64)))
    idx = np.arange(RET_TILE)
    same_or_earlier = (idx[None, :] // CHUNK) <= (idx[:, None] // CHUNK)
    dec = np.exp(log_g[:, None, None] * np.abs(idx[:, None] - idx[None, :])) * same_or_earlier
    loc = idx.astype(np.float64)
    qdec = np.exp(log_g[:, None] * (loc[None, :] + 1.0))
    kdec = np.exp(log_g[:, None] * (RET_TILE - 1.0 - loc[None, :]))
    tdec = np.exp(log_g * RET_TILE)
    qk = np.stack([qdec, kdec], axis=1)[..., None] * np.ones((1, 1, 1, RET_DIM))
    return (jnp.asarray(cos, F32), jnp.asarray(sin, F32), jnp.asarray(dec, F32),
            jnp.asarray(qk, F32), tuple(float(t) for t in tdec))


def _ret_kernel(tdec, q_ref, k_ref, v_ref, g_ref, cos_ref, sin_ref, dec_ref, qk_ref, o_ref, st_ref):
    @pl.when(pl.program_id(1) == 0)
    def _():
        st_ref[...] = jnp.zeros_like(st_ref)

    cos = cos_ref[...]
    sin = sin_ref[...]
    half = RET_DIM // 2
    for h in range(RET_HEADS):
        cols = slice(h * RET_DIM, (h + 1) * RET_DIM)
        q = q_ref[:, cols].astype(F32)
        k = k_ref[:, cols].astype(F32)
        q = q * cos + pltpu.roll(q, half, 1) * sin
        k = (k * cos + pltpu.roll(k, half, 1) * sin) * (RET_DIM ** -0.5)
        v = v_ref[:, cols]
        qb = q.astype(BF16)
        kb = k.astype(BF16)
        s = lax.dot_general(qb, kb, (((1,), (1,)), ((), ())), preferred_element_type=F32)
        s = s * dec_ref[h]
        o = jnp.dot(s.astype(BF16), v, preferred_element_type=F32)
        st = st_ref[h]
        o = o + jnp.dot((q * qk_ref[h, 0]).astype(BF16), st.astype(BF16), preferred_element_type=F32)
        kd = (k * qk_ref[h, 1]).astype(BF16)
        st_ref[h] = st * tdec[h] + lax.dot_general(kd, v, (((0,), (0,)), ((), ())),
                                                   preferred_element_type=F32)
        o = o * lax.rsqrt(jnp.mean(o * o, axis=-1, keepdims=True) + EPS)
        g = g_ref[:, cols].astype(F32)
        o_ref[:, cols] = (o * (g * jax.nn.sigmoid(g))).astype(BF16)


def _retention(proj, bsz, seq):
    n = proj.shape[0]
    nt = seq // RET_TILE
    cos, sin, dec, qk, tdec = _ret_tables(seq)

    def col(c):
        return pl.BlockSpec((RET_TILE, HALF), lambda b, i: (b * nt + i, c))

    pos = pl.BlockSpec((RET_TILE, RET_DIM), lambda b, i: (i, 0))
    return pl.pallas_call(
        functools.partial(_ret_kernel, tdec),
        grid=(bsz, nt),
        in_specs=[col(0), col(1), col(2), col(3), pos, pos,
                  pl.BlockSpec((RET_HEADS, RET_TILE, RET_TILE), lambda b, i: (0, 0, 0)),
                  pl.BlockSpec((RET_HEADS, 2, RET_TILE, RET_DIM), lambda b, i: (0, 0, 0, 0))],
        out_specs=pl.BlockSpec((RET_TILE, HALF), lambda b, i: (b * nt + i, 0)),
        out_shape=jax.ShapeDtypeStruct((n, HALF), BF16),
        scratch_shapes=[pltpu.VMEM((RET_HEADS, RET_DIM, RET_DIM), F32)],
        compiler_params=_cparams(("arbitrary", "arbitrary")), name="retention",
    )(proj, proj, proj, proj, cos, sin, dec, qk)


def _att_bias(rel_bias):
    iq = np.arange(ATT_QB)[:, None]
    col = np.arange(ATT_NB * ATT_QB)[None, :]
    dist = (ATT_NB - 1) * ATT_QB + iq - col
    rel_idx = np.clip(dist, -(CHUNK - 1), REL_CLIP) + (CHUNK - 1)
    qc = iq // CHUNK
    kc = col // CHUNK - (ATT_NB - 1) * (ATT_QB // CHUNK)
    ok = (kc <= qc) & (kc >= qc - ATT_LEFT_CHUNKS)
    return jnp.where(jnp.asarray(ok)[None], rel_bias[:, rel_idx].astype(F32), -jnp.inf)


def _head_norm(t, gain):
    return t * lax.rsqrt(jnp.mean(t * t, axis=-1, keepdims=True) + EPS) * gain


def _att_kernel(q_ref, k_ref, v_ref, qg_ref, kg_ref, bias_ref, o_ref):
    i = pl.program_id(1)
    qg = qg_ref[...]
    kg = kg_ref[...]
    kvs = []
    for d in range(ATT_NB):
        blk = i - (ATT_NB - 1) + d
        start = pl.multiple_of(jnp.maximum(blk, 0) * ATT_QB, ATT_QB)
        kvs.append((k_ref[pl.ds(start, ATT_QB), :].astype(F32), v_ref[pl.ds(start, ATT_QB), :],
                    jnp.where(blk >= 0, 0.0, -jnp.inf)))
    for h in range(ATT_HEADS):
        cols = slice(h * ATT_DIM, (h + 1) * ATT_DIM)
        q = _head_norm(q_ref[:, cols].astype(F32), qg).astype(BF16)
        parts = []
        for d in range(ATT_NB):
            kf, _, pen = kvs[d]
            kh = _head_norm(kf[:, cols], kg).astype(BF16)
            s = lax.dot_general(q, kh, (((1,), (1,)), ((), ())), preferred_element_type=F32)
            parts.append(s * (ATT_DIM ** -0.5) + bias_ref[h, :, d * ATT_QB:(d + 1) * ATT_QB] + pen)
        m = parts[0].max(axis=-1, keepdims=True)
        for p_ in parts[1:]:
            m = jnp.maximum(m, p_.max(axis=-1, keepdims=True))
        den = jnp.zeros_like(m)
        acc = jnp.zeros((ATT_QB, ATT_DIM), F32)
        for d in range(ATT_NB):
            e = jnp.exp(parts[d] - m)
            den = den + e.sum(axis=-1, keepdims=True)
            acc = acc + jnp.dot(e.astype(BF16), kvs[d][1][:, cols], preferred_element_type=F32)
        o_ref[:, cols] = (acc / den).astype(BF16)


def _chunk_attention(proj, q_gain, k_gain, rel_bias, bsz, seq):
    n = proj.shape[0]
    nq = seq // ATT_QB
    bias = _att_bias(rel_bias)
    gspec = pl.BlockSpec((1, ATT_DIM), lambda b, i: (0, 0))
    return pl.pallas_call(
        _att_kernel,
        grid=(bsz, nq),
        in_specs=[pl.BlockSpec((ATT_QB, HALF), lambda b, i: (b * nq + i, 4)),
                  pl.BlockSpec((seq, HALF), lambda b, i: (b, 5)),
                  pl.BlockSpec((seq, HALF), lambda b, i: (b, 6)),
                  gspec, gspec,
                  pl.BlockSpec((ATT_HEADS, ATT_QB, ATT_NB * ATT_QB), lambda b, i: (0, 0, 0))],
        out_specs=pl.BlockSpec((ATT_QB, HALF), lambda b, i: (b * nq + i, 0)),
        out_shape=jax.ShapeDtypeStruct((n, HALF), BF16),
        compiler_params=_cparams(("arbitrary", "arbitrary")), name="chunk_attention",
    )(proj, proj, proj, q_gain.reshape(1, ATT_DIM), k_gain.reshape(1, ATT_DIM), bias)


def _log1pexp_neg_abs(z):
    return jnp.log1p(jnp.exp(-jnp.abs(z)))


def _sb_kernel(q_ref, k_ref, v_ref, tri_ref, o_ref):
    iq = pl.program_id(2)
    lane = lax.broadcasted_iota(jnp.int32, (SB_BLK, LANES), 1)
    row = lax.broadcasted_iota(jnp.int32, (SB_BLK, SB_BLK), 0)
    colk = lax.broadcasted_iota(jnp.int32, (SB_BLK, SB_BLK), 1)
    below = colk < row
    first = lane < SB_DIM
    q = q_ref[...]
    zero = jnp.zeros_like(q)
    qs = (jnp.where(first, q, zero), jnp.where(first, zero, q))
    tri = tri_ref[...]

    def cond(c):
        j, alive = c[0], c[1]
        return jnp.logical_and(j >= 0, alive > SB_EXIT)

    def body(c):
        j = c[0]
        start = pl.multiple_of(j * SB_BLK, SB_BLK)
        kb = k_ref[pl.ds(start, SB_BLK), :]
        vb = v_ref[pl.ds(start, SB_BLK), :]
        mask = jnp.logical_or(below, j < iq)
        new = []
        alive = jnp.float32(-jnp.inf)
        for hh in range(2):
            run, acc = c[2 + 2 * hh], c[3 + 2 * hh]
            z = lax.dot_general(qs[hh], kb, (((1,), (1,)), ((), ())),
                                preferred_element_type=F32) * (SB_DIM ** -0.5)
            u = _log1pexp_neg_abs(z)
            log_beta = jnp.minimum(z, 0.0) - u
            l1 = jnp.where(mask, jnp.minimum(-z, 0.0) - u, 0.0)
            hi = l1.astype(BF16)
            lo = (l1 - hi.astype(F32)).astype(BF16)
            cs = (jnp.dot(hi, tri, preferred_element_type=F32)
                  + jnp.dot(lo, tri, preferred_element_type=F32))
            a = jnp.where(mask, jnp.exp(log_beta + run + cs[:, :SB_BLK]), 0.0)
            acc = acc + jnp.dot(a.astype(BF16), vb, preferred_element_type=F32)
            run = run + cs[:, SB_BLK:]
            alive = jnp.maximum(alive, jnp.max(run))
            new += [run, acc]
        return (j - 1, alive, *new)

    zt = jnp.zeros((SB_BLK, LANES), F32)
    out = lax.while_loop(cond, body, (iq, jnp.float32(0.0), zt, zt, zt, zt))
    o_ref[...] = jnp.where(first, out[3], out[5]).astype(BF16)


def _stick_breaking(proj, bsz, seq):
    n = proj.shape[0]
    nq = seq // SB_BLK
    pairs = SB_HEADS * SB_DIM // LANES
    j = np.arange(SB_BLK)
    tri = np.concatenate([(j[:, None] > j[None, :]).astype(np.float32),
                          np.ones((SB_BLK, SB_BLK), np.float32)], axis=1)
    return pl.pallas_call(
        _sb_kernel,
        grid=(bsz, pairs, nq),
        in_specs=[pl.BlockSpec((SB_BLK, LANES), lambda b, p, i: (b * nq + i, p)),
                  pl.BlockSpec((seq, LANES), lambda b, p, i: (b, pairs + p)),
                  pl.BlockSpec((seq, LANES), lambda b, p, i: (b, 2 * pairs + p)),
                  pl.BlockSpec((SB_BLK, 2 * SB_BLK), lambda b, p, i: (0, 0))],
        out_specs=pl.BlockSpec((SB_BLK, LANES), lambda b, p, i: (b * nq + i, p)),
        out_shape=jax.ShapeDtypeStruct((n, SB_HEADS * SB_DIM), BF16),
        compiler_params=_cparams(("arbitrary", "arbitrary", "arbitrary")), name="stick_breaking",
    )(proj, proj, proj, jnp.asarray(tri, BF16))


def _out_route_kernel(ma_ref, mb_ref, wa_ref, wb_ref, x_ref, g1_ref, g_ref, sc_ref, sh_ref,
                      rw_ref, rb_ref, tri_ref, xo_ref, he_ref, rt_ref, cnt_ref, carry_ref):
    @pl.when(pl.program_id(0) == 0)
    def _():
        carry_ref[...] = jnp.zeros_like(carry_ref)

    tm = x_ref.shape[0]
    mix = (jnp.dot(ma_ref[...], wa_ref[...], preferred_element_type=F32)
           + jnp.dot(mb_ref[...], wb_ref[...], preferred_element_type=F32))
    x = x_ref[...] + g1_ref[0] * mix
    xo_ref[...] = x
    h = _norm_mod(x, g_ref[...], sc_ref[0], sh_ref[0])
    he_ref[:, :D_MODEL] = h

    nt = (((1,), (1,)), ((), ()))
    h_hi = h.astype(BF16)
    h_lo = (h - h_hi.astype(F32)).astype(BF16)
    w = rw_ref[...]
    w_hi = w.astype(BF16)
    w_lo = (w - w_hi.astype(F32)).astype(BF16)
    logits = (lax.dot_general(w_hi, h_hi, nt, preferred_element_type=F32)
              + lax.dot_general(w_hi, h_lo, nt, preferred_element_type=F32)
              + lax.dot_general(w_lo, h_hi, nt, preferred_element_type=F32))
    score = jax.nn.sigmoid(logits)
    sel = score + rb_ref[...]

    best = None
    for g in range(N_GROUPS):
        for p in range(len(PAIR_A)):
            ea, eb = 4 * g + PAIR_A[p], 4 * g + PAIR_B[p]
            val = sel[ea:ea + 1, :] + sel[eb:eb + 1, :]
            cid = jnp.full((1, tm), float(len(PAIR_A) * g + p), F32)
            cand = (val, cid, score[ea:ea + 1, :], score[eb:eb + 1, :])
            if best is None:
                best = cand
            else:
                take = val > best[0]
                best = tuple(jnp.where(take, c_, b_) for c_, b_ in zip(cand, best))
    _, cls, sa, sb = best
    tot = sa + sb
    wgt_a = sa / tot
    wgt_b = sb / tot

    crow = lax.broadcasted_iota(jnp.int32, (CLASS_ROWS, tm), 0).astype(F32)
    onehot = jnp.where(crow == cls, 1.0, 0.0)
    before = jnp.dot(onehot.astype(BF16), tri_ref[...], preferred_element_type=F32)
    carry = carry_ref[...]
    rank = jnp.sum(onehot * (before + carry[:, :1]), axis=0, keepdims=True)
    carry = carry + jnp.sum(onehot, axis=1, keepdims=True)
    carry_ref[...] = carry
    cnt_ref[0] = carry

    srow = lax.broadcasted_iota(jnp.int32, (8, tm), 0)
    rt_ref[0] = jnp.where(srow == 0, cls, jnp.where(srow == 1, rank, 0.0))
    wrow = lax.broadcasted_iota(jnp.int32, (LANES, tm), 0)
    wtile = jnp.where(wrow == 0, wgt_a, jnp.where(wrow == 1, wgt_b, 0.0))
    he_ref[:, D_MODEL:] = wtile.T


def _out_route(mix_a, mix_b, col_a, col_b, w_out, x, g1, gain, sc, sh, router_w, router_b, seq):
    n = x.shape[0]
    tm = ROW_TILE
    nt = n // tm
    tpb = seq // tm
    t = np.arange(tm)
    tri = jnp.asarray((t[:, None] < t[None, :]).astype(np.float32), BF16)
    row = pl.BlockSpec((tm, D_MODEL), lambda i: (i, 0))
    per_b = pl.BlockSpec((1, 1, D_MODEL), lambda i: (i // tpb, 0, 0))
    rb = jnp.broadcast_to(router_b.astype(F32)[:, None], (N_EXPERTS, tm))
    return pl.pallas_call(
        _out_route_kernel,
        grid=(nt,),
        in_specs=[pl.BlockSpec((tm, HALF), lambda i: (i, col_a)),
                  pl.BlockSpec((tm, HALF), lambda i: (i, col_b)),
                  pl.BlockSpec((HALF, D_MODEL), lambda i: (0, 0)),
                  pl.BlockSpec((HALF, D_MODEL), lambda i: (1, 0)),
                  row, per_b,
                  pl.BlockSpec((1, D_MODEL), lambda i: (0, 0)), per_b, per_b,
                  pl.BlockSpec((N_EXPERTS, D_MODEL), lambda i: (0, 0)),
                  pl.BlockSpec((N_EXPERTS, tm), lambda i: (0, 0)),
                  pl.BlockSpec((tm, tm), lambda i: (0, 0))],
        out_specs=[row,
                   pl.BlockSpec((tm, EXT), lambda i: (i, 0)),
                   pl.BlockSpec((1, 8, tm), lambda i: (i, 0, 0)),
                   pl.BlockSpec((1, CLASS_ROWS, LANES), lambda i: (i, 0, 0))],
        out_shape=[jax.ShapeDtypeStruct((n, D_MODEL), F32),
                   jax.ShapeDtypeStruct((n, EXT), F32),
                   jax.ShapeDtypeStruct((nt, 8, tm), F32),
                   jax.ShapeDtypeStruct((nt, CLASS_ROWS, LANES), F32)],
        scratch_shapes=[pltpu.VMEM((CLASS_ROWS, LANES), F32)],
        compiler_params=_cparams(("arbitrary",)), name="out_proj_route",
    )(mix_a, mix_b, w_out, w_out, x, g1, gain.reshape(1, D_MODEL), sc, sh,
      router_w.T.astype(F32), rb, tri)


def _slot_kernel(pos_ref, inv_ref):
    n = pos_ref.shape[0]
    m = inv_ref.shape[0]

    def fill(s, c):
        inv_ref[s] = 0
        return c

    lax.fori_loop(0, m, fill, 0, unroll=8)

    def place(t, c):
        inv_ref[pos_ref[t]] = t
        return c

    lax.fori_loop(0, n, place, 0, unroll=8)


def _routing_plan(route, counts, n):
    nt, _, tm = route.shape
    cls = route[:, 0, :].reshape(n).astype(jnp.int32)
    rank = route[:, 1, :].reshape(n).astype(jnp.int32)
    cnt = counts[-1, :N_CLASS, 0].astype(jnp.int32)
    tiles_c = (cnt + MOE_TILE - 1) // MOE_TILE
    ends = jnp.cumsum(tiles_c)
    starts = ends - tiles_c
    pos = starts[cls] * MOE_TILE + rank
    max_tiles = n // MOE_TILE + N_CLASS
    total = ends[-1]
    j = jnp.arange(max_tiles, dtype=jnp.int32)
    jj = jnp.minimum(j, total - 1)
    tcls = jnp.sum((ends[None, :] <= jj[:, None]).astype(jnp.int32), axis=1)
    grp = tcls // len(PAIR_A)
    pair = tcls % len(PAIR_A)
    ea = 4 * grp + jnp.asarray(PAIR_A, jnp.int32)[pair]
    eb = 4 * grp + jnp.asarray(PAIR_B, jnp.int32)[pair]
    valid_rows = jnp.clip(cnt[tcls] - (jj - starts[tcls]) * MOE_TILE, 0, MOE_TILE)
    valid_rows = jnp.where(j < total, valid_rows, 0).astype(jnp.int32)
    inv = pl.pallas_call(
        _slot_kernel,
        in_specs=[pl.BlockSpec(memory_space=pltpu.SMEM)],
        out_specs=pl.BlockSpec(memory_space=pltpu.SMEM),
        out_shape=jax.ShapeDtypeStruct((max_tiles * MOE_TILE,), jnp.int32),
        name="moe_slots",
    )(pos)
    return inv, ea, eb, valid_rows, total.reshape(1)


def _moe_kernel(inv_ref, ea_ref, eb_ref, nv_ref, tot_ref, h_hbm, wg_a, wu_a, wd_a, wg_b, wu_b, wd_b,
                y_hbm, gbuf, ybuf, gsem, ssem):
    del ea_ref, eb_ref
    j = pl.program_id(0)
    total = tot_ref[0]
    slot = j % 2
    unroll = 8

    def gather_start(tile, s):
        base = tile * MOE_TILE

        def issue(r, c):
            pltpu.make_async_copy(h_hbm.at[pl.ds(inv_ref[base + r], 1), :],
                                  gbuf.at[s, pl.ds(r, 1), :], gsem.at[s]).start()
            return c

        lax.fori_loop(0, MOE_TILE, issue, 0, unroll=unroll)

    def gather_wait(s):
        pltpu.make_async_copy(h_hbm.at[pl.ds(0, MOE_TILE), :], gbuf.at[s], gsem.at[s]).wait()

    def scatter_start(tile, s):
        base = tile * MOE_TILE
        nv = nv_ref[tile]

        def issue(r):
            pltpu.make_async_copy(ybuf.at[s, pl.ds(r, 1), :],
                                  y_hbm.at[pl.ds(inv_ref[base + r], 1), :], ssem.at[s]).start()

        groups = nv // unroll

        def group(g, c):
            for u in range(unroll):
                issue(g * unroll + u)
            return c

        lax.fori_loop(0, groups, group, 0)
        for u in range(unroll):
            @pl.when(groups * unroll + u < nv)
            def _():
                issue(groups * unroll + u)

    def scatter_wait(tile, s):
        nv = nv_ref[tile]
        for bit in range(MOE_TILE.bit_length()):
            rows = 1 << bit

            @pl.when((nv >> bit) & 1 == 1)
            def _():
                pltpu.make_async_copy(ybuf.at[s, pl.ds(0, rows), :], y_hbm.at[pl.ds(0, rows), :],
                                      ssem.at[s]).wait()

    @pl.when(j == 0)
    def _():
        gather_start(0, 0)

    @pl.when(j < total)
    def _():
        gather_wait(slot)

        @pl.when(j + 1 < total)
        def _():
            gather_start(j + 1, 1 - slot)

        rows = gbuf[slot]
        xb = rows[:, :D_MODEL].astype(BF16)
        y = None
        for lane, (wg, wu, wd) in enumerate(((wg_a, wu_a, wd_a), (wg_b, wu_b, wd_b))):
            g = jnp.dot(xb, wg[0], preferred_element_type=F32)
            u = jnp.dot(xb, wu[0], preferred_element_type=F32)
            wgt = rows[:, D_MODEL + lane:D_MODEL + lane + 1]
            he = ((g * jax.nn.sigmoid(g)) * u * wgt).astype(BF16)
            part = jnp.dot(he, wd[0], preferred_element_type=F32)
            y = part if y is None else y + part

        @pl.when(j >= 2)
        def _():
            scatter_wait(j - 2, slot)

        ybuf[slot] = y
        scatter_start(j, slot)

        @pl.when(j == total - 1)
        def _():
            @pl.when(j >= 1)
            def _():
                scatter_wait(j - 1, 1 - slot)

            scatter_wait(j, slot)


def _moe(hext, inv, ea, eb, valid_rows, total, w_gate, w_up, w_down):
    n = hext.shape[0]
    max_tiles = inv.shape[0] // MOE_TILE

    def wspec(shape, which):
        if which == 0:
            return pl.BlockSpec((1,) + shape, lambda j, inv_, ea_, eb_, nv_, t_: (ea_[j], 0, 0))
        return pl.BlockSpec((1,) + shape, lambda j, inv_, ea_, eb_, nv_, t_: (eb_[j], 0, 0))

    up = (D_MODEL, D_EXPERT)
    down = (D_EXPERT, D_MODEL)
    grid_spec = pltpu.PrefetchScalarGridSpec(
        num_scalar_prefetch=5,
        grid=(max_tiles,),
        in_specs=[pl.BlockSpec(memory_space=pl.ANY),
                  wspec(up, 0), wspec(up, 0), wspec(down, 0),
                  wspec(up, 1), wspec(up, 1), wspec(down, 1)],
        out_specs=pl.BlockSpec(memory_space=pl.ANY),
        scratch_shapes=[pltpu.VMEM((2, MOE_TILE, EXT), F32),
                        pltpu.VMEM((2, MOE_TILE, D_MODEL), F32),
                        pltpu.SemaphoreType.DMA((2,)),
                        pltpu.SemaphoreType.DMA((2,))],
    )
    return pl.pallas_call(
        _moe_kernel,
        grid_spec=grid_spec,
        out_shape=jax.ShapeDtypeStruct((n, D_MODEL), F32),
        compiler_params=_cparams(("arbitrary",)), name="moe_experts",
    )(inv, ea, eb, valid_rows, total, hext, w_gate, w_up, w_down, w_gate, w_up, w_down)


def _final_kernel(x_ref, y_ref, g_ref, o_ref):
    o_ref[...] = x_ref[...] + g_ref[0] * y_ref[...]


def _final(x, y, gate, seq):
    n = x.shape[0]
    tpb = seq // ROW_TILE
    row = pl.BlockSpec((ROW_TILE, D_MODEL), lambda i: (i, 0))
    return pl.pallas_call(
        _final_kernel, grid=(n // ROW_TILE,),
        in_specs=[row, row, pl.BlockSpec((1, 1, D_MODEL), lambda i: (i // tpb, 0, 0))],
        out_specs=row, out_shape=jax.ShapeDtypeStruct((n, D_MODEL), F32),
        compiler_params=_cparams(("arbitrary",)), name="final_residual",
    )(x, y, gate)


def kernel(x, c, ada_w, ada_b, norm1_g, norm2_g, even_w_in, even_w_out, att_q_norm_g, att_k_norm_g,
           att_rel_bias, odd_w_in, odd_w_out, router_w, router_b, exp_w_gate, exp_w_up, exp_w_down):
    bsz, seq, d = x.shape
    n = bsz * seq
    mod = _modulation(c, ada_w, ada_b)
    mods = [[mod[l, :, k * d:(k + 1) * d].reshape(bsz, 1, d) for k in range(6)] for l in range(2)]
    xf = x.reshape(n, d)

    def moe_layer(layer, mix_a, mix_b, col_a, col_b, w_out, xin):
        sh1, sc1, g1, sh2, sc2, g2 = mods[layer]
        x1, hext, route, counts = _out_route(mix_a, mix_b, col_a, col_b, w_out.astype(BF16), xin, g1,
                                             norm2_g[layer], sc2, sh2, router_w, router_b, seq)
        inv, ea, eb, valid_rows, total = _routing_plan(route, counts, n)
        y = _moe(hext, inv, ea, eb, valid_rows, total, exp_w_gate[layer].astype(BF16),
                 exp_w_up[layer].astype(BF16), exp_w_down[layer].astype(BF16))
        return x1, y, g2

    sh1, sc1, _, _, _, _ = mods[0]
    proj0 = _project(xf, norm1_g[0], sc1, sh1, even_w_in[0].astype(BF16), seq)
    ret = _retention(proj0, bsz, seq)
    att = _chunk_attention(proj0, att_q_norm_g[0], att_k_norm_g[0], att_rel_bias[0], bsz, seq)
    x1, y0, g2_0 = moe_layer(0, ret, att, 0, 0, even_w_out[0], xf)

    sh1, sc1, _, _, _, _ = mods[1]
    x2, proj1 = _project(x1, norm1_g[1], sc1, sh1, odd_w_in[0].astype(BF16), seq, y=y0, gate=g2_0)
    sbo = _stick_breaking(proj1, bsz, seq)
    x3, y1, g2_1 = moe_layer(1, sbo, sbo, 0, 1, odd_w_out[0], x2)
    return _final(x3, y1, g2_1, seq).reshape(bsz, seq, d)
```

```python
import functools

import numpy as np
import jax
import jax.numpy as jnp
from jax import lax
from jax.experimental import pallas as pl
from jax.experimental.pallas import tpu as pltpu

F32 = jnp.float32
BF16 = jnp.bfloat16

D_MODEL = 1024
CHUNK = 64
EPS = 1e-6
ROPE_BASE = 10000.0
RET_HEADS = 4
RET_DIM = 128
ATT_HEADS = 8
ATT_DIM = 64
ATT_LEFT_CHUNKS = 8
REL_CLIP = 128
SB_HEADS = 16
SB_DIM = 64
N_EXPERTS = 16
N_GROUPS = 4
D_EXPERT = 512
HALF = 512

VMEM_LIMIT = 48 * 1024 * 1024
LANES = 128

ROW_TILE = 512
RET_TILE = 256
ATT_QB = 256
ATT_WIN = ATT_LEFT_CHUNKS * CHUNK + ATT_QB
SB_BLK = 128
SB_WIN = 3 * SB_BLK
SB_EXIT = -110.0

PAIR_A = (0, 0, 0, 1, 1, 2)
PAIR_B = (1, 2, 3, 2, 3, 3)
N_CLASS = N_GROUPS * len(PAIR_A)
CLASS_ROWS = 32
MOE_TILE = 256
EXT = D_MODEL + LANES


def _cparams(sem):
    return pltpu.CompilerParams(dimension_semantics=sem, vmem_limit_bytes=VMEM_LIMIT)


def _mod_kernel(c_ref, w_ref, b_ref, o_ref):
    c = c_ref[...]
    ca = c * jax.nn.sigmoid(c)
    o_ref[0] = jnp.dot(ca.astype(BF16), w_ref[0].astype(BF16), preferred_element_type=F32) + b_ref[0]


def _modulation(c, ada_w, ada_b):
    depth, _, width = ada_w.shape
    bsz = c.shape[0]
    rows = 8
    cp = jnp.zeros((rows, D_MODEL), F32).at[:bsz].set(c)
    tn = 1536
    out = pl.pallas_call(
        _mod_kernel,
        grid=(depth, width // tn),
        in_specs=[pl.BlockSpec((rows, D_MODEL), lambda l, j: (0, 0)),
                  pl.BlockSpec((1, D_MODEL, tn), lambda l, j: (l, 0, j)),
                  pl.BlockSpec((1, 1, tn), lambda l, j: (l, 0, j))],
        out_specs=pl.BlockSpec((1, rows, tn), lambda l, j: (l, 0, j)),
        out_shape=jax.ShapeDtypeStruct((depth, rows, width), F32),
        compiler_params=_cparams(("arbitrary", "arbitrary")),
        name="adaln_mod",
    )(cp, ada_w, ada_b.reshape(depth, 1, width))
    return out[:, :bsz]


def _norm_mod(x, g, sc, sh):
    ms = jnp.mean(x * x, axis=-1, keepdims=True)
    return (x * lax.rsqrt(ms + EPS) * g) * (1.0 + sc) + sh


def _project_chunks(h, w_ref, o_ref, qk_ref, bd_ref, q_scale):
    for ci, n0 in enumerate(range(0, o_ref.shape[1], HALF)):
        p = jnp.dot(h, w_ref[:, n0:n0 + HALF], preferred_element_type=F32)
        if qk_ref is not None and ci in (4, 5):
            ms = jnp.dot((p * p).astype(BF16), bd_ref[...], preferred_element_type=F32)
            p = p * lax.rsqrt(ms + EPS) * qk_ref[ci - 4:ci - 3, :]
            if ci == 4:
                p = p * q_scale
        elif qk_ref is None and n0 < SB_HEADS * SB_DIM:
            p = p * q_scale
        o_ref[:, n0:n0 + HALF] = p.astype(BF16)


def _proj_kernel(x_ref, g_ref, sc_ref, sh_ref, w_ref, qk_ref, bd_ref, o_ref):
    h = _norm_mod(x_ref[...], g_ref[...], sc_ref[0], sh_ref[0]).astype(BF16)
    _project_chunks(h, w_ref, o_ref, qk_ref, bd_ref, ATT_DIM ** -0.5)


def _proj_res_kernel(x_ref, y_ref, gt_ref, g_ref, sc_ref, sh_ref, w_ref, xo_ref, o_ref):
    x = x_ref[...] + gt_ref[0] * y_ref[...]
    xo_ref[...] = x
    h = _norm_mod(x, g_ref[...], sc_ref[0], sh_ref[0]).astype(BF16)
    _project_chunks(h, w_ref, o_ref, None, None, SB_DIM ** -0.5)


def _project(x, gain, sc, sh, w, seq, qk_gain=None, y=None, gate=None):
    n = x.shape[0]
    nout = w.shape[1]
    tpb = seq // ROW_TILE
    row = pl.BlockSpec((ROW_TILE, D_MODEL), lambda i: (i, 0))
    per_b = pl.BlockSpec((1, 1, D_MODEL), lambda i: (i // tpb, 0, 0))
    gspec = pl.BlockSpec((1, D_MODEL), lambda i: (0, 0))
    wspec = pl.BlockSpec((D_MODEL, nout), lambda i: (0, 0))
    ospec = pl.BlockSpec((ROW_TILE, nout), lambda i: (i, 0))
    oshape = jax.ShapeDtypeStruct((n, nout), BF16)
    if y is None:
        head = np.arange(HALF) // ATT_DIM
        bd = jnp.asarray((head[:, None] == head[None, :]).astype(np.float32) / ATT_DIM, BF16)
        qk = jnp.tile(qk_gain.astype(F32), (1, ATT_HEADS))
        return pl.pallas_call(
            _proj_kernel, grid=(n // ROW_TILE,),
            in_specs=[row, gspec, per_b, per_b, wspec,
                      pl.BlockSpec((2, HALF), lambda i: (0, 0)),
                      pl.BlockSpec((HALF, HALF), lambda i: (0, 0))],
            out_specs=ospec, out_shape=oshape,
            compiler_params=_cparams(("arbitrary",)), name="norm_proj",
        )(x, gain.reshape(1, D_MODEL), sc, sh, w, qk, bd)
    return pl.pallas_call(
        _proj_res_kernel, grid=(n // ROW_TILE,),
        in_specs=[row, row, per_b, gspec, per_b, per_b, wspec],
        out_specs=[row, ospec],
        out_shape=[jax.ShapeDtypeStruct((n, D_MODEL), F32), oshape],
        compiler_params=_cparams(("arbitrary",)), name="res_norm_proj",
    )(x, y, gate, gain.reshape(1, D_MODEL), sc, sh, w)


def _ret_tables(seq):
    inv = ROPE_BASE ** (-np.arange(0, RET_DIM, 2, dtype=np.float64) / RET_DIM)
    ang = np.arange(seq, dtype=np.float64)[:, None] * inv[None, :]
    cos = np.concatenate([np.cos(ang), np.cos(ang)], axis=1)
    sin = np.concatenate([-np.sin(ang), np.sin(ang)], axis=1)
    log_g = np.log(1.0 - 2.0 ** (-5.0 - np.arange(RET_HEADS, dtype=np.float64)))
    idx = np.arange(RET_TILE)
    same_or_earlier = (idx[None, :] // CHUNK) <= (idx[:, None] // CHUNK)
    dec = np.exp(log_g[:, None, None] * np.abs(idx[:, None] - idx[None, :])) * same_or_earlier
    loc = idx.astype(np.float64)
    qdec = np.exp(log_g[:, None] * (loc[None, :] + 1.0))
    kdec = np.exp(log_g[:, None] * (RET_TILE - 1.0 - loc[None, :]))
    tdec = np.exp(log_g * RET_TILE)
    qk = np.stack([qdec, kdec], axis=1)[..., None] * np.ones((1, 1, 1, RET_DIM))
    return (jnp.asarray(cos, F32), jnp.asarray(sin, F32), jnp.asarray(dec, F32),
            jnp.asarray(qk, F32), tuple(float(t) for t in tdec))


def _ret_kernel(tdec, q_ref, k_ref, v_ref, g_ref, cos_ref, sin_ref, dec_ref, qk_ref, o_ref, st_ref):
    @pl.when(pl.program_id(1) == 0)
    def _():
        st_ref[...] = jnp.zeros_like(st_ref)

    cos = cos_ref[...]
    sin = sin_ref[...]
    half = RET_DIM // 2
    for h in range(RET_HEADS):
        cols = slice(h * RET_DIM, (h + 1) * RET_DIM)
        q = q_ref[:, cols].astype(F32)
        k = k_ref[:, cols].astype(F32)
        q = q * cos + pltpu.roll(q, half, 1) * sin
        k = (k * cos + pltpu.roll(k, half, 1) * sin) * (RET_DIM ** -0.5)
        v = v_ref[:, cols]
        qb = q.astype(BF16)
        kb = k.astype(BF16)
        s = lax.dot_general(qb, kb, (((1,), (1,)), ((), ())), preferred_element_type=F32)
        s = s * dec_ref[h]
        o = jnp.dot(s.astype(BF16), v, preferred_element_type=F32)
        st = st_ref[h]
        o = o + jnp.dot((q * qk_ref[h, 0]).astype(BF16), st.astype(BF16), preferred_element_type=F32)
        kd = (k * qk_ref[h, 1]).astype(BF16)
        st_ref[h] = st * tdec[h] + lax.dot_general(kd, v, (((0,), (0,)), ((), ())),
                                                   preferred_element_type=F32)
        o = o * lax.rsqrt(jnp.mean(o * o, axis=-1, keepdims=True) + EPS)
        g = g_ref[:, cols].astype(F32)
        o_ref[:, cols] = (o * (g * jax.nn.sigmoid(g))).astype(BF16)


def _retention(proj, bsz, seq):
    n = proj.shape[0]
    nt = seq // RET_TILE
    cos, sin, dec, qk, tdec = _ret_tables(seq)

    def col(c):
        return pl.BlockSpec((RET_TILE, HALF), lambda b, i: (b * nt + i, c))

    pos = pl.BlockSpec((RET_TILE, RET_DIM), lambda b, i: (i, 0))
    return pl.pallas_call(
        functools.partial(_ret_kernel, tdec),
        grid=(bsz, nt),
        in_specs=[col(0), col(1), col(2), col(3), pos, pos,
                  pl.BlockSpec((RET_HEADS, RET_TILE, RET_TILE), lambda b, i: (0, 0, 0)),
                  pl.BlockSpec((RET_HEADS, 2, RET_TILE, RET_DIM), lambda b, i: (0, 0, 0, 0))],
        out_specs=pl.BlockSpec((RET_TILE, HALF), lambda b, i: (b * nt + i, 0)),
        out_shape=jax.ShapeDtypeStruct((n, HALF), BF16),
        scratch_shapes=[pltpu.VMEM((RET_HEADS, RET_DIM, RET_DIM), F32)],
        compiler_params=_cparams(("arbitrary", "arbitrary")), name="retention",
    )(proj, proj, proj, proj, cos, sin, dec, qk)


def _att_bias(rel_bias):
    span = ATT_WIN + ATT_QB
    m = np.concatenate([np.arange(0, ATT_WIN), np.zeros(1, np.int64), np.arange(-(ATT_QB - 1), 0)])
    idx = np.clip((ATT_WIN - ATT_QB) - m, -(CHUNK - 1), REL_CLIP) + (CHUNK - 1)
    vec = rel_bias[:, idx].astype(F32)
    heads = rel_bias.shape[0]
    toep = jnp.tile(vec, (1, ATT_QB))[:, :ATT_QB * (span - 1)].reshape(heads, ATT_QB, span - 1)
    qc = np.arange(ATT_QB)[:, None] // CHUNK
    kc = np.arange(ATT_WIN)[None, :] // CHUNK - ATT_LEFT_CHUNKS
    ok = (kc <= qc) & (kc >= qc - ATT_LEFT_CHUNKS)
    return jnp.where(jnp.asarray(ok)[None], toep[:, :, :ATT_WIN], -jnp.inf)


def _fill_padded(pad_ref, src_ref, front):
    pad_ref[:front, :] = jnp.zeros((front, pad_ref.shape[1]), pad_ref.dtype)
    pad_ref[front:, :] = src_ref[...]


def _att_kernel(q_ref, k_ref, v_ref, bias_ref, o_ref, kpad, vpad):
    i = pl.program_id(1)
    front = ATT_WIN - ATT_QB

    @pl.when(i == 0)
    def _():
        _fill_padded(kpad, k_ref, front)
        _fill_padded(vpad, v_ref, front)

    start = pl.multiple_of(i * ATT_QB, ATT_QB)
    col = lax.broadcasted_iota(jnp.int32, (1, ATT_WIN), 1)
    pen = jnp.where(col >= front - start, 0.0, -jnp.inf)
    first = lax.broadcasted_iota(jnp.int32, (ATT_QB, LANES), 1) < ATT_DIM
    nt = (((1,), (1,)), ((), ()))
    for p in range(ATT_HEADS * ATT_DIM // LANES):
        cols = slice(p * LANES, (p + 1) * LANES)
        q = q_ref[:, cols]
        kw = kpad[pl.ds(start, ATT_WIN), cols]
        vw = vpad[pl.ds(start, ATT_WIN), cols]
        zero = jnp.zeros_like(q)
        outs = []
        for hh in range(2):
            qm = jnp.where(first, q, zero) if hh == 0 else jnp.where(first, zero, q)
            s = lax.dot_general(qm, kw, nt, preferred_element_type=F32) + bias_ref[2 * p + hh] + pen
            e = jnp.exp(s - jnp.max(s, axis=-1, keepdims=True))
            den = jnp.sum(e, axis=-1, keepdims=True)
            outs.append(jnp.dot(e.astype(BF16), vw, preferred_element_type=F32) / den)
        o_ref[:, cols] = jnp.where(first, outs[0], outs[1]).astype(BF16)


def _chunk_attention(proj, rel_bias, bsz, seq):
    n = proj.shape[0]
    nq = seq // ATT_QB
    front = ATT_WIN - ATT_QB
    return pl.pallas_call(
        _att_kernel,
        grid=(bsz, nq),
        in_specs=[pl.BlockSpec((ATT_QB, HALF), lambda b, i: (b * nq + i, 4)),
                  pl.BlockSpec((seq, HALF), lambda b, i: (b, 5)),
                  pl.BlockSpec((seq, HALF), lambda b, i: (b, 6)),
                  pl.BlockSpec((ATT_HEADS, ATT_QB, ATT_WIN), lambda b, i: (0, 0, 0))],
        out_specs=pl.BlockSpec((ATT_QB, HALF), lambda b, i: (b * nq + i, 0)),
        out_shape=jax.ShapeDtypeStruct((n, HALF), BF16),
        scratch_shapes=[pltpu.VMEM((front + seq, HALF), BF16), pltpu.VMEM((front + seq, HALF), BF16)],
        compiler_params=_cparams(("arbitrary", "arbitrary")), name="chunk_attention",
    )(proj, proj, proj, _att_bias(rel_bias))


def _sb_logs(z):
    log_beta = jnp.minimum(z, 0.0) - jnp.log(1.0 + jnp.exp(-jnp.abs(z)))
    return log_beta, log_beta - z


def _sb_kernel(q_ref, k_ref, v_ref, tri_ref, o_ref, kpad, vpad, run_ref, acc_ref):
    iq = pl.program_id(2)
    front = SB_WIN - SB_BLK
    nblk = SB_WIN // SB_BLK
    pairs = HALF // LANES

    @pl.when(iq == 0)
    def _():
        _fill_padded(kpad, k_ref, front)
        _fill_padded(vpad, v_ref, front)

    start = pl.multiple_of(iq * SB_BLK, SB_BLK)
    row = lax.broadcasted_iota(jnp.int32, (SB_BLK, SB_WIN), 0)
    col = lax.broadcasted_iota(jnp.int32, (SB_BLK, SB_WIN), 1)
    mask = jnp.where(col - row < front, col, -1) >= jnp.maximum(front - start, 0)
    first = lax.broadcasted_iota(jnp.int32, (SB_BLK, LANES), 1) < SB_DIM
    nt = (((1,), (1,)), ((), ()))
    tri = tri_ref[...]

    def heads_of(p):
        q = q_ref[:, p * LANES:(p + 1) * LANES]
        zero = jnp.zeros_like(q)
        return jnp.where(first, q, zero), jnp.where(first, zero, q)

    alive = None
    for p in range(pairs):
        cols = slice(p * LANES, (p + 1) * LANES)
        kw = kpad[pl.ds(start, SB_WIN), cols]
        vw = vpad[pl.ds(start, SB_WIN), cols]
        for hh, qm in enumerate(heads_of(p)):
            z = lax.dot_general(qm, kw, nt, preferred_element_type=F32)
            log_beta, log_1mb = _sb_logs(z)
            l1 = jnp.where(mask, log_1mb, 0.0).astype(BF16)
            stacked = jnp.concatenate([l1[:, c * SB_BLK:(c + 1) * SB_BLK] for c in range(nblk)], axis=0)
            cs = jnp.dot(stacked, tri, preferred_element_type=F32)
            run = jnp.zeros((SB_BLK, SB_BLK), F32)
            accs = [None] * nblk
            for c in reversed(range(nblk)):
                blk = cs[c * SB_BLK:(c + 1) * SB_BLK]
                accs[c] = blk[:, :SB_BLK] + run
                run = run + blk[:, SB_BLK:]
            a = jnp.where(mask, jnp.exp(log_beta + jnp.concatenate(accs, axis=1)), 0.0)
            run_ref[2 * p + hh] = run
            acc_ref[2 * p + hh] = jnp.dot(a.astype(BF16), vw, preferred_element_type=F32)
            top = jnp.max(run)
            alive = top if alive is None else jnp.maximum(alive, top)

    def cond(c):
        return jnp.logical_and(c[0] >= 0, c[1] > SB_EXIT)

    def body(c):
        j = c[0]
        kstart = pl.multiple_of(front + j * SB_BLK, SB_BLK)
        alive = jnp.float32(-jnp.inf)
        for p in range(pairs):
            cols = slice(p * LANES, (p + 1) * LANES)
            kb = kpad[pl.ds(kstart, SB_BLK), cols]
            vb = vpad[pl.ds(kstart, SB_BLK), cols]
            for hh, qm in enumerate(heads_of(p)):
                z = lax.dot_general(qm, kb, nt, preferred_element_type=F32)
                log_beta, log_1mb = _sb_logs(z)
                cs = jnp.dot(log_1mb.astype(BF16), tri, preferred_element_type=F32)
                run = run_ref[2 * p + hh]
                a = jnp.exp(log_beta + run + cs[:, :SB_BLK])
                acc_ref[2 * p + hh] += jnp.dot(a.astype(BF16), vb, preferred_element_type=F32)
                run = run + cs[:, SB_BLK:]
                run_ref[2 * p + hh] = run
                alive = jnp.maximum(alive, jnp.max(run))
        return j - 1, alive

    lax.while_loop(cond, body, (iq - nblk, alive))
    for p in range(pairs):
        o_ref[:, p * LANES:(p + 1) * LANES] = jnp.where(first, acc_ref[2 * p],
                                                        acc_ref[2 * p + 1]).astype(BF16)


def _stick_breaking(proj, bsz, seq):
    n = proj.shape[0]
    nq = seq // SB_BLK
    groups = SB_HEADS * SB_DIM // HALF
    front = SB_WIN - SB_BLK
    j = np.arange(SB_BLK)
    tri = np.concatenate([(j[:, None] > j[None, :]).astype(np.float32),
                          np.ones((SB_BLK, SB_BLK), np.float32)], axis=1)
    return pl.pallas_call(
        _sb_kernel,
        grid=(bsz, groups, nq),
        in_specs=[pl.BlockSpec((SB_BLK, HALF), lambda b, g, i: (b * nq + i, g)),
                  pl.BlockSpec((seq, HALF), lambda b, g, i: (b, groups + g)),
                  pl.BlockSpec((seq, HALF), lambda b, g, i: (b, 2 * groups + g)),
                  pl.BlockSpec((SB_BLK, 2 * SB_BLK), lambda b, g, i: (0, 0))],
        out_specs=pl.BlockSpec((SB_BLK, HALF), lambda b, g, i: (b * nq + i, g)),
        out_shape=jax.ShapeDtypeStruct((n, SB_HEADS * SB_DIM), BF16),
        scratch_shapes=[pltpu.VMEM((front + seq, HALF), BF16), pltpu.VMEM((front + seq, HALF), BF16),
                        pltpu.VMEM((HALF // SB_DIM, SB_BLK, SB_BLK), F32),
                        pltpu.VMEM((HALF // SB_DIM, SB_BLK, LANES), F32)],
        compiler_params=_cparams(("arbitrary", "arbitrary", "arbitrary")), name="stick_breaking",
    )(proj, proj, proj, jnp.asarray(tri, BF16))


def _out_route_kernel(ma_ref, mb_ref, wa_ref, wb_ref, x_ref, g1_ref, g_ref, sc_ref, sh_ref,
                      rw_ref, rb_ref, tri_ref, xo_ref, he_ref, rt_ref, cnt_ref, carry_ref):
    @pl.when(pl.program_id(0) == 0)
    def _():
        carry_ref[...] = jnp.zeros_like(carry_ref)

    tm = x_ref.shape[0]
    mix = (jnp.dot(ma_ref[...], wa_ref[...], preferred_element_type=F32)
           + jnp.dot(mb_ref[...], wb_ref[...], preferred_element_type=F32))
    x = x_ref[...] + g1_ref[0] * mix
    xo_ref[...] = x
    h = _norm_mod(x, g_ref[...], sc_ref[0], sh_ref[0])
    he_ref[:, :D_MODEL] = h

    nt = (((1,), (1,)), ((), ()))
    h_hi = h.astype(BF16)
    h_lo = (h - h_hi.astype(F32)).astype(BF16)
    w = rw_ref[...]
    w_hi = w.astype(BF16)
    w_lo = (w - w_hi.astype(F32)).astype(BF16)
    logits = (lax.dot_general(w_hi, h_hi, nt, preferred_element_type=F32)
              + lax.dot_general(w_hi, h_lo, nt, preferred_element_type=F32)
              + lax.dot_general(w_lo, h_hi, nt, preferred_element_type=F32))
    score = jax.nn.sigmoid(logits)
    sel = score + rb_ref[...]

    best = None
    for g in range(N_GROUPS):
        for p in range(len(PAIR_A)):
            ea, eb = 4 * g + PAIR_A[p], 4 * g + PAIR_B[p]
            val = sel[ea:ea + 1, :] + sel[eb:eb + 1, :]
            cid = jnp.full((1, tm), float(len(PAIR_A) * g + p), F32)
            cand = (val, cid, score[ea:ea + 1, :], score[eb:eb + 1, :])
            if best is None:
                best = cand
            else:
                take = val > best[0]
                best = tuple(jnp.where(take, c_, b_) for c_, b_ in zip(cand, best))
    _, cls, sa, sb = best
    tot = sa + sb
    wgt_a = sa / tot
    wgt_b = sb / tot

    crow = lax.broadcasted_iota(jnp.int32, (CLASS_ROWS, tm), 0).astype(F32)
    onehot = jnp.where(crow == cls, 1.0, 0.0)
    before = jnp.dot(onehot.astype(BF16), tri_ref[...], preferred_element_type=F32)
    carry = carry_ref[...]
    rank = jnp.sum(onehot * (before + carry[:, :1]), axis=0, keepdims=True)
    carry = carry + jnp.sum(onehot, axis=1, keepdims=True)
    carry_ref[...] = carry
    cnt_ref[0] = carry

    srow = lax.broadcasted_iota(jnp.int32, (8, tm), 0)
    rt_ref[0] = jnp.where(srow == 0, cls, jnp.where(srow == 1, rank, 0.0))
    wrow = lax.broadcasted_iota(jnp.int32, (LANES, tm), 0)
    wtile = jnp.where(wrow == 0, wgt_a, jnp.where(wrow == 1, wgt_b, 0.0))
    he_ref[:, D_MODEL:] = wtile.T


def _out_route(mix_a, mix_b, col_a, col_b, w_out, x, g1, gain, sc, sh, router_w, router_b, seq):
    n = x.shape[0]
    tm = ROW_TILE
    nt = n // tm
    tpb = seq // tm
    t = np.arange(tm)
    tri = jnp.asarray((t[:, None] < t[None, :]).astype(np.float32), BF16)
    row = pl.BlockSpec((tm, D_MODEL), lambda i: (i, 0))
    per_b = pl.BlockSpec((1, 1, D_MODEL), lambda i: (i // tpb, 0, 0))
    rb = jnp.broadcast_to(router_b.astype(F32)[:, None], (N_EXPERTS, tm))
    return pl.pallas_call(
        _out_route_kernel,
        grid=(nt,),
        in_specs=[pl.BlockSpec((tm, HALF), lambda i: (i, col_a)),
                  pl.BlockSpec((tm, HALF), lambda i: (i, col_b)),
                  pl.BlockSpec((HALF, D_MODEL), lambda i: (0, 0)),
                  pl.BlockSpec((HALF, D_MODEL), lambda i: (1, 0)),
                  row, per_b,
                  pl.BlockSpec((1, D_MODEL), lambda i: (0, 0)), per_b, per_b,
                  pl.BlockSpec((N_EXPERTS, D_MODEL), lambda i: (0, 0)),
                  pl.BlockSpec((N_EXPERTS, tm), lambda i: (0, 0)),
                  pl.BlockSpec((tm, tm), lambda i: (0, 0))],
        out_specs=[row,
                   pl.BlockSpec((tm, EXT), lambda i: (i, 0)),
                   pl.BlockSpec((1, 8, tm), lambda i: (i, 0, 0)),
                   pl.BlockSpec((1, CLASS_ROWS, LANES), lambda i: (i, 0, 0))],
        out_shape=[jax.ShapeDtypeStruct((n, D_MODEL), F32),
                   jax.ShapeDtypeStruct((n, EXT), F32),
                   jax.ShapeDtypeStruct((nt, 8, tm), F32),
                   jax.ShapeDtypeStruct((nt, CLASS_ROWS, LANES), F32)],
        scratch_shapes=[pltpu.VMEM((CLASS_ROWS, LANES), F32)],
        compiler_params=_cparams(("arbitrary",)), name="out_proj_route",
    )(mix_a, mix_b, w_out, w_out, x, g1, gain.reshape(1, D_MODEL), sc, sh,
      router_w.T.astype(F32), rb, tri)


def _slot_kernel(pos_ref, inv_ref):
    n = pos_ref.shape[0]
    m = inv_ref.shape[0]

    def fill(s, c):
        inv_ref[s] = 0
        return c

    lax.fori_loop(0, m, fill, 0, unroll=8)

    def place(t, c):
        inv_ref[pos_ref[t]] = t
        return c

    lax.fori_loop(0, n, place, 0, unroll=8)


def _routing_plan(route, counts, n):
    nt, _, tm = route.shape
    cls = route[:, 0, :].reshape(n).astype(jnp.int32)
    rank = route[:, 1, :].reshape(n).astype(jnp.int32)
    cnt = counts[-1, :N_CLASS, 0].astype(jnp.int32)
    tiles_c = (cnt + MOE_TILE - 1) // MOE_TILE
    ends = jnp.cumsum(tiles_c)
    starts = ends - tiles_c
    pos = starts[cls] * MOE_TILE + rank
    max_tiles = n // MOE_TILE + N_CLASS
    total = ends[-1]
    j = jnp.arange(max_tiles, dtype=jnp.int32)
    jj = jnp.minimum(j, total - 1)
    tcls = jnp.sum((ends[None, :] <= jj[:, None]).astype(jnp.int32), axis=1)
    grp = tcls // len(PAIR_A)
    pair = tcls % len(PAIR_A)
    ea = 4 * grp + jnp.asarray(PAIR_A, jnp.int32)[pair]
    eb = 4 * grp + jnp.asarray(PAIR_B, jnp.int32)[pair]
    valid_rows = jnp.clip(cnt[tcls] - (jj - starts[tcls]) * MOE_TILE, 0, MOE_TILE)
    valid_rows = jnp.where(j < total, valid_rows, 0).astype(jnp.int32)
    inv = pl.pallas_call(
        _slot_kernel,
        in_specs=[pl.BlockSpec(memory_space=pltpu.SMEM)],
        out_specs=pl.BlockSpec(memory_space=pltpu.SMEM),
        out_shape=jax.ShapeDtypeStruct((max_tiles * MOE_TILE,), jnp.int32),
        name="moe_slots",
    )(pos)
    return inv, ea, eb, valid_rows, total.reshape(1)


def _moe_kernel(inv_ref, ea_ref, eb_ref, nv_ref, tot_ref, h_hbm, wg_a, wu_a, wd_a, wg_b, wu_b, wd_b,
                y_hbm, gbuf, ybuf, gsem, ssem):
    del ea_ref, eb_ref
    j = pl.program_id(0)
    total = tot_ref[0]
    slot = j % 2
    unroll = 8

    def gather_start(tile, s):
        base = tile * MOE_TILE

        def issue(r, c):
            pltpu.make_async_copy(h_hbm.at[pl.ds(inv_ref[base + r], 1), :],
                                  gbuf.at[s, pl.ds(r, 1), :], gsem.at[s]).start()
            return c

        lax.fori_loop(0, MOE_TILE, issue, 0, unroll=unroll)

    def gather_wait(s):
        pltpu.make_async_copy(h_hbm.at[pl.ds(0, MOE_TILE), :], gbuf.at[s], gsem.at[s]).wait()

    def scatter_start(tile, s):
        base = tile * MOE_TILE
        nv = nv_ref[tile]

        def issue(r):
            pltpu.make_async_copy(ybuf.at[s, pl.ds(r, 1), :],
                                  y_hbm.at[pl.ds(inv_ref[base + r], 1), :], ssem.at[s]).start()

        groups = nv // unroll

        def group(g, c):
            for u in range(unroll):
                issue(g * unroll + u)
            return c

        lax.fori_loop(0, groups, group, 0)
        for u in range(unroll):
            @pl.when(groups * unroll + u < nv)
            def _():
                issue(groups * unroll + u)

    def scatter_wait(tile, s):
        nv = nv_ref[tile]
        for bit in range(MOE_TILE.bit_length()):
            rows = 1 << bit

            @pl.when((nv >> bit) & 1 == 1)
            def _():
                pltpu.make_async_copy(ybuf.at[s, pl.ds(0, rows), :], y_hbm.at[pl.ds(0, rows), :],
                                      ssem.at[s]).wait()

    @pl.when(j == 0)
    def _():
        gather_start(0, 0)

    @pl.when(j < total)
    def _():
        gather_wait(slot)

        @pl.when(j + 1 < total)
        def _():
            gather_start(j + 1, 1 - slot)

        rows = gbuf[slot]
        xb = rows[:, :D_MODEL].astype(BF16)
        y = None
        for lane, (wg, wu, wd) in enumerate(((wg_a, wu_a, wd_a), (wg_b, wu_b, wd_b))):
            g = jnp.dot(xb, wg[0], preferred_element_type=F32)
            u = jnp.dot(xb, wu[0], preferred_element_type=F32)
            wgt = rows[:, D_MODEL + lane:D_MODEL + lane + 1]
            he = ((g * jax.nn.sigmoid(g)) * u * wgt).astype(BF16)
            part = jnp.dot(he, wd[0], preferred_element_type=F32)
            y = part if y is None else y + part

        @pl.when(j >= 2)
        def _():
            scatter_wait(j - 2, slot)

        ybuf[slot] = y
        scatter_start(j, slot)

        @pl.when(j == total - 1)
        def _():
            @pl.when(j >= 1)
            def _():
                scatter_wait(j - 1, 1 - slot)

            scatter_wait(j, slot)


def _moe(hext, inv, ea, eb, valid_rows, total, w_gate, w_up, w_down):
    n = hext.shape[0]
    max_tiles = inv.shape[0] // MOE_TILE

    def wspec(shape, which):
        if which == 0:
            return pl.BlockSpec((1,) + shape, lambda j, inv_, ea_, eb_, nv_, t_: (ea_[j], 0, 0))
        return pl.BlockSpec((1,) + shape, lambda j, inv_, ea_, eb_, nv_, t_: (eb_[j], 0, 0))

    up = (D_MODEL, D_EXPERT)
    down = (D_EXPERT, D_MODEL)
    grid_spec = pltpu.PrefetchScalarGridSpec(
        num_scalar_prefetch=5,
        grid=(max_tiles,),
        in_specs=[pl.BlockSpec(memory_space=pl.ANY),
                  wspec(up, 0), wspec(up, 0), wspec(down, 0),
                  wspec(up, 1), wspec(up, 1), wspec(down, 1)],
        out_specs=pl.BlockSpec(memory_space=pl.ANY),
        scratch_shapes=[pltpu.VMEM((2, MOE_TILE, EXT), F32),
                        pltpu.VMEM((2, MOE_TILE, D_MODEL), F32),
                        pltpu.SemaphoreType.DMA((2,)),
                        pltpu.SemaphoreType.DMA((2,))],
    )
    return pl.pallas_call(
        _moe_kernel,
        grid_spec=grid_spec,
        out_shape=jax.ShapeDtypeStruct((n, D_MODEL), F32),
        compiler_params=_cparams(("arbitrary",)), name="moe_experts",
    )(inv, ea, eb, valid_rows, total, hext, w_gate, w_up, w_down, w_gate, w_up, w_down)


def _final_kernel(x_ref, y_ref, g_ref, o_ref):
    o_ref[...] = x_ref[...] + g_ref[0] * y_ref[...]


def _final(x, y, gate, seq):
    n = x.shape[0]
    tpb = seq // ROW_TILE
    row = pl.BlockSpec((ROW_TILE, D_MODEL), lambda i: (i, 0))
    return pl.pallas_call(
        _final_kernel, grid=(n // ROW_TILE,),
        in_specs=[row, row, pl.BlockSpec((1, 1, D_MODEL), lambda i: (i // tpb, 0, 0))],
        out_specs=row, out_shape=jax.ShapeDtypeStruct((n, D_MODEL), F32),
        compiler_params=_cparams(("arbitrary",)), name="final_residual",
    )(x, y, gate)


def kernel(x, c, ada_w, ada_b, norm1_g, norm2_g, even_w_in, even_w_out, att_q_norm_g, att_k_norm_g,
           att_rel_bias, odd_w_in, odd_w_out, router_w, router_b, exp_w_gate, exp_w_up, exp_w_down):
    bsz, seq, d = x.shape
    n = bsz * seq
    mod = _modulation(c, ada_w, ada_b)
    mods = [[mod[l, :, k * d:(k + 1) * d].reshape(bsz, 1, d) for k in range(6)] for l in range(2)]
    xf = x.reshape(n, d)

    def moe_layer(layer, mix_a, mix_b, col_a, col_b, w_out, xin):
        sh1, sc1, g1, sh2, sc2, g2 = mods[layer]
        x1, hext, route, counts = _out_route(mix_a, mix_b, col_a, col_b, w_out.astype(BF16), xin, g1,
                                             norm2_g[layer], sc2, sh2, router_w, router_b, seq)
        inv, ea, eb, valid_rows, total = _routing_plan(route, counts, n)
        y = _moe(hext, inv, ea, eb, valid_rows, total, exp_w_gate[layer].astype(BF16),
                 exp_w_up[layer].astype(BF16), exp_w_down[layer].astype(BF16))
        return x1, y, g2

    sh1, sc1, _, _, _, _ = mods[0]
    qk_gain = jnp.stack([att_q_norm_g[0], att_k_norm_g[0]])
    proj0 = _project(xf, norm1_g[0], sc1, sh1, even_w_in[0].astype(BF16), seq, qk_gain=qk_gain)
    ret = _retention(proj0, bsz, seq)
    att = _chunk_attention(proj0, att_rel_bias[0], bsz, seq)
    x1, y0, g2_0 = moe_layer(0, ret, att, 0, 0, even_w_out[0], xf)

    sh1, sc1, _, _, _, _ = mods[1]
    x2, proj1 = _project(x1, norm1_g[1], sc1, sh1, odd_w_in[0].astype(BF16), seq, y=y0, gate=g2_0)
    sbo = _stick_breaking(proj1, bsz, seq)
    x3, y1, g2_1 = moe_layer(1, sbo, sbo, 0, 1, odd_w_out[0], x2)
    return _final(x3, y1, g2_1, seq).reshape(bsz, seq, d)
```

```python
import functools

import numpy as np
import jax
import jax.numpy as jnp
from jax import lax
from jax.experimental import pallas as pl
from jax.experimental.pallas import tpu as pltpu

F32 = jnp.float32
BF16 = jnp.bfloat16

D_MODEL = 1024
CHUNK = 64
EPS = 1e-6
ROPE_BASE = 10000.0
RET_HEADS = 4
RET_DIM = 128
ATT_HEADS = 8
ATT_DIM = 64
ATT_LEFT_CHUNKS = 8
REL_CLIP = 128
SB_HEADS = 16
SB_DIM = 64
N_EXPERTS = 16
N_GROUPS = 4
D_EXPERT = 512
HALF = 512

VMEM_LIMIT = 48 * 1024 * 1024
LANES = 128

ROW_TILE = 512
RET_TILE = 256
ATT_QB = 256
ATT_WIN = ATT_LEFT_CHUNKS * CHUNK + ATT_QB
SB_BLK = 128
SB_WIN = 3 * SB_BLK
SB_EXIT = -110.0

PAIR_A = (0, 0, 0, 1, 1, 2)
PAIR_B = (1, 2, 3, 2, 3, 3)
N_CLASS = N_GROUPS * len(PAIR_A)
CLASS_ROWS = 32
MOE_TILE = 256
ROW_SUB = D_MODEL // LANES
EXT_SUB = 2 * ROW_SUB


def _cparams(sem):
    return pltpu.CompilerParams(dimension_semantics=sem, vmem_limit_bytes=VMEM_LIMIT)


def _mod_kernel(c_ref, w_ref, b_ref, o_ref):
    c = c_ref[...]
    ca = c * jax.nn.sigmoid(c)
    o_ref[0] = jnp.dot(ca.astype(BF16), w_ref[0].astype(BF16), preferred_element_type=F32) + b_ref[0]


def _modulation(c, ada_w, ada_b):
    depth, _, width = ada_w.shape
    bsz = c.shape[0]
    rows = 8
    cp = jnp.zeros((rows, D_MODEL), F32).at[:bsz].set(c)
    tn = 1536
    out = pl.pallas_call(
        _mod_kernel,
        grid=(depth, width // tn),
        in_specs=[pl.BlockSpec((rows, D_MODEL), lambda l, j: (0, 0)),
                  pl.BlockSpec((1, D_MODEL, tn), lambda l, j: (l, 0, j)),
                  pl.BlockSpec((1, 1, tn), lambda l, j: (l, 0, j))],
        out_specs=pl.BlockSpec((1, rows, tn), lambda l, j: (l, 0, j)),
        out_shape=jax.ShapeDtypeStruct((depth, rows, width), F32),
        compiler_params=_cparams(("arbitrary", "arbitrary")),
        name="adaln_mod",
    )(cp, ada_w, ada_b.reshape(depth, 1, width))
    return out[:, :bsz]


def _norm_mod(x, g, sc, sh):
    ms = jnp.mean(x * x, axis=-1, keepdims=True)
    return (x * lax.rsqrt(ms + EPS) * g) * (1.0 + sc) + sh


def _project_chunks(h, w_ref, o_ref, qk_ref, bd_ref, q_scale):
    for ci, n0 in enumerate(range(0, o_ref.shape[1], HALF)):
        p = jnp.dot(h, w_ref[:, n0:n0 + HALF], preferred_element_type=F32)
        if qk_ref is not None and ci in (4, 5):
            ms = jnp.dot((p * p).astype(BF16), bd_ref[...], preferred_element_type=F32)
            p = p * lax.rsqrt(ms + EPS) * qk_ref[ci - 4:ci - 3, :]
            if ci == 4:
                p = p * q_scale
        elif qk_ref is None and n0 < SB_HEADS * SB_DIM:
            p = p * q_scale
        o_ref[:, n0:n0 + HALF] = p.astype(BF16)


def _proj_kernel(x_ref, g_ref, sc_ref, sh_ref, w_ref, qk_ref, bd_ref, o_ref):
    h = _norm_mod(x_ref[...], g_ref[...], sc_ref[0], sh_ref[0]).astype(BF16)
    _project_chunks(h, w_ref, o_ref, qk_ref, bd_ref, ATT_DIM ** -0.5)


def _token_rows(y_ref):
    rows = y_ref.shape[0] // ROW_SUB
    return jnp.concatenate([y_ref[pl.ds(jt, rows, stride=ROW_SUB), :] for jt in range(ROW_SUB)],
                           axis=1)


def _proj_res_kernel(x_ref, y_ref, gt_ref, g_ref, sc_ref, sh_ref, w_ref, xo_ref, o_ref):
    x = x_ref[...] + gt_ref[0] * _token_rows(y_ref)
    xo_ref[...] = x
    h = _norm_mod(x, g_ref[...], sc_ref[0], sh_ref[0]).astype(BF16)
    _project_chunks(h, w_ref, o_ref, None, None, SB_DIM ** -0.5)


def _project(x, gain, sc, sh, w, seq, qk_gain=None, y=None, gate=None):
    n = x.shape[0]
    nout = w.shape[1]
    tpb = seq // ROW_TILE
    row = pl.BlockSpec((ROW_TILE, D_MODEL), lambda i: (i, 0))
    per_b = pl.BlockSpec((1, 1, D_MODEL), lambda i: (i // tpb, 0, 0))
    gspec = pl.BlockSpec((1, D_MODEL), lambda i: (0, 0))
    wspec = pl.BlockSpec((D_MODEL, nout), lambda i: (0, 0))
    ospec = pl.BlockSpec((ROW_TILE, nout), lambda i: (i, 0))
    oshape = jax.ShapeDtypeStruct((n, nout), BF16)
    if y is None:
        head = np.arange(HALF) // ATT_DIM
        bd = jnp.asarray((head[:, None] == head[None, :]).astype(np.float32) / ATT_DIM, BF16)
        qk = jnp.tile(qk_gain.astype(F32), (1, ATT_HEADS))
        return pl.pallas_call(
            _proj_kernel, grid=(n // ROW_TILE,),
            in_specs=[row, gspec, per_b, per_b, wspec,
                      pl.BlockSpec((2, HALF), lambda i: (0, 0)),
                      pl.BlockSpec((HALF, HALF), lambda i: (0, 0))],
            out_specs=ospec, out_shape=oshape,
            compiler_params=_cparams(("arbitrary",)), name="norm_proj",
        )(x, gain.reshape(1, D_MODEL), sc, sh, w, qk, bd)
    return pl.pallas_call(
        _proj_res_kernel, grid=(n // ROW_TILE,),
        in_specs=[row, pl.BlockSpec((ROW_TILE * ROW_SUB, LANES), lambda i: (i, 0)), per_b, gspec,
                  per_b, per_b, wspec],
        out_specs=[row, ospec],
        out_shape=[jax.ShapeDtypeStruct((n, D_MODEL), F32), oshape],
        compiler_params=_cparams(("arbitrary",)), name="res_norm_proj",
    )(x, y, gate, gain.reshape(1, D_MODEL), sc, sh, w)


def _ret_tables(seq):
    inv = ROPE_BASE ** (-np.arange(0, RET_DIM, 2, dtype=np.float64) / RET_DIM)
    ang = np.arange(seq, dtype=np.float64)[:, None] * inv[None, :]
    cos = np.concatenate([np.cos(ang), np.cos(ang)], axis=1)
    sin = np.concatenate([-np.sin(ang), np.sin(ang)], axis=1)
    log_g = np.log(1.0 - 2.0 ** (-5.0 - np.arange(RET_HEADS, dtype=np.float64)))
    idx = np.arange(RET_TILE)
    same_or_earlier = (idx[None, :] // CHUNK) <= (idx[:, None] // CHUNK)
    dec = np.exp(log_g[:, None, None] * np.abs(idx[:, None] - idx[None, :])) * same_or_earlier
    loc = idx.astype(np.float64)
    qdec = np.exp(log_g[:, None] * (loc[None, :] + 1.0))
    kdec = np.exp(log_g[:, None] * (RET_TILE - 1.0 - loc[None, :]))
    tdec = np.exp(log_g * RET_TILE)
    qk = np.stack([qdec, kdec], axis=1)[..., None] * np.ones((1, 1, 1, RET_DIM))
    return (jnp.asarray(cos, F32), jnp.asarray(sin, F32), jnp.asarray(dec, F32),
            jnp.asarray(qk, F32), tuple(float(t) for t in tdec))


def _ret_kernel(tdec, q_ref, k_ref, v_ref, g_ref, cos_ref, sin_ref, dec_ref, qk_ref, o_ref, st_ref):
    @pl.when(pl.program_id(1) == 0)
    def _():
        st_ref[...] = jnp.zeros_like(st_ref)

    cos = cos_ref[...]
    sin = sin_ref[...]
    half = RET_DIM // 2
    for h in range(RET_HEADS):
        cols = slice(h * RET_DIM, (h + 1) * RET_DIM)
        q = q_ref[:, cols].astype(F32)
        k = k_ref[:, cols].astype(F32)
        q = q * cos + pltpu.roll(q, half, 1) * sin
        k = (k * cos + pltpu.roll(k, half, 1) * sin) * (RET_DIM ** -0.5)
        v = v_ref[:, cols]
        qb = q.astype(BF16)
        kb = k.astype(BF16)
        s = lax.dot_general(qb, kb, (((1,), (1,)), ((), ())), preferred_element_type=F32)
        s = s * dec_ref[h]
        o = jnp.dot(s.astype(BF16), v, preferred_element_type=F32)
        st = st_ref[h]
        o = o + jnp.dot((q * qk_ref[h, 0]).astype(BF16), st.astype(BF16), preferred_element_type=F32)
        kd = (k * qk_ref[h, 1]).astype(BF16)
        st_ref[h] = st * tdec[h] + lax.dot_general(kd, v, (((0,), (0,)), ((), ())),
                                                   preferred_element_type=F32)
        o = o * lax.rsqrt(jnp.mean(o * o, axis=-1, keepdims=True) + EPS)
        g = g_ref[:, cols].astype(F32)
        o_ref[:, cols] = (o * (g * jax.nn.sigmoid(g))).astype(BF16)


def _retention(proj, bsz, seq):
    n = proj.shape[0]
    nt = seq // RET_TILE
    cos, sin, dec, qk, tdec = _ret_tables(seq)

    def col(c):
        return pl.BlockSpec((RET_TILE, HALF), lambda b, i: (b * nt + i, c))

    pos = pl.BlockSpec((RET_TILE, RET_DIM), lambda b, i: (i, 0))
    return pl.pallas_call(
        functools.partial(_ret_kernel, tdec),
        grid=(bsz, nt),
        in_specs=[col(0), col(1), col(2), col(3), pos, pos,
                  pl.BlockSpec((RET_HEADS, RET_TILE, RET_TILE), lambda b, i: (0, 0, 0)),
                  pl.BlockSpec((RET_HEADS, 2, RET_TILE, RET_DIM), lambda b, i: (0, 0, 0, 0))],
        out_specs=pl.BlockSpec((RET_TILE, HALF), lambda b, i: (b * nt + i, 0)),
        out_shape=jax.ShapeDtypeStruct((n, HALF), BF16),
        scratch_shapes=[pltpu.VMEM((RET_HEADS, RET_DIM, RET_DIM), F32)],
        compiler_params=_cparams(("arbitrary", "arbitrary")), name="retention",
    )(proj, proj, proj, proj, cos, sin, dec, qk)


def _att_bias(rel_bias):
    span = ATT_WIN + ATT_QB
    m = np.concatenate([np.arange(0, ATT_WIN), np.zeros(1, np.int64), np.arange(-(ATT_QB - 1), 0)])
    idx = np.clip((ATT_WIN - ATT_QB) - m, -(CHUNK - 1), REL_CLIP) + (CHUNK - 1)
    vec = rel_bias[:, idx].astype(F32)
    heads = rel_bias.shape[0]
    toep = jnp.tile(vec, (1, ATT_QB))[:, :ATT_QB * (span - 1)].reshape(heads, ATT_QB, span - 1)
    qc = np.arange(ATT_QB)[:, None] // CHUNK
    kc = np.arange(ATT_WIN)[None, :] // CHUNK - ATT_LEFT_CHUNKS
    ok = (kc <= qc) & (kc >= qc - ATT_LEFT_CHUNKS)
    return jnp.where(jnp.asarray(ok)[None], toep[:, :, :ATT_WIN], -jnp.inf)


def _fill_padded(pad_ref, src_ref, front):
    pad_ref[:front, :] = jnp.zeros((front, pad_ref.shape[1]), pad_ref.dtype)
    pad_ref[front:, :] = src_ref[...]


def _att_kernel(q_ref, k_ref, v_ref, bias_ref, o_ref, kpad, vpad):
    i = pl.program_id(1)
    front = ATT_WIN - ATT_QB

    @pl.when(i == 0)
    def _():
        _fill_padded(kpad, k_ref, front)
        _fill_padded(vpad, v_ref, front)

    start = pl.multiple_of(i * ATT_QB, ATT_QB)
    col = lax.broadcasted_iota(jnp.int32, (1, ATT_WIN), 1)
    pen = jnp.where(col >= front - start, 0.0, -jnp.inf)
    first = lax.broadcasted_iota(jnp.int32, (ATT_QB, LANES), 1) < ATT_DIM
    nt = (((1,), (1,)), ((), ()))
    for p in range(ATT_HEADS * ATT_DIM // LANES):
        cols = slice(p * LANES, (p + 1) * LANES)
        q = q_ref[:, cols]
        kw = kpad[pl.ds(start, ATT_WIN), cols]
        vw = vpad[pl.ds(start, ATT_WIN), cols]
        zero = jnp.zeros_like(q)
        outs = []
        for hh in range(2):
            qm = jnp.where(first, q, zero) if hh == 0 else jnp.where(first, zero, q)
            s = lax.dot_general(qm, kw, nt, preferred_element_type=F32) + bias_ref[2 * p + hh] + pen
            e = jnp.exp(s - jnp.max(s, axis=-1, keepdims=True))
            den = jnp.sum(e, axis=-1, keepdims=True)
            outs.append(jnp.dot(e.astype(BF16), vw, preferred_element_type=F32) / den)
        o_ref[:, cols] = jnp.where(first, outs[0], outs[1]).astype(BF16)


def _chunk_attention(proj, rel_bias, bsz, seq):
    n = proj.shape[0]
    nq = seq // ATT_QB
    front = ATT_WIN - ATT_QB
    return pl.pallas_call(
        _att_kernel,
        grid=(bsz, nq),
        in_specs=[pl.BlockSpec((ATT_QB, HALF), lambda b, i: (b * nq + i, 4)),
                  pl.BlockSpec((seq, HALF), lambda b, i: (b, 5)),
                  pl.BlockSpec((seq, HALF), lambda b, i: (b, 6)),
                  pl.BlockSpec((ATT_HEADS, ATT_QB, ATT_WIN), lambda b, i: (0, 0, 0))],
        out_specs=pl.BlockSpec((ATT_QB, HALF), lambda b, i: (b * nq + i, 0)),
        out_shape=jax.ShapeDtypeStruct((n, HALF), BF16),
        scratch_shapes=[pltpu.VMEM((front + seq, HALF), BF16), pltpu.VMEM((front + seq, HALF), BF16)],
        compiler_params=_cparams(("arbitrary", "arbitrary")), name="chunk_attention",
    )(proj, proj, proj, _att_bias(rel_bias))


def _sb_logs(z):
    log_beta = jnp.minimum(z, 0.0) - jnp.log(1.0 + jnp.exp(-jnp.abs(z)))
    return log_beta, log_beta - z


def _sb_kernel(q_ref, k_ref, v_ref, tri_ref, o_ref, kpad, vpad, run_ref, acc_ref):
    iq = pl.program_id(2)
    front = SB_WIN - SB_BLK
    nblk = SB_WIN // SB_BLK
    pairs = HALF // LANES

    @pl.when(iq == 0)
    def _():
        _fill_padded(kpad, k_ref, front)
        _fill_padded(vpad, v_ref, front)

    start = pl.multiple_of(iq * SB_BLK, SB_BLK)
    row = lax.broadcasted_iota(jnp.int32, (SB_BLK, SB_WIN), 0)
    col = lax.broadcasted_iota(jnp.int32, (SB_BLK, SB_WIN), 1)
    mask = jnp.where(col - row < front, col, -1) >= jnp.maximum(front - start, 0)
    first = lax.broadcasted_iota(jnp.int32, (SB_BLK, LANES), 1) < SB_DIM
    nt = (((1,), (1,)), ((), ()))
    tri = tri_ref[...]

    def heads_of(p):
        q = q_ref[:, p * LANES:(p + 1) * LANES]
        zero = jnp.zeros_like(q)
        return jnp.where(first, q, zero), jnp.where(first, zero, q)

    alive = None
    for p in range(pairs):
        cols = slice(p * LANES, (p + 1) * LANES)
        kw = kpad[pl.ds(start, SB_WIN), cols]
        vw = vpad[pl.ds(start, SB_WIN), cols]
        for hh, qm in enumerate(heads_of(p)):
            z = lax.dot_general(qm, kw, nt, preferred_element_type=F32)
            log_beta, log_1mb = _sb_logs(z)
            l1 = jnp.where(mask, log_1mb, 0.0).astype(BF16)
            stacked = jnp.concatenate([l1[:, c * SB_BLK:(c + 1) * SB_BLK] for c in range(nblk)], axis=0)
            cs = jnp.dot(stacked, tri, preferred_element_type=F32)
            run = jnp.zeros((SB_BLK, SB_BLK), F32)
            accs = [None] * nblk
            for c in reversed(range(nblk)):
                blk = cs[c * SB_BLK:(c + 1) * SB_BLK]
                accs[c] = blk[:, :SB_BLK] + run
                run = run + blk[:, SB_BLK:]
            a = jnp.where(mask, jnp.exp(log_beta + jnp.concatenate(accs, axis=1)), 0.0)
            run_ref[2 * p + hh] = run
            acc_ref[2 * p + hh] = jnp.dot(a.astype(BF16), vw, preferred_element_type=F32)
            alive = run if alive is None else jnp.maximum(alive, run)

    def cond(c):
        return jnp.logical_and(c[0] >= 0, c[1] > SB_EXIT)

    def body(c):
        j = c[0]
        kstart = pl.multiple_of(front + j * SB_BLK, SB_BLK)
        alive = None
        for p in range(pairs):
            cols = slice(p * LANES, (p + 1) * LANES)
            kb = kpad[pl.ds(kstart, SB_BLK), cols]
            vb = vpad[pl.ds(kstart, SB_BLK), cols]
            for hh, qm in enumerate(heads_of(p)):
                z = lax.dot_general(qm, kb, nt, preferred_element_type=F32)
                log_beta, log_1mb = _sb_logs(z)
                cs = jnp.dot(log_1mb.astype(BF16), tri, preferred_element_type=F32)
                run = run_ref[2 * p + hh]
                a = jnp.exp(log_beta + run + cs[:, :SB_BLK])
                acc_ref[2 * p + hh] += jnp.dot(a.astype(BF16), vb, preferred_element_type=F32)
                run = run + cs[:, SB_BLK:]
                run_ref[2 * p + hh] = run
                alive = run if alive is None else jnp.maximum(alive, run)
        return j - 1, jnp.max(alive)

    lax.while_loop(cond, body, (iq - nblk, jnp.max(alive)))
    for p in range(pairs):
        o_ref[:, p * LANES:(p + 1) * LANES] = jnp.where(first, acc_ref[2 * p],
                                                        acc_ref[2 * p + 1]).astype(BF16)


def _stick_breaking(proj, bsz, seq):
    n = proj.shape[0]
    nq = seq // SB_BLK
    groups = SB_HEADS * SB_DIM // HALF
    front = SB_WIN - SB_BLK
    j = np.arange(SB_BLK)
    tri = np.concatenate([(j[:, None] > j[None, :]).astype(np.float32),
                          np.ones((SB_BLK, SB_BLK), np.float32)], axis=1)
    return pl.pallas_call(
        _sb_kernel,
        grid=(bsz, groups, nq),
        in_specs=[pl.BlockSpec((SB_BLK, HALF), lambda b, g, i: (b * nq + i, g)),
                  pl.BlockSpec((seq, HALF), lambda b, g, i: (b, groups + g)),
                  pl.BlockSpec((seq, HALF), lambda b, g, i: (b, 2 * groups + g)),
                  pl.BlockSpec((SB_BLK, 2 * SB_BLK), lambda b, g, i: (0, 0))],
        out_specs=pl.BlockSpec((SB_BLK, HALF), lambda b, g, i: (b * nq + i, g)),
        out_shape=jax.ShapeDtypeStruct((n, SB_HEADS * SB_DIM), BF16),
        scratch_shapes=[pltpu.VMEM((front + seq, HALF), BF16), pltpu.VMEM((front + seq, HALF), BF16),
                        pltpu.VMEM((HALF // SB_DIM, SB_BLK, SB_BLK), F32),
                        pltpu.VMEM((HALF // SB_DIM, SB_BLK, LANES), F32)],
        compiler_params=_cparams(("arbitrary", "arbitrary", "arbitrary")), name="stick_breaking",
    )(proj, proj, proj, jnp.asarray(tri, BF16))


def _out_route_kernel(ma_ref, mb_ref, wa_ref, wb_ref, x_ref, g1_ref, g_ref, sc_ref, sh_ref,
                      rw_ref, rb_ref, tri_ref, xo_ref, he_ref, rt_ref, cnt_ref, carry_ref):
    @pl.when(pl.program_id(0) == 0)
    def _():
        carry_ref[...] = jnp.zeros_like(carry_ref)

    tm = x_ref.shape[0]
    mix = (jnp.dot(ma_ref[...], wa_ref[...], preferred_element_type=F32)
           + jnp.dot(mb_ref[...], wb_ref[...], preferred_element_type=F32))
    x = x_ref[...] + g1_ref[0] * mix
    xo_ref[...] = x
    h = _norm_mod(x, g_ref[...], sc_ref[0], sh_ref[0])
    for jt in range(ROW_SUB):
        he_ref[pl.ds(jt, tm, stride=EXT_SUB), :] = h[:, jt * LANES:(jt + 1) * LANES]

    nt = (((1,), (1,)), ((), ()))
    h_hi = h.astype(BF16)
    h_lo = (h - h_hi.astype(F32)).astype(BF16)
    w = rw_ref[...]
    w_hi = w.astype(BF16)
    w_lo = (w - w_hi.astype(F32)).astype(BF16)
    logits = (lax.dot_general(w_hi, h_hi, nt, preferred_element_type=F32)
              + lax.dot_general(w_hi, h_lo, nt, preferred_element_type=F32)
              + lax.dot_general(w_lo, h_hi, nt, preferred_element_type=F32))
    score = jax.nn.sigmoid(logits)
    sel = score + rb_ref[...]

    best = None
    for g in range(N_GROUPS):
        for p in range(len(PAIR_A)):
            ea, eb = 4 * g + PAIR_A[p], 4 * g + PAIR_B[p]
            val = sel[ea:ea + 1, :] + sel[eb:eb + 1, :]
            cid = jnp.full((1, tm), float(len(PAIR_A) * g + p), F32)
            cand = (val, cid, score[ea:ea + 1, :], score[eb:eb + 1, :])
            if best is None:
                best = cand
            else:
                take = val > best[0]
                best = tuple(jnp.where(take, c_, b_) for c_, b_ in zip(cand, best))
    _, cls, sa, sb = best
    tot = sa + sb
    wgt_a = sa / tot
    wgt_b = sb / tot

    crow = lax.broadcasted_iota(jnp.int32, (CLASS_ROWS, tm), 0).astype(F32)
    onehot = jnp.where(crow == cls, 1.0, 0.0)
    before = jnp.dot(onehot.astype(BF16), tri_ref[...], preferred_element_type=F32)
    carry = carry_ref[...]
    rank = jnp.sum(onehot * (before + carry[:, :1]), axis=0, keepdims=True)
    carry = carry + jnp.sum(onehot, axis=1, keepdims=True)
    carry_ref[...] = carry
    cnt_ref[0] = carry

    srow = lax.broadcasted_iota(jnp.int32, (8, tm), 0)
    rt_ref[0] = jnp.where(srow == 0, cls, jnp.where(srow == 1, rank, 0.0))
    wrow = lax.broadcasted_iota(jnp.int32, (LANES, tm), 0)
    wtile = jnp.where(wrow == 0, wgt_a, jnp.where(wrow == 1, wgt_b, 0.0))
    wt = wtile.T
    for jt in range(ROW_SUB, EXT_SUB):
        he_ref[pl.ds(jt, tm, stride=EXT_SUB), :] = wt


def _out_route(mix_a, mix_b, col_a, col_b, w_out, x, g1, gain, sc, sh, router_w, router_b, seq):
    n = x.shape[0]
    tm = ROW_TILE
    nt = n // tm
    tpb = seq // tm
    t = np.arange(tm)
    tri = jnp.asarray((t[:, None] < t[None, :]).astype(np.float32), BF16)
    row = pl.BlockSpec((tm, D_MODEL), lambda i: (i, 0))
    per_b = pl.BlockSpec((1, 1, D_MODEL), lambda i: (i // tpb, 0, 0))
    rb = jnp.broadcast_to(router_b.astype(F32)[:, None], (N_EXPERTS, tm))
    return pl.pallas_call(
        _out_route_kernel,
        grid=(nt,),
        in_specs=[pl.BlockSpec((tm, HALF), lambda i: (i, col_a)),
                  pl.BlockSpec((tm, HALF), lambda i: (i, col_b)),
                  pl.BlockSpec((HALF, D_MODEL), lambda i: (0, 0)),
                  pl.BlockSpec((HALF, D_MODEL), lambda i: (1, 0)),
                  row, per_b,
                  pl.BlockSpec((1, D_MODEL), lambda i: (0, 0)), per_b, per_b,
                  pl.BlockSpec((N_EXPERTS, D_MODEL), lambda i: (0, 0)),
                  pl.BlockSpec((N_EXPERTS, tm), lambda i: (0, 0)),
                  pl.BlockSpec((tm, tm), lambda i: (0, 0))],
        out_specs=[row,
                   pl.BlockSpec((tm * EXT_SUB, LANES), lambda i: (i, 0)),
                   pl.BlockSpec((1, 8, tm), lambda i: (i, 0, 0)),
                   pl.BlockSpec((1, CLASS_ROWS, LANES), lambda i: (i, 0, 0))],
        out_shape=[jax.ShapeDtypeStruct((n, D_MODEL), F32),
                   jax.ShapeDtypeStruct((n * EXT_SUB, LANES), F32),
                   jax.ShapeDtypeStruct((nt, 8, tm), F32),
                   jax.ShapeDtypeStruct((nt, CLASS_ROWS, LANES), F32)],
        scratch_shapes=[pltpu.VMEM((CLASS_ROWS, LANES), F32)],
        compiler_params=_cparams(("arbitrary",)), name="out_proj_route",
    )(mix_a, mix_b, w_out, w_out, x, g1, gain.reshape(1, D_MODEL), sc, sh,
      router_w.T.astype(F32), rb, tri)


def _slot_kernel(pos_ref, inv_ref):
    n = pos_ref.shape[0]
    m = inv_ref.shape[0]

    def fill(s, c):
        inv_ref[s] = 0
        return c

    lax.fori_loop(0, m, fill, 0, unroll=8)

    def place(t, c):
        inv_ref[pos_ref[t]] = t
        return c

    lax.fori_loop(0, n, place, 0, unroll=8)


def _routing_plan(route, counts, n):
    nt, _, tm = route.shape
    cls = route[:, 0, :].reshape(n).astype(jnp.int32)
    rank = route[:, 1, :].reshape(n).astype(jnp.int32)
    cnt = counts[-1, :N_CLASS, 0].astype(jnp.int32)
    tiles_c = (cnt + MOE_TILE - 1) // MOE_TILE
    ends = jnp.cumsum(tiles_c)
    starts = ends - tiles_c
    pos = starts[cls] * MOE_TILE + rank
    max_tiles = n // MOE_TILE + N_CLASS
    total = ends[-1]
    j = jnp.arange(max_tiles, dtype=jnp.int32)
    jj = jnp.minimum(j, total - 1)
    tcls = jnp.sum((ends[None, :] <= jj[:, None]).astype(jnp.int32), axis=1)
    grp = tcls // len(PAIR_A)
    pair = tcls % len(PAIR_A)
    ea = 4 * grp + jnp.asarray(PAIR_A, jnp.int32)[pair]
    eb = 4 * grp + jnp.asarray(PAIR_B, jnp.int32)[pair]
    valid_rows = jnp.clip(cnt[tcls] - (jj - starts[tcls]) * MOE_TILE, 0, MOE_TILE)
    valid_rows = jnp.where(j < total, valid_rows, 0).astype(jnp.int32)
    inv = pl.pallas_call(
        _slot_kernel,
        in_specs=[pl.BlockSpec(memory_space=pltpu.SMEM)],
        out_specs=pl.BlockSpec(memory_space=pltpu.SMEM),
        out_shape=jax.ShapeDtypeStruct((max_tiles * MOE_TILE,), jnp.int32),
        name="moe_slots",
    )(pos)
    return inv, ea, eb, valid_rows, total.reshape(1)


def _moe_kernel(inv_ref, ea_ref, eb_ref, nv_ref, tot_ref, h_hbm, wg_a, wu_a, wd_a, wg_b, wu_b, wd_b,
                y_hbm, gbuf, ybuf, gsem, ssem):
    del ea_ref, eb_ref
    j = pl.program_id(0)
    total = tot_ref[0]
    slot = j % 2
    unroll = 8

    def gather_start(tile, s):
        base = tile * MOE_TILE

        def issue(r, c):
            src = pl.multiple_of(inv_ref[base + r] * EXT_SUB, EXT_SUB)
            dst = pl.multiple_of(r * EXT_SUB, EXT_SUB)
            pltpu.make_async_copy(h_hbm.at[pl.ds(src, EXT_SUB), :], gbuf.at[s, pl.ds(dst, EXT_SUB), :],
                                  gsem.at[s]).start()
            return c

        lax.fori_loop(0, MOE_TILE, issue, 0, unroll=unroll)

    def gather_wait(s):
        pltpu.make_async_copy(h_hbm.at[pl.ds(0, MOE_TILE * EXT_SUB), :], gbuf.at[s],
                              gsem.at[s]).wait()

    def scatter_start(tile, s):
        base = tile * MOE_TILE
        nv = nv_ref[tile]

        def issue(r):
            src = pl.multiple_of(r * ROW_SUB, ROW_SUB)
            dst = pl.multiple_of(inv_ref[base + r] * ROW_SUB, ROW_SUB)
            pltpu.make_async_copy(ybuf.at[s, pl.ds(src, ROW_SUB), :], y_hbm.at[pl.ds(dst, ROW_SUB), :],
                                  ssem.at[s]).start()

        groups = nv // unroll

        def group(g, c):
            for u in range(unroll):
                issue(g * unroll + u)
            return c

        lax.fori_loop(0, groups, group, 0)
        for u in range(unroll):
            @pl.when(groups * unroll + u < nv)
            def _():
                issue(groups * unroll + u)

    def scatter_wait(tile, s):
        nv = nv_ref[tile]
        for bit in range(MOE_TILE.bit_length()):
            rows = 1 << bit

            @pl.when((nv >> bit) & 1 == 1)
            def _():
                pltpu.make_async_copy(ybuf.at[s, pl.ds(0, rows * ROW_SUB), :],
                                      y_hbm.at[pl.ds(0, rows * ROW_SUB), :], ssem.at[s]).wait()

    @pl.when(j == 0)
    def _():
        gather_start(0, 0)

    @pl.when(j < total)
    def _():
        gather_wait(slot)

        @pl.when(j + 1 < total)
        def _():
            gather_start(j + 1, 1 - slot)

        xb = jnp.concatenate([gbuf[slot, pl.ds(jt, MOE_TILE, stride=EXT_SUB), :]
                              for jt in range(ROW_SUB)], axis=1).astype(BF16)
        wrow = gbuf[slot, pl.ds(ROW_SUB, MOE_TILE, stride=EXT_SUB), :]
        y = None
        for lane, (wg, wu, wd) in enumerate(((wg_a, wu_a, wd_a), (wg_b, wu_b, wd_b))):
            g = jnp.dot(xb, wg[0], preferred_element_type=F32)
            u = jnp.dot(xb, wu[0], preferred_element_type=F32)
            wgt = wrow[:, lane:lane + 1]
            he = ((g * jax.nn.sigmoid(g)) * u * wgt).astype(BF16)
            part = jnp.dot(he, wd[0], preferred_element_type=F32)
            y = part if y is None else y + part

        @pl.when(j >= 2)
        def _():
            scatter_wait(j - 2, slot)

        for jt in range(ROW_SUB):
            chunk = y[:, jt * LANES:(jt + 1) * LANES]
            ybuf[slot, pl.ds(jt, MOE_TILE, stride=ROW_SUB), :] = chunk
        scatter_start(j, slot)

        @pl.when(j == total - 1)
        def _():
            @pl.when(j >= 1)
            def _():
                scatter_wait(j - 1, 1 - slot)

            scatter_wait(j, slot)


def _moe(hext, inv, ea, eb, valid_rows, total, layer, w_gate, w_up, w_down):
    n = hext.shape[0] // EXT_SUB
    max_tiles = inv.shape[0] // MOE_TILE

    def wspec(shape, which):
        if which == 0:
            return pl.BlockSpec((None, 1) + shape,
                                lambda j, inv_, ea_, eb_, nv_, t_: (layer, ea_[j], 0, 0))
        return pl.BlockSpec((None, 1) + shape, lambda j, inv_, ea_, eb_, nv_, t_: (layer, eb_[j], 0, 0))

    up = (D_MODEL, D_EXPERT)
    down = (D_EXPERT, D_MODEL)
    grid_spec = pltpu.PrefetchScalarGridSpec(
        num_scalar_prefetch=5,
        grid=(max_tiles,),
        in_specs=[pl.BlockSpec(memory_space=pl.ANY),
                  wspec(up, 0), wspec(up, 0), wspec(down, 0),
                  wspec(up, 1), wspec(up, 1), wspec(down, 1)],
        out_specs=pl.BlockSpec(memory_space=pl.ANY),
        scratch_shapes=[pltpu.VMEM((2, MOE_TILE * EXT_SUB, LANES), F32),
                        pltpu.VMEM((2, MOE_TILE * ROW_SUB, LANES), F32),
                        pltpu.SemaphoreType.DMA((2,)),
                        pltpu.SemaphoreType.DMA((2,))],
    )
    return pl.pallas_call(
        _moe_kernel,
        grid_spec=grid_spec,
        out_shape=jax.ShapeDtypeStruct((n * ROW_SUB, LANES), F32),
        compiler_params=_cparams(("arbitrary",)), name="moe_experts",
    )(inv, ea, eb, valid_rows, total, hext, w_gate, w_up, w_down, w_gate, w_up, w_down)


def _final_kernel(x_ref, y_ref, g_ref, o_ref):
    o_ref[...] = x_ref[...] + g_ref[0] * _token_rows(y_ref)


def _final(x, y, gate, seq):
    n = x.shape[0]
    tpb = seq // ROW_TILE
    row = pl.BlockSpec((ROW_TILE, D_MODEL), lambda i: (i, 0))
    return pl.pallas_call(
        _final_kernel, grid=(n // ROW_TILE,),
        in_specs=[row, pl.BlockSpec((ROW_TILE * ROW_SUB, LANES), lambda i: (i, 0)),
                  pl.BlockSpec((1, 1, D_MODEL), lambda i: (i // tpb, 0, 0))],
        out_specs=row, out_shape=jax.ShapeDtypeStruct((n, D_MODEL), F32),
        compiler_params=_cparams(("arbitrary",)), name="final_residual",
    )(x, y, gate)


def kernel(x, c, ada_w, ada_b, norm1_g, norm2_g, even_w_in, even_w_out, att_q_norm_g, att_k_norm_g,
           att_rel_bias, odd_w_in, odd_w_out, router_w, router_b, exp_w_gate, exp_w_up, exp_w_down):
    bsz, seq, d = x.shape
    n = bsz * seq
    mod = _modulation(c, ada_w, ada_b)
    mods = [[mod[l, :, k * d:(k + 1) * d].reshape(bsz, 1, d) for k in range(6)] for l in range(2)]
    xf = x.reshape(n, d)
    w_gate, w_up, w_down = (w.astype(BF16) for w in (exp_w_gate, exp_w_up, exp_w_down))

    def moe_layer(layer, mix_a, mix_b, col_a, col_b, w_out, xin):
        sh1, sc1, g1, sh2, sc2, g2 = mods[layer]
        x1, hext, route, counts = _out_route(mix_a, mix_b, col_a, col_b, w_out.astype(BF16), xin, g1,
                                             norm2_g[layer], sc2, sh2, router_w, router_b, seq)
        inv, ea, eb, valid_rows, total = _routing_plan(route, counts, n)
        y = _moe(hext, inv, ea, eb, valid_rows, total, layer, w_gate, w_up, w_down)
        return x1, y, g2

    sh1, sc1, _, _, _, _ = mods[0]
    qk_gain = jnp.stack([att_q_norm_g[0], att_k_norm_g[0]])
    proj0 = _project(xf, norm1_g[0], sc1, sh1, even_w_in[0].astype(BF16), seq, qk_gain=qk_gain)
    ret = _retention(proj0, bsz, seq)
    att = _chunk_attention(proj0, att_rel_bias[0], bsz, seq)
    x1, y0, g2_0 = moe_layer(0, ret, att, 0, 0, even_w_out[0], xf)

    sh1, sc1, _, _, _, _ = mods[1]
    x2, proj1 = _project(x1, norm1_g[1], sc1, sh1, odd_w_in[0].astype(BF16), seq, y=y0, gate=g2_0)
    sbo = _stick_breaking(proj1, bsz, seq)
    x3, y1, g2_1 = moe_layer(1, sbo, sbo, 0, 1, odd_w_out[0], x2)
    return _final(x3, y1, g2_1, seq).reshape(bsz, seq, d)
```

```python
import functools

import numpy as np
import jax
import jax.numpy as jnp
from jax import lax
from jax.experimental import pallas as pl
from jax.experimental.pallas import tpu as pltpu

F32 = jnp.float32
BF16 = jnp.bfloat16

D_MODEL = 1024
CHUNK = 64
EPS = 1e-6
ROPE_BASE = 10000.0
RET_HEADS = 4
RET_DIM = 128
ATT_HEADS = 8
ATT_DIM = 64
ATT_LEFT_CHUNKS = 8
REL_CLIP = 128
SB_HEADS = 16
SB_DIM = 64
N_EXPERTS = 16
N_GROUPS = 4
D_EXPERT = 512
HALF = 512

VMEM_LIMIT = 48 * 1024 * 1024
LANES = 128

ROW_TILE = 512
RET_TILE = 256
ATT_QB = 256
ATT_WIN = ATT_LEFT_CHUNKS * CHUNK + ATT_QB
SB_BLK = 128
SB_WIN = 3 * SB_BLK
SB_EXIT = -110.0

PAIR_A = (0, 0, 0, 1, 1, 2)
PAIR_B = (1, 2, 3, 2, 3, 3)
N_CLASS = N_GROUPS * len(PAIR_A)
CLASS_ROWS = 32
MOE_TILE = 256
ROW_SUB = D_MODEL // LANES
EXT_SUB = 2 * ROW_SUB


def _cparams(sem):
    return pltpu.CompilerParams(dimension_semantics=sem, vmem_limit_bytes=VMEM_LIMIT)


def _mod_kernel(c_ref, w_ref, b_ref, o_ref):
    c = c_ref[...]
    ca = c * jax.nn.sigmoid(c)
    o_ref[0] = jnp.dot(ca.astype(BF16), w_ref[0].astype(BF16), preferred_element_type=F32) + b_ref[0]


def _modulation(c, ada_w, ada_b):
    depth, _, width = ada_w.shape
    bsz = c.shape[0]
    rows = 8
    cp = jnp.zeros((rows, D_MODEL), F32).at[:bsz].set(c)
    tn = 1536
    out = pl.pallas_call(
        _mod_kernel,
        grid=(depth, width // tn),
        in_specs=[pl.BlockSpec((rows, D_MODEL), lambda l, j: (0, 0)),
                  pl.BlockSpec((1, D_MODEL, tn), lambda l, j: (l, 0, j)),
                  pl.BlockSpec((1, 1, tn), lambda l, j: (l, 0, j))],
        out_specs=pl.BlockSpec((1, rows, tn), lambda l, j: (l, 0, j)),
        out_shape=jax.ShapeDtypeStruct((depth, rows, width), F32),
        compiler_params=_cparams(("arbitrary", "arbitrary")),
        name="adaln_mod",
    )(cp, ada_w, ada_b.reshape(depth, 1, width))
    return out[:, :bsz]


def _norm_mod(x, g, sc, sh):
    ms = jnp.mean(x * x, axis=-1, keepdims=True)
    return (x * lax.rsqrt(ms + EPS) * g) * (1.0 + sc) + sh


def _project_chunks(h, w_ref, o_ref, qk_ref, bd_ref, q_scale):
    for ci, n0 in enumerate(range(0, o_ref.shape[1], HALF)):
        p = jnp.dot(h, w_ref[:, n0:n0 + HALF], preferred_element_type=F32)
        if qk_ref is not None and ci in (4, 5):
            ms = jnp.dot((p * p).astype(BF16), bd_ref[...], preferred_element_type=F32)
            p = p * lax.rsqrt(ms + EPS) * qk_ref[ci - 4:ci - 3, :]
            if ci == 4:
                p = p * q_scale
        elif qk_ref is None and n0 < SB_HEADS * SB_DIM:
            p = p * q_scale
        o_ref[:, n0:n0 + HALF] = p.astype(BF16)


def _proj_kernel(x_ref, g_ref, sc_ref, sh_ref, w_ref, qk_ref, bd_ref, o_ref):
    h = _norm_mod(x_ref[...], g_ref[...], sc_ref[0], sh_ref[0]).astype(BF16)
    _project_chunks(h, w_ref, o_ref, qk_ref, bd_ref, ATT_DIM ** -0.5)


def _token_rows(y_ref):
    rows = y_ref.shape[0] // ROW_SUB
    return jnp.concatenate([y_ref[pl.ds(jt, rows, stride=ROW_SUB), :] for jt in range(ROW_SUB)],
                           axis=1)


def _proj_res_kernel(x_ref, y_ref, gt_ref, g_ref, sc_ref, sh_ref, w_ref, xo_ref, o_ref):
    x = x_ref[...] + gt_ref[0] * _token_rows(y_ref)
    xo_ref[...] = x
    h = _norm_mod(x, g_ref[...], sc_ref[0], sh_ref[0]).astype(BF16)
    _project_chunks(h, w_ref, o_ref, None, None, SB_DIM ** -0.5)


def _project(x, gain, sc, sh, w, seq, qk_gain=None, y=None, gate=None):
    n = x.shape[0]
    nout = w.shape[1]
    tpb = seq // ROW_TILE
    row = pl.BlockSpec((ROW_TILE, D_MODEL), lambda i: (i, 0))
    per_b = pl.BlockSpec((1, 1, D_MODEL), lambda i: (i // tpb, 0, 0))
    gspec = pl.BlockSpec((1, D_MODEL), lambda i: (0, 0))
    wspec = pl.BlockSpec((D_MODEL, nout), lambda i: (0, 0))
    ospec = pl.BlockSpec((ROW_TILE, nout), lambda i: (i, 0))
    oshape = jax.ShapeDtypeStruct((n, nout), BF16)
    if y is None:
        head = np.arange(HALF) // ATT_DIM
        bd = jnp.asarray((head[:, None] == head[None, :]).astype(np.float32) / ATT_DIM, BF16)
        qk = jnp.tile(qk_gain.astype(F32), (1, ATT_HEADS))
        return pl.pallas_call(
            _proj_kernel, grid=(n // ROW_TILE,),
            in_specs=[row, gspec, per_b, per_b, wspec,
                      pl.BlockSpec((2, HALF), lambda i: (0, 0)),
                      pl.BlockSpec((HALF, HALF), lambda i: (0, 0))],
            out_specs=ospec, out_shape=oshape,
            compiler_params=_cparams(("arbitrary",)), name="norm_proj",
        )(x, gain.reshape(1, D_MODEL), sc, sh, w, qk, bd)
    return pl.pallas_call(
        _proj_res_kernel, grid=(n // ROW_TILE,),
        in_specs=[row, pl.BlockSpec((ROW_TILE * ROW_SUB, LANES), lambda i: (i, 0)), per_b, gspec,
                  per_b, per_b, wspec],
        out_specs=[row, ospec],
        out_shape=[jax.ShapeDtypeStruct((n, D_MODEL), F32), oshape],
        compiler_params=_cparams(("arbitrary",)), name="res_norm_proj",
    )(x, y, gate, gain.reshape(1, D_MODEL), sc, sh, w)


def _ret_tables(seq):
    inv = ROPE_BASE ** (-np.arange(0, RET_DIM, 2, dtype=np.float64) / RET_DIM)
    ang = np.arange(seq, dtype=np.float64)[:, None] * inv[None, :]
    cos = np.concatenate([np.cos(ang), np.cos(ang)], axis=1)
    sin = np.concatenate([-np.sin(ang), np.sin(ang)], axis=1)
    log_g = np.log(1.0 - 2.0 ** (-5.0 - np.arange(RET_HEADS, dtype=np.float64)))
    idx = np.arange(RET_TILE)
    same_or_earlier = (idx[None, :] // CHUNK) <= (idx[:, None] // CHUNK)
    dec = np.exp(log_g[:, None, None] * np.abs(idx[:, None] - idx[None, :])) * same_or_earlier
    loc = idx.astype(np.float64)
    qdec = np.exp(log_g[:, None] * (loc[None, :] + 1.0))
    kdec = np.exp(log_g[:, None] * (RET_TILE - 1.0 - loc[None, :]))
    tdec = np.exp(log_g * RET_TILE)
    qk = np.stack([qdec, kdec], axis=1)[..., None] * np.ones((1, 1, 1, RET_DIM))
    return (jnp.asarray(cos, F32), jnp.asarray(sin, F32), jnp.asarray(dec, F32),
            jnp.asarray(qk, F32), tuple(float(t) for t in tdec))


def _ret_kernel(tdec, q_ref, k_ref, v_ref, g_ref, cos_ref, sin_ref, dec_ref, qk_ref, o_ref, st_ref):
    @pl.when(pl.program_id(1) == 0)
    def _():
        st_ref[...] = jnp.zeros_like(st_ref)

    cos = cos_ref[...]
    sin = sin_ref[...]
    half = RET_DIM // 2
    for h in range(RET_HEADS):
        cols = slice(h * RET_DIM, (h + 1) * RET_DIM)
        q = q_ref[:, cols].astype(F32)
        k = k_ref[:, cols].astype(F32)
        q = q * cos + pltpu.roll(q, half, 1) * sin
        k = (k * cos + pltpu.roll(k, half, 1) * sin) * (RET_DIM ** -0.5)
        v = v_ref[:, cols]
        qb = q.astype(BF16)
        kb = k.astype(BF16)
        s = lax.dot_general(qb, kb, (((1,), (1,)), ((), ())), preferred_element_type=F32)
        s = s * dec_ref[h]
        o = jnp.dot(s.astype(BF16), v, preferred_element_type=F32)
        st = st_ref[h]
        o = o + jnp.dot((q * qk_ref[h, 0]).astype(BF16), st.astype(BF16), preferred_element_type=F32)
        kd = (k * qk_ref[h, 1]).astype(BF16)
        st_ref[h] = st * tdec[h] + lax.dot_general(kd, v, (((0,), (0,)), ((), ())),
                                                   preferred_element_type=F32)
        o = o * lax.rsqrt(jnp.mean(o * o, axis=-1, keepdims=True) + EPS)
        g = g_ref[:, cols].astype(F32)
        o_ref[:, cols] = (o * (g * jax.nn.sigmoid(g))).astype(BF16)


def _retention(proj, bsz, seq):
    n = proj.shape[0]
    nt = seq // RET_TILE
    cos, sin, dec, qk, tdec = _ret_tables(seq)

    def col(c):
        return pl.BlockSpec((RET_TILE, HALF), lambda b, i: (b * nt + i, c))

    pos = pl.BlockSpec((RET_TILE, RET_DIM), lambda b, i: (i, 0))
    return pl.pallas_call(
        functools.partial(_ret_kernel, tdec),
        grid=(bsz, nt),
        in_specs=[col(0), col(1), col(2), col(3), pos, pos,
                  pl.BlockSpec((RET_HEADS, RET_TILE, RET_TILE), lambda b, i: (0, 0, 0)),
                  pl.BlockSpec((RET_HEADS, 2, RET_TILE, RET_DIM), lambda b, i: (0, 0, 0, 0))],
        out_specs=pl.BlockSpec((RET_TILE, HALF), lambda b, i: (b * nt + i, 0)),
        out_shape=jax.ShapeDtypeStruct((n, HALF), BF16),
        scratch_shapes=[pltpu.VMEM((RET_HEADS, RET_DIM, RET_DIM), F32)],
        compiler_params=_cparams(("arbitrary", "arbitrary")), name="retention",
    )(proj, proj, proj, proj, cos, sin, dec, qk)


def _att_bias(rel_bias):
    span = ATT_WIN + ATT_QB
    m = np.concatenate([np.arange(0, ATT_WIN), np.zeros(1, np.int64), np.arange(-(ATT_QB - 1), 0)])
    idx = np.clip((ATT_WIN - ATT_QB) - m, -(CHUNK - 1), REL_CLIP) + (CHUNK - 1)
    vec = rel_bias[:, idx].astype(F32)
    heads = rel_bias.shape[0]
    toep = jnp.tile(vec, (1, ATT_QB))[:, :ATT_QB * (span - 1)].reshape(heads, ATT_QB, span - 1)
    qc = np.arange(ATT_QB)[:, None] // CHUNK
    kc = np.arange(ATT_WIN)[None, :] // CHUNK - ATT_LEFT_CHUNKS
    ok = (kc <= qc) & (kc >= qc - ATT_LEFT_CHUNKS)
    return jnp.where(jnp.asarray(ok)[None], toep[:, :, :ATT_WIN], -jnp.inf)


def _fill_padded(pad_ref, src_ref, front):
    pad_ref[:front, :] = jnp.zeros((front, pad_ref.shape[1]), pad_ref.dtype)
    pad_ref[front:, :] = src_ref[...]


def _att_kernel(q_ref, k_ref, v_ref, bias_ref, o_ref, kpad, vpad):
    i = pl.program_id(1)
    front = ATT_WIN - ATT_QB

    @pl.when(i == 0)
    def _():
        _fill_padded(kpad, k_ref, front)
        _fill_padded(vpad, v_ref, front)

    start = pl.multiple_of(i * ATT_QB, ATT_QB)
    col = lax.broadcasted_iota(jnp.int32, (1, ATT_WIN), 1)
    pen = jnp.where(col >= front - start, 0.0, -jnp.inf)
    first = lax.broadcasted_iota(jnp.int32, (ATT_QB, LANES), 1) < ATT_DIM
    nt = (((1,), (1,)), ((), ()))
    for p in range(ATT_HEADS * ATT_DIM // LANES):
        cols = slice(p * LANES, (p + 1) * LANES)
        q = q_ref[:, cols]
        kw = kpad[pl.ds(start, ATT_WIN), cols]
        vw = vpad[pl.ds(start, ATT_WIN), cols]
        zero = jnp.zeros_like(q)
        qq = jnp.concatenate([jnp.where(first, q, zero), jnp.where(first, zero, q)], axis=0)
        bias = jnp.concatenate([bias_ref[2 * p], bias_ref[2 * p + 1]], axis=0)
        s = lax.dot_general(qq, kw, nt, preferred_element_type=F32) + bias + pen
        e = jnp.exp(s - jnp.max(s, axis=-1, keepdims=True))
        den = jnp.sum(e, axis=-1, keepdims=True)
        o = jnp.dot(e.astype(BF16), vw, preferred_element_type=F32) / den
        o_ref[:, cols] = jnp.where(first, o[:ATT_QB], o[ATT_QB:]).astype(BF16)


def _chunk_attention(proj, rel_bias, bsz, seq):
    n = proj.shape[0]
    nq = seq // ATT_QB
    front = ATT_WIN - ATT_QB
    return pl.pallas_call(
        _att_kernel,
        grid=(bsz, nq),
        in_specs=[pl.BlockSpec((ATT_QB, HALF), lambda b, i: (b * nq + i, 4)),
                  pl.BlockSpec((seq, HALF), lambda b, i: (b, 5)),
                  pl.BlockSpec((seq, HALF), lambda b, i: (b, 6)),
                  pl.BlockSpec((ATT_HEADS, ATT_QB, ATT_WIN), lambda b, i: (0, 0, 0))],
        out_specs=pl.BlockSpec((ATT_QB, HALF), lambda b, i: (b * nq + i, 0)),
        out_shape=jax.ShapeDtypeStruct((n, HALF), BF16),
        scratch_shapes=[pltpu.VMEM((front + seq, HALF), BF16), pltpu.VMEM((front + seq, HALF), BF16)],
        compiler_params=_cparams(("arbitrary", "arbitrary")), name="chunk_attention",
    )(proj, proj, proj, _att_bias(rel_bias))


def _sb_logs(z):
    log_beta = jnp.minimum(z, 0.0) - jnp.log(1.0 + jnp.exp(-jnp.abs(z)))
    return log_beta, log_beta - z


def _sb_kernel(q_ref, k_ref, v_ref, tri_ref, o_ref, kpad, vpad, run_ref, acc_ref):
    iq = pl.program_id(2)
    front = SB_WIN - SB_BLK
    nblk = SB_WIN // SB_BLK
    pairs = HALF // LANES

    @pl.when(iq == 0)
    def _():
        _fill_padded(kpad, k_ref, front)
        _fill_padded(vpad, v_ref, front)

    start = pl.multiple_of(iq * SB_BLK, SB_BLK)
    row = lax.broadcasted_iota(jnp.int32, (SB_BLK, SB_WIN), 0)
    col = lax.broadcasted_iota(jnp.int32, (SB_BLK, SB_WIN), 1)
    mask = jnp.where(col - row < front, col, -1) >= jnp.maximum(front - start, 0)
    first = lax.broadcasted_iota(jnp.int32, (SB_BLK, LANES), 1) < SB_DIM
    nt = (((1,), (1,)), ((), ()))
    tri = tri_ref[...]

    def heads_of(p):
        q = q_ref[:, p * LANES:(p + 1) * LANES]
        zero = jnp.zeros_like(q)
        return jnp.where(first, q, zero), jnp.where(first, zero, q)

    alive = None
    mask2 = jnp.concatenate([mask, mask], axis=0)
    lbs, l1s, vws = [], [], []
    for p in range(pairs):
        cols = slice(p * LANES, (p + 1) * LANES)
        kw = kpad[pl.ds(start, SB_WIN), cols]
        vws.append(vpad[pl.ds(start, SB_WIN), cols])
        qq = jnp.concatenate(heads_of(p), axis=0)
        z = lax.dot_general(qq, kw, nt, preferred_element_type=F32)
        log_beta, log_1mb = _sb_logs(z)
        lbs.append(log_beta)
        l1 = jnp.where(mask2, log_1mb, 0.0).astype(BF16)
        l1s += [l1[hh * SB_BLK:(hh + 1) * SB_BLK, c * SB_BLK:(c + 1) * SB_BLK]
                for hh in range(2) for c in range(nblk)]
    cs = jnp.dot(jnp.concatenate(l1s, axis=0), tri, preferred_element_type=F32)
    for p in range(pairs):
        rows = []
        for hh in range(2):
            h = 2 * p + hh
            run = jnp.zeros((SB_BLK, SB_BLK), F32)
            accs = [None] * nblk
            for c in reversed(range(nblk)):
                blk = cs[(h * nblk + c) * SB_BLK:(h * nblk + c + 1) * SB_BLK]
                accs[c] = blk[:, :SB_BLK] + run
                run = run + blk[:, SB_BLK:]
            rows.append(jnp.concatenate(accs, axis=1))
            run_ref[h] = run
            alive = run if alive is None else jnp.maximum(alive, run)
        a = jnp.where(mask2, jnp.exp(lbs[p] + jnp.concatenate(rows, axis=0)), 0.0)
        o = jnp.dot(a.astype(BF16), vws[p], preferred_element_type=F32)
        acc_ref[2 * p] = o[:SB_BLK]
        acc_ref[2 * p + 1] = o[SB_BLK:]

    def cond(c):
        return jnp.logical_and(c[0] >= 0, c[1] > SB_EXIT)

    def body(c):
        j = c[0]
        kstart = pl.multiple_of(front + j * SB_BLK, SB_BLK)
        alive = None
        for p in range(pairs):
            cols = slice(p * LANES, (p + 1) * LANES)
            kb = kpad[pl.ds(kstart, SB_BLK), cols]
            vb = vpad[pl.ds(kstart, SB_BLK), cols]
            for hh, qm in enumerate(heads_of(p)):
                z = lax.dot_general(qm, kb, nt, preferred_element_type=F32)
                log_beta, log_1mb = _sb_logs(z)
                cs = jnp.dot(log_1mb.astype(BF16), tri, preferred_element_type=F32)
                run = run_ref[2 * p + hh]
                a = jnp.exp(log_beta + run + cs[:, :SB_BLK])
                acc_ref[2 * p + hh] += jnp.dot(a.astype(BF16), vb, preferred_element_type=F32)
                run = run + cs[:, SB_BLK:]
                run_ref[2 * p + hh] = run
                alive = run if alive is None else jnp.maximum(alive, run)
        return j - 1, jnp.max(alive)

    lax.while_loop(cond, body, (iq - nblk, jnp.max(alive)))
    for p in range(pairs):
        o_ref[:, p * LANES:(p + 1) * LANES] = jnp.where(first, acc_ref[2 * p],
                                                        acc_ref[2 * p + 1]).astype(BF16)


def _stick_breaking(proj, bsz, seq):
    n = proj.shape[0]
    nq = seq // SB_BLK
    groups = SB_HEADS * SB_DIM // HALF
    front = SB_WIN - SB_BLK
    j = np.arange(SB_BLK)
    tri = np.concatenate([(j[:, None] > j[None, :]).astype(np.float32),
                          np.ones((SB_BLK, SB_BLK), np.float32)], axis=1)
    return pl.pallas_call(
        _sb_kernel,
        grid=(bsz, groups, nq),
        in_specs=[pl.BlockSpec((SB_BLK, HALF), lambda b, g, i: (b * nq + i, g)),
                  pl.BlockSpec((seq, HALF), lambda b, g, i: (b, groups + g)),
                  pl.BlockSpec((seq, HALF), lambda b, g, i: (b, 2 * groups + g)),
                  pl.BlockSpec((SB_BLK, 2 * SB_BLK), lambda b, g, i: (0, 0))],
        out_specs=pl.BlockSpec((SB_BLK, HALF), lambda b, g, i: (b * nq + i, g)),
        out_shape=jax.ShapeDtypeStruct((n, SB_HEADS * SB_DIM), BF16),
        scratch_shapes=[pltpu.VMEM((front + seq, HALF), BF16), pltpu.VMEM((front + seq, HALF), BF16),
                        pltpu.VMEM((HALF // SB_DIM, SB_BLK, SB_BLK), F32),
                        pltpu.VMEM((HALF // SB_DIM, SB_BLK, LANES), F32)],
        compiler_params=_cparams(("arbitrary", "arbitrary", "arbitrary")), name="stick_breaking",
    )(proj, proj, proj, jnp.asarray(tri, BF16))


def _out_route_kernel(ma_ref, mb_ref, wa_ref, wb_ref, x_ref, g1_ref, g_ref, sc_ref, sh_ref,
                      rw_ref, rb_ref, tri_ref, xo_ref, he_ref, rt_ref, cnt_ref, carry_ref):
    @pl.when(pl.program_id(0) == 0)
    def _():
        carry_ref[...] = jnp.zeros_like(carry_ref)

    tm = x_ref.shape[0]
    mix = (jnp.dot(ma_ref[...], wa_ref[...], preferred_element_type=F32)
           + jnp.dot(mb_ref[...], wb_ref[...], preferred_element_type=F32))
    x = x_ref[...] + g1_ref[0] * mix
    xo_ref[...] = x
    h = _norm_mod(x, g_ref[...], sc_ref[0], sh_ref[0])
    for jt in range(ROW_SUB):
        he_ref[pl.ds(jt, tm, stride=EXT_SUB), :] = h[:, jt * LANES:(jt + 1) * LANES]

    nt = (((1,), (1,)), ((), ()))
    h_hi = h.astype(BF16)
    h_lo = (h - h_hi.astype(F32)).astype(BF16)
    w = rw_ref[...]
    w_hi = w.astype(BF16)
    w_lo = (w - w_hi.astype(F32)).astype(BF16)
    logits = (lax.dot_general(w_hi, h_hi, nt, preferred_element_type=F32)
              + lax.dot_general(w_hi, h_lo, nt, preferred_element_type=F32)
              + lax.dot_general(w_lo, h_hi, nt, preferred_element_type=F32))
    score = jax.nn.sigmoid(logits)
    sel = score + rb_ref[...]

    best = None
    for g in range(N_GROUPS):
        for p in range(len(PAIR_A)):
            ea, eb = 4 * g + PAIR_A[p], 4 * g + PAIR_B[p]
            val = sel[ea:ea + 1, :] + sel[eb:eb + 1, :]
            cid = jnp.full((1, tm), float(len(PAIR_A) * g + p), F32)
            cand = (val, cid, score[ea:ea + 1, :], score[eb:eb + 1, :])
            if best is None:
                best = cand
            else:
                take = val > best[0]
                best = tuple(jnp.where(take, c_, b_) for c_, b_ in zip(cand, best))
    _, cls, sa, sb = best
    tot = sa + sb
    wgt_a = sa / tot
    wgt_b = sb / tot

    crow = lax.broadcasted_iota(jnp.int32, (CLASS_ROWS, tm), 0).astype(F32)
    onehot = jnp.where(crow == cls, 1.0, 0.0)
    before = jnp.dot(onehot.astype(BF16), tri_ref[...], preferred_element_type=F32)
    carry = carry_ref[...]
    rank = jnp.sum(onehot * (before + carry[:, :1]), axis=0, keepdims=True)
    carry = carry + jnp.sum(onehot, axis=1, keepdims=True)
    carry_ref[...] = carry
    cnt_ref[0] = carry

    srow = lax.broadcasted_iota(jnp.int32, (8, tm), 0)
    rt_ref[0] = jnp.where(srow == 0, cls, jnp.where(srow == 1, rank, 0.0))
    wrow = lax.broadcasted_iota(jnp.int32, (LANES, tm), 0)
    wtile = jnp.where(wrow == 0, wgt_a, jnp.where(wrow == 1, wgt_b, 0.0))
    wt = wtile.T
    for jt in range(ROW_SUB, EXT_SUB):
        he_ref[pl.ds(jt, tm, stride=EXT_SUB), :] = wt


def _out_route(mix_a, mix_b, col_a, col_b, w_out, x, g1, gain, sc, sh, router_w, router_b, seq):
    n = x.shape[0]
    tm = ROW_TILE
    nt = n // tm
    tpb = seq // tm
    t = np.arange(tm)
    tri = jnp.asarray((t[:, None] < t[None, :]).astype(np.float32), BF16)
    row = pl.BlockSpec((tm, D_MODEL), lambda i: (i, 0))
    per_b = pl.BlockSpec((1, 1, D_MODEL), lambda i: (i // tpb, 0, 0))
    rb = jnp.broadcast_to(router_b.astype(F32)[:, None], (N_EXPERTS, tm))
    return pl.pallas_call(
        _out_route_kernel,
        grid=(nt,),
        in_specs=[pl.BlockSpec((tm, HALF), lambda i: (i, col_a)),
                  pl.BlockSpec((tm, HALF), lambda i: (i, col_b)),
                  pl.BlockSpec((HALF, D_MODEL), lambda i: (0, 0)),
                  pl.BlockSpec((HALF, D_MODEL), lambda i: (1, 0)),
                  row, per_b,
                  pl.BlockSpec((1, D_MODEL), lambda i: (0, 0)), per_b, per_b,
                  pl.BlockSpec((N_EXPERTS, D_MODEL), lambda i: (0, 0)),
                  pl.BlockSpec((N_EXPERTS, tm), lambda i: (0, 0)),
                  pl.BlockSpec((tm, tm), lambda i: (0, 0))],
        out_specs=[row,
                   pl.BlockSpec((tm * EXT_SUB, LANES), lambda i: (i, 0)),
                   pl.BlockSpec((1, 8, tm), lambda i: (i, 0, 0)),
                   pl.BlockSpec((1, CLASS_ROWS, LANES), lambda i: (i, 0, 0))],
        out_shape=[jax.ShapeDtypeStruct((n, D_MODEL), F32),
                   jax.ShapeDtypeStruct((n * EXT_SUB, LANES), F32),
                   jax.ShapeDtypeStruct((nt, 8, tm), F32),
                   jax.ShapeDtypeStruct((nt, CLASS_ROWS, LANES), F32)],
        scratch_shapes=[pltpu.VMEM((CLASS_ROWS, LANES), F32)],
        compiler_params=_cparams(("arbitrary",)), name="out_proj_route",
    )(mix_a, mix_b, w_out, w_out, x, g1, gain.reshape(1, D_MODEL), sc, sh,
      router_w.T.astype(F32), rb, tri)


def _slot_kernel(pos_ref, nv_ref, inv_ref):
    n = pos_ref.shape[0]

    def fill_tile(j, c):
        def fill(r, c2):
            inv_ref[j * MOE_TILE + r] = 0
            return c2

        return lax.fori_loop(nv_ref[j], MOE_TILE, fill, c)

    lax.fori_loop(0, nv_ref.shape[0], fill_tile, 0)

    def place(t, c):
        inv_ref[pos_ref[t]] = t
        return c

    lax.fori_loop(0, n, place, 0, unroll=8)


def _routing_plan(route, counts, n):
    nt, _, tm = route.shape
    cls = route[:, 0, :].reshape(n).astype(jnp.int32)
    rank = route[:, 1, :].reshape(n).astype(jnp.int32)
    cnt = counts[-1, :N_CLASS, 0].astype(jnp.int32)
    tiles_c = (cnt + MOE_TILE - 1) // MOE_TILE
    ends = jnp.cumsum(tiles_c)
    starts = ends - tiles_c
    pos = starts[cls] * MOE_TILE + rank
    max_tiles = n // MOE_TILE + N_CLASS
    total = ends[-1]
    j = jnp.arange(max_tiles, dtype=jnp.int32)
    jj = jnp.minimum(j, total - 1)
    tcls = jnp.sum((ends[None, :] <= jj[:, None]).astype(jnp.int32), axis=1)
    grp = tcls // len(PAIR_A)
    pair = tcls % len(PAIR_A)
    ea = 4 * grp + jnp.asarray(PAIR_A, jnp.int32)[pair]
    eb = 4 * grp + jnp.asarray(PAIR_B, jnp.int32)[pair]
    valid_rows = jnp.clip(cnt[tcls] - (jj - starts[tcls]) * MOE_TILE, 0, MOE_TILE)
    valid_rows = jnp.where(j < total, valid_rows, 0).astype(jnp.int32)
    inv = pl.pallas_call(
        _slot_kernel,
        in_specs=[pl.BlockSpec(memory_space=pltpu.SMEM), pl.BlockSpec(memory_space=pltpu.SMEM)],
        out_specs=pl.BlockSpec(memory_space=pltpu.SMEM),
        out_shape=jax.ShapeDtypeStruct((max_tiles * MOE_TILE,), jnp.int32),
        name="moe_slots",
    )(pos, valid_rows)
    return inv, ea, eb, valid_rows, total.reshape(1)


def _moe_kernel(inv_ref, ea_ref, eb_ref, nv_ref, tot_ref, h_hbm, wg_a, wu_a, wd_a, wg_b, wu_b, wd_b,
                y_hbm, gbuf, ybuf, gsem, ssem):
    del ea_ref, eb_ref
    j = pl.program_id(0)
    total = tot_ref[0]
    slot = j % 2
    unroll = 8

    def gather_start(tile, s):
        base = tile * MOE_TILE

        def issue(r, prio):
            src = pl.multiple_of(inv_ref[base + r] * EXT_SUB, EXT_SUB)
            dst = pl.multiple_of(r * EXT_SUB, EXT_SUB)
            pltpu.make_async_copy(h_hbm.at[pl.ds(src, EXT_SUB), :], gbuf.at[s, pl.ds(dst, EXT_SUB), :],
                                  gsem.at[s]).start(priority=prio)

        def group(g, c):
            for u in range(unroll):
                issue(g * unroll + u, u % 2)
            return c

        lax.fori_loop(0, MOE_TILE // unroll, group, 0)

    def gather_wait(s):
        pltpu.make_async_copy(h_hbm.at[pl.ds(0, MOE_TILE * EXT_SUB), :], gbuf.at[s],
                              gsem.at[s]).wait()

    def scatter_start(tile, s):
        base = tile * MOE_TILE
        nv = nv_ref[tile]

        def issue(r, prio):
            src = pl.multiple_of(r * ROW_SUB, ROW_SUB)
            dst = pl.multiple_of(inv_ref[base + r] * ROW_SUB, ROW_SUB)
            pltpu.make_async_copy(ybuf.at[s, pl.ds(src, ROW_SUB), :], y_hbm.at[pl.ds(dst, ROW_SUB), :],
                                  ssem.at[s]).start(priority=prio)

        groups = nv // unroll

        def group(g, c):
            for u in range(unroll):
                issue(g * unroll + u, u % 2)
            return c

        lax.fori_loop(0, groups, group, 0)
        for u in range(unroll):
            @pl.when(groups * unroll + u < nv)
            def _():
                issue(groups * unroll + u, u % 2)

    def scatter_wait(tile, s):
        nv = nv_ref[tile]
        for bit in range(MOE_TILE.bit_length()):
            rows = 1 << bit

            @pl.when((nv >> bit) & 1 == 1)
            def _():
                pltpu.make_async_copy(ybuf.at[s, pl.ds(0, rows * ROW_SUB), :],
                                      y_hbm.at[pl.ds(0, rows * ROW_SUB), :], ssem.at[s]).wait()

    @pl.when(j == 0)
    def _():
        gather_start(0, 0)

    @pl.when(j < total)
    def _():
        gather_wait(slot)

        @pl.when(j + 1 < total)
        def _():
            gather_start(j + 1, 1 - slot)

        xb = jnp.concatenate([gbuf[slot, pl.ds(jt, MOE_TILE, stride=EXT_SUB), :]
                              for jt in range(ROW_SUB)], axis=1).astype(BF16)
        wrow = gbuf[slot, pl.ds(ROW_SUB, MOE_TILE, stride=EXT_SUB), :]
        y = None
        for lane, (wg, wu, wd) in enumerate(((wg_a, wu_a, wd_a), (wg_b, wu_b, wd_b))):
            g = jnp.dot(xb, wg[0], preferred_element_type=F32)
            u = jnp.dot(xb, wu[0], preferred_element_type=F32)
            wgt = wrow[:, lane:lane + 1]
            he = ((g * jax.nn.sigmoid(g)) * u * wgt).astype(BF16)
            part = jnp.dot(he, wd[0], preferred_element_type=F32)
            y = part if y is None else y + part

        @pl.when(j >= 2)
        def _():
            scatter_wait(j - 2, slot)

        for jt in range(ROW_SUB):
            chunk = y[:, jt * LANES:(jt + 1) * LANES]
            ybuf[slot, pl.ds(jt, MOE_TILE, stride=ROW_SUB), :] = chunk
        scatter_start(j, slot)

        @pl.when(j == total - 1)
        def _():
            @pl.when(j >= 1)
            def _():
                scatter_wait(j - 1, 1 - slot)

            scatter_wait(j, slot)


def _moe(hext, inv, ea, eb, valid_rows, total, layer, w_gate, w_up, w_down):
    n = hext.shape[0] // EXT_SUB
    max_tiles = inv.shape[0] // MOE_TILE

    def wspec(shape, which):
        if which == 0:
            return pl.BlockSpec((None, 1) + shape,
                                lambda j, inv_, ea_, eb_, nv_, t_: (layer, ea_[j], 0, 0))
        return pl.BlockSpec((None, 1) + shape, lambda j, inv_, ea_, eb_, nv_, t_: (layer, eb_[j], 0, 0))

    up = (D_MODEL, D_EXPERT)
    down = (D_EXPERT, D_MODEL)
    grid_spec = pltpu.PrefetchScalarGridSpec(
        num_scalar_prefetch=5,
        grid=(max_tiles,),
        in_specs=[pl.BlockSpec(memory_space=pl.ANY),
                  wspec(up, 0), wspec(up, 0), wspec(down, 0),
                  wspec(up, 1), wspec(up, 1), wspec(down, 1)],
        out_specs=pl.BlockSpec(memory_space=pl.ANY),
        scratch_shapes=[pltpu.VMEM((2, MOE_TILE * EXT_SUB, LANES), F32),
                        pltpu.VMEM((2, MOE_TILE * ROW_SUB, LANES), F32),
                        pltpu.SemaphoreType.DMA((2,)),
                        pltpu.SemaphoreType.DMA((2,))],
    )
    return pl.pallas_call(
        _moe_kernel,
        grid_spec=grid_spec,
        out_shape=jax.ShapeDtypeStruct((n * ROW_SUB, LANES), F32),
        compiler_params=_cparams(("arbitrary",)), name="moe_experts",
    )(inv, ea, eb, valid_rows, total, hext, w_gate, w_up, w_down, w_gate, w_up, w_down)


def _final_kernel(x_ref, y_ref, g_ref, o_ref):
    o_ref[...] = x_ref[...] + g_ref[0] * _token_rows(y_ref)


def _final(x, y, gate, seq):
    n = x.shape[0]
    tpb = seq // ROW_TILE
    row = pl.BlockSpec((ROW_TILE, D_MODEL), lambda i: (i, 0))
    return pl.pallas_call(
        _final_kernel, grid=(n // ROW_TILE,),
        in_specs=[row, pl.BlockSpec((ROW_TILE * ROW_SUB, LANES), lambda i: (i, 0)),
                  pl.BlockSpec((1, 1, D_MODEL), lambda i: (i // tpb, 0, 0))],
        out_specs=row, out_shape=jax.ShapeDtypeStruct((n, D_MODEL), F32),
        compiler_params=_cparams(("arbitrary",)), name="final_residual",
    )(x, y, gate)


def kernel(x, c, ada_w, ada_b, norm1_g, norm2_g, even_w_in, even_w_out, att_q_norm_g, att_k_norm_g,
           att_rel_bias, odd_w_in, odd_w_out, router_w, router_b, exp_w_gate, exp_w_up, exp_w_down):
    bsz, seq, d = x.shape
    n = bsz * seq
    mod = _modulation(c, ada_w, ada_b)
    mods = [[mod[l, :, k * d:(k + 1) * d].reshape(bsz, 1, d) for k in range(6)] for l in range(2)]
    xf = x.reshape(n, d)
    w_gate, w_up, w_down = (w.astype(BF16) for w in (exp_w_gate, exp_w_up, exp_w_down))

    def moe_layer(layer, mix_a, mix_b, col_a, col_b, w_out, xin):
        sh1, sc1, g1, sh2, sc2, g2 = mods[layer]
        x1, hext, route, counts = _out_route(mix_a, mix_b, col_a, col_b, w_out.astype(BF16), xin, g1,
                                             norm2_g[layer], sc2, sh2, router_w, router_b, seq)
        inv, ea, eb, valid_rows, total = _routing_plan(route, counts, n)
        y = _moe(hext, inv, ea, eb, valid_rows, total, layer, w_gate, w_up, w_down)
        return x1, y, g2

    sh1, sc1, _, _, _, _ = mods[0]
    qk_gain = jnp.stack([att_q_norm_g[0], att_k_norm_g[0]])
    proj0 = _project(xf, norm1_g[0], sc1, sh1, even_w_in[0].astype(BF16), seq, qk_gain=qk_gain)
    ret = _retention(proj0, bsz, seq)
    att = _chunk_attention(proj0, att_rel_bias[0], bsz, seq)
    x1, y0, g2_0 = moe_layer(0, ret, att, 0, 0, even_w_out[0], xf)

    sh1, sc1, _, _, _, _ = mods[1]
    x2, proj1 = _project(x1, norm1_g[1], sc1, sh1, odd_w_in[0].astype(BF16), seq, y=y0, gate=g2_0)
    sbo = _stick_breaking(proj1, bsz, seq)
    x3, y1, g2_1 = moe_layer(1, sbo, sbo, 0, 1, odd_w_out[0], x2)
    return _final(x3, y1, g2_1, seq).reshape(bsz, seq, d)
```

```python
import functools

import numpy as np
import jax
import jax.numpy as jnp
from jax import lax
from jax.experimental import pallas as pl
from jax.experimental.pallas import tpu as pltpu

F32 = jnp.float32
BF16 = jnp.bfloat16

D_MODEL = 1024
CHUNK = 64
EPS = 1e-6
ROPE_BASE = 10000.0
RET_HEADS = 4
RET_DIM = 128
ATT_HEADS = 8
ATT_DIM = 64
ATT_LEFT_CHUNKS = 8
REL_CLIP = 128
SB_HEADS = 16
SB_DIM = 64
N_EXPERTS = 16
N_GROUPS = 4
D_EXPERT = 512
HALF = 512

VMEM_LIMIT = 48 * 1024 * 1024
LANES = 128

ROW_TILE = 512
RET_TILE = 256
ATT_QB = 256
ATT_WIN = ATT_LEFT_CHUNKS * CHUNK + ATT_QB
ATT_STEP_BLOCKS = 2
SB_BLK = 128
SB_WIN = 3 * SB_BLK
SB_EXIT = -110.0

PAIR_A = (0, 0, 0, 1, 1, 2)
PAIR_B = (1, 2, 3, 2, 3, 3)
N_CLASS = N_GROUPS * len(PAIR_A)
CLASS_ROWS = 32
MOE_TILE = 256
ROW_SUB = D_MODEL // LANES
GATHER_BUFS = 3


def _cparams(sem):
    return pltpu.CompilerParams(dimension_semantics=sem, vmem_limit_bytes=VMEM_LIMIT)


def _mod_kernel(c_ref, w_ref, b_ref, o_ref):
    c = c_ref[...]
    ca = c * jax.nn.sigmoid(c)
    o_ref[0] = jnp.dot(ca.astype(BF16), w_ref[0].astype(BF16), preferred_element_type=F32) + b_ref[0]


def _modulation(c, ada_w, ada_b):
    depth, _, width = ada_w.shape
    bsz = c.shape[0]
    rows = 8
    cp = jnp.zeros((rows, D_MODEL), F32).at[:bsz].set(c)
    tn = 1536
    out = pl.pallas_call(
        _mod_kernel,
        grid=(depth, width // tn),
        in_specs=[pl.BlockSpec((rows, D_MODEL), lambda l, j: (0, 0)),
                  pl.BlockSpec((1, D_MODEL, tn), lambda l, j: (l, 0, j)),
                  pl.BlockSpec((1, 1, tn), lambda l, j: (l, 0, j))],
        out_specs=pl.BlockSpec((1, rows, tn), lambda l, j: (l, 0, j)),
        out_shape=jax.ShapeDtypeStruct((depth, rows, width), F32),
        compiler_params=_cparams(("arbitrary", "arbitrary")),
        name="adaln_mod",
    )(cp, ada_w, ada_b.reshape(depth, 1, width))
    return out[:, :bsz]


def _norm_mod(x, g, sc, sh):
    ms = jnp.mean(x * x, axis=-1, keepdims=True)
    return (x * lax.rsqrt(ms + EPS) * g) * (1.0 + sc) + sh


def _project_chunks(h, w_ref, o_ref, qk_ref, bd_ref, q_scale):
    for ci, n0 in enumerate(range(0, o_ref.shape[1], HALF)):
        p = jnp.dot(h, w_ref[:, n0:n0 + HALF], preferred_element_type=F32)
        if qk_ref is not None and ci in (4, 5):
            ms = jnp.dot((p * p).astype(BF16), bd_ref[...], preferred_element_type=F32)
            p = p * lax.rsqrt(ms + EPS) * qk_ref[ci - 4:ci - 3, :]
            if ci == 4:
                p = p * q_scale
        elif qk_ref is None and n0 < SB_HEADS * SB_DIM:
            p = p * q_scale
        o_ref[:, n0:n0 + HALF] = p.astype(BF16)


def _proj_kernel(x_ref, g_ref, sc_ref, sh_ref, w_ref, qk_ref, bd_ref, o_ref):
    h = _norm_mod(x_ref[...], g_ref[...], sc_ref[0], sh_ref[0]).astype(BF16)
    _project_chunks(h, w_ref, o_ref, qk_ref, bd_ref, ATT_DIM ** -0.5)


def _token_rows(y_ref):
    rows = y_ref.shape[0] // ROW_SUB
    return jnp.concatenate([y_ref[pl.ds(jt, rows, stride=ROW_SUB), :] for jt in range(ROW_SUB)],
                           axis=1)


def _proj_res_kernel(x_ref, y_ref, gt_ref, g_ref, sc_ref, sh_ref, w_ref, xo_ref, o_ref):
    x = x_ref[...] + gt_ref[0] * _token_rows(y_ref)
    xo_ref[...] = x
    h = _norm_mod(x, g_ref[...], sc_ref[0], sh_ref[0]).astype(BF16)
    _project_chunks(h, w_ref, o_ref, None, None, SB_DIM ** -0.5)


def _project(x, gain, sc, sh, w, seq, qk_gain=None, y=None, gate=None):
    n = x.shape[0]
    nout = w.shape[1]
    tpb = seq // ROW_TILE
    row = pl.BlockSpec((ROW_TILE, D_MODEL), lambda i: (i, 0))
    per_b = pl.BlockSpec((1, 1, D_MODEL), lambda i: (i // tpb, 0, 0))
    gspec = pl.BlockSpec((1, D_MODEL), lambda i: (0, 0))
    wspec = pl.BlockSpec((D_MODEL, nout), lambda i: (0, 0))
    ospec = pl.BlockSpec((ROW_TILE, nout), lambda i: (i, 0))
    oshape = jax.ShapeDtypeStruct((n, nout), BF16)
    if y is None:
        head = np.arange(HALF) // ATT_DIM
        bd = jnp.asarray((head[:, None] == head[None, :]).astype(np.float32) / ATT_DIM, BF16)
        qk = jnp.tile(qk_gain.astype(F32), (1, ATT_HEADS))
        return pl.pallas_call(
            _proj_kernel, grid=(n // ROW_TILE,),
            in_specs=[row, gspec, per_b, per_b, wspec,
                      pl.BlockSpec((2, HALF), lambda i: (0, 0)),
                      pl.BlockSpec((HALF, HALF), lambda i: (0, 0))],
            out_specs=ospec, out_shape=oshape,
            compiler_params=_cparams(("arbitrary",)), name="norm_proj",
        )(x, gain.reshape(1, D_MODEL), sc, sh, w, qk, bd)
    return pl.pallas_call(
        _proj_res_kernel, grid=(n // ROW_TILE,),
        in_specs=[row, pl.BlockSpec((ROW_TILE * ROW_SUB, LANES), lambda i: (i, 0)), per_b, gspec,
                  per_b, per_b, wspec],
        out_specs=[row, ospec],
        out_shape=[jax.ShapeDtypeStruct((n, D_MODEL), F32), oshape],
        compiler_params=_cparams(("arbitrary",)), name="res_norm_proj",
    )(x, y, gate, gain.reshape(1, D_MODEL), sc, sh, w)


def _ret_tables(seq):
    inv = ROPE_BASE ** (-np.arange(0, RET_DIM, 2, dtype=np.float64) / RET_DIM)
    ang = np.arange(seq, dtype=np.float64)[:, None] * inv[None, :]
    cos = np.concatenate([np.cos(ang), np.cos(ang)], axis=1)
    sin = np.concatenate([-np.sin(ang), np.sin(ang)], axis=1)
    log_g = np.log(1.0 - 2.0 ** (-5.0 - np.arange(RET_HEADS, dtype=np.float64)))
    idx = np.arange(RET_TILE)
    same_or_earlier = (idx[None, :] // CHUNK) <= (idx[:, None] // CHUNK)
    dec = np.exp(log_g[:, None, None] * np.abs(idx[:, None] - idx[None, :])) * same_or_earlier
    loc = idx.astype(np.float64)
    qdec = np.exp(log_g[:, None] * (loc[None, :] + 1.0))
    kdec = np.exp(log_g[:, None] * (RET_TILE - 1.0 - loc[None, :]))
    tdec = np.exp(log_g * RET_TILE)
    qk = np.stack([qdec, kdec], axis=1)[..., None] * np.ones((1, 1, 1, RET_DIM))
    return (jnp.asarray(cos, F32), jnp.asarray(sin, F32), jnp.asarray(dec, F32),
            jnp.asarray(qk, F32), tuple(float(t) for t in tdec))


def _ret_kernel(tdec, q_ref, k_ref, v_ref, g_ref, cos_ref, sin_ref, dec_ref, qk_ref, o_ref, st_ref):
    @pl.when(pl.program_id(1) == 0)
    def _():
        st_ref[...] = jnp.zeros_like(st_ref)

    cos = cos_ref[...]
    sin = sin_ref[...]
    half = RET_DIM // 2
    for h in range(RET_HEADS):
        cols = slice(h * RET_DIM, (h + 1) * RET_DIM)
        q = q_ref[:, cols].astype(F32)
        k = k_ref[:, cols].astype(F32)
        q = q * cos + pltpu.roll(q, half, 1) * sin
        k = (k * cos + pltpu.roll(k, half, 1) * sin) * (RET_DIM ** -0.5)
        v = v_ref[:, cols]
        qb = q.astype(BF16)
        kb = k.astype(BF16)
        s = lax.dot_general(qb, kb, (((1,), (1,)), ((), ())), preferred_element_type=F32)
        s = s * dec_ref[h]
        o = jnp.dot(s.astype(BF16), v, preferred_element_type=F32)
        st = st_ref[h]
        o = o + jnp.dot((q * qk_ref[h, 0]).astype(BF16), st.astype(BF16), preferred_element_type=F32)
        kd = (k * qk_ref[h, 1]).astype(BF16)
        st_ref[h] = st * tdec[h] + lax.dot_general(kd, v, (((0,), (0,)), ((), ())),
                                                   preferred_element_type=F32)
        o = o * lax.rsqrt(jnp.mean(o * o, axis=-1, keepdims=True) + EPS)
        g = g_ref[:, cols].astype(F32)
        o_ref[:, cols] = (o * (g * jax.nn.sigmoid(g))).astype(BF16)


def _retention(proj, bsz, seq):
    n = proj.shape[0]
    nt = seq // RET_TILE
    cos, sin, dec, qk, tdec = _ret_tables(seq)

    def col(c):
        return pl.BlockSpec((RET_TILE, HALF), lambda b, i: (b * nt + i, c))

    pos = pl.BlockSpec((RET_TILE, RET_DIM), lambda b, i: (i, 0))
    return pl.pallas_call(
        functools.partial(_ret_kernel, tdec),
        grid=(bsz, nt),
        in_specs=[col(0), col(1), col(2), col(3), pos, pos,
                  pl.BlockSpec((RET_HEADS, RET_TILE, RET_TILE), lambda b, i: (0, 0, 0)),
                  pl.BlockSpec((RET_HEADS, 2, RET_TILE, RET_DIM), lambda b, i: (0, 0, 0, 0))],
        out_specs=pl.BlockSpec((RET_TILE, HALF), lambda b, i: (b * nt + i, 0)),
        out_shape=jax.ShapeDtypeStruct((n, HALF), BF16),
        scratch_shapes=[pltpu.VMEM((RET_HEADS, RET_DIM, RET_DIM), F32)],
        compiler_params=_cparams(("arbitrary", "arbitrary")), name="retention",
    )(proj, proj, proj, proj, cos, sin, dec, qk)


def _att_bias(rel_bias):
    span = ATT_WIN + ATT_QB
    m = np.concatenate([np.arange(0, ATT_WIN), np.zeros(1, np.int64), np.arange(-(ATT_QB - 1), 0)])
    idx = np.clip((ATT_WIN - ATT_QB) - m, -(CHUNK - 1), REL_CLIP) + (CHUNK - 1)
    vec = rel_bias[:, idx].astype(F32)
    heads = rel_bias.shape[0]
    toep = jnp.tile(vec, (1, ATT_QB))[:, :ATT_QB * (span - 1)].reshape(heads, ATT_QB, span - 1)
    qc = np.arange(ATT_QB)[:, None] // CHUNK
    kc = np.arange(ATT_WIN)[None, :] // CHUNK - ATT_LEFT_CHUNKS
    ok = (kc <= qc) & (kc >= qc - ATT_LEFT_CHUNKS)
    return jnp.where(jnp.asarray(ok)[None], toep[:, :, :ATT_WIN], -jnp.inf)


def _fill_padded(pad_ref, src_ref, front):
    pad_ref[:front, :] = jnp.zeros((front, pad_ref.shape[1]), pad_ref.dtype)
    pad_ref[front:, :] = src_ref[...]


def _att_kernel(q_ref, k_ref, v_ref, bias_ref, o_ref, kpad, vpad):
    i = pl.program_id(1)
    front = ATT_WIN - ATT_QB

    @pl.when(i == 0)
    def _():
        _fill_padded(kpad, k_ref, front)
        _fill_padded(vpad, v_ref, front)

    col = lax.broadcasted_iota(jnp.int32, (1, ATT_WIN), 1)
    first = lax.broadcasted_iota(jnp.int32, (ATT_QB, LANES), 1) < ATT_DIM
    nt = (((1,), (1,)), ((), ()))
    for sub in range(ATT_STEP_BLOCKS):
        rows = slice(sub * ATT_QB, (sub + 1) * ATT_QB)
        start = pl.multiple_of((i * ATT_STEP_BLOCKS + sub) * ATT_QB, ATT_QB)
        pen = jnp.where(col >= front - start, 0.0, -jnp.inf)
        for p in range(ATT_HEADS * ATT_DIM // LANES):
            cols = slice(p * LANES, (p + 1) * LANES)
            q = q_ref[rows, cols]
            kw = kpad[pl.ds(start, ATT_WIN), cols]
            vw = vpad[pl.ds(start, ATT_WIN), cols]
            zero = jnp.zeros_like(q)
            qq = jnp.concatenate([jnp.where(first, q, zero), jnp.where(first, zero, q)], axis=0)
            bias = jnp.concatenate([bias_ref[2 * p], bias_ref[2 * p + 1]], axis=0)
            s = lax.dot_general(qq, kw, nt, preferred_element_type=F32) + bias + pen
            e = jnp.exp(s - jnp.max(s, axis=-1, keepdims=True))
            den = jnp.sum(e, axis=-1, keepdims=True)
            o = jnp.dot(e.astype(BF16), vw, preferred_element_type=F32) / den
            o_ref[rows, cols] = jnp.where(first, o[:ATT_QB], o[ATT_QB:]).astype(BF16)


def _chunk_attention(proj, rel_bias, bsz, seq):
    n = proj.shape[0]
    rows = ATT_STEP_BLOCKS * ATT_QB
    nq = seq // rows
    front = ATT_WIN - ATT_QB
    return pl.pallas_call(
        _att_kernel,
        grid=(bsz, nq),
        in_specs=[pl.BlockSpec((rows, HALF), lambda b, i: (b * nq + i, 4)),
                  pl.BlockSpec((seq, HALF), lambda b, i: (b, 5)),
                  pl.BlockSpec((seq, HALF), lambda b, i: (b, 6)),
                  pl.BlockSpec((ATT_HEADS, ATT_QB, ATT_WIN), lambda b, i: (0, 0, 0))],
        out_specs=pl.BlockSpec((rows, HALF), lambda b, i: (b * nq + i, 0)),
        out_shape=jax.ShapeDtypeStruct((n, HALF), BF16),
        scratch_shapes=[pltpu.VMEM((front + seq, HALF), BF16), pltpu.VMEM((front + seq, HALF), BF16)],
        compiler_params=_cparams(("arbitrary", "arbitrary")), name="chunk_attention",
    )(proj, proj, proj, _att_bias(rel_bias))


def _sb_logs(z):
    log_beta = jnp.minimum(z, 0.0) - jnp.log(1.0 + jnp.exp(-jnp.abs(z)))
    return log_beta, log_beta - z


def _sb_kernel(q_ref, k_ref, v_ref, tri_ref, o_ref, kpad, vpad, run_ref, acc_ref):
    front = SB_WIN - SB_BLK
    nblk = SB_WIN // SB_BLK
    pairs = HALF // LANES
    _fill_padded(kpad, k_ref, front)
    _fill_padded(vpad, v_ref, front)
    row = lax.broadcasted_iota(jnp.int32, (SB_BLK, SB_WIN), 0)
    col = lax.broadcasted_iota(jnp.int32, (SB_BLK, SB_WIN), 1)
    causal_col = jnp.where(col - row < front, col, -1)
    first = lax.broadcasted_iota(jnp.int32, (SB_BLK, LANES), 1) < SB_DIM
    nt = (((1,), (1,)), ((), ()))

    def query_block(iq, carry):
        start = pl.multiple_of(iq * SB_BLK, SB_BLK)
        mask = causal_col >= jnp.maximum(front - start, 0)
        tri = tri_ref[...]

        def heads_of(p):
            q = q_ref[pl.ds(start, SB_BLK), p * LANES:(p + 1) * LANES]
            zero = jnp.zeros_like(q)
            return jnp.where(first, q, zero), jnp.where(first, zero, q)

        alive = None
        mask2 = jnp.concatenate([mask, mask], axis=0)
        lbs, l1s, vws = [], [], []
        for p in range(pairs):
            cols = slice(p * LANES, (p + 1) * LANES)
            kw = kpad[pl.ds(start, SB_WIN), cols]
            vws.append(vpad[pl.ds(start, SB_WIN), cols])
            qq = jnp.concatenate(heads_of(p), axis=0)
            z = lax.dot_general(qq, kw, nt, preferred_element_type=F32)
            log_beta, log_1mb = _sb_logs(z)
            lbs.append(log_beta)
            l1 = jnp.where(mask2, log_1mb, 0.0).astype(BF16)
            l1s += [l1[hh * SB_BLK:(hh + 1) * SB_BLK, c * SB_BLK:(c + 1) * SB_BLK]
                    for hh in range(2) for c in range(nblk)]
        cs = jnp.dot(jnp.concatenate(l1s, axis=0), tri, preferred_element_type=F32)
        for p in range(pairs):
            rows = []
            for hh in range(2):
                h = 2 * p + hh
                run = jnp.zeros((SB_BLK, SB_BLK), F32)
                accs = [None] * nblk
                for c in reversed(range(nblk)):
                    blk = cs[(h * nblk + c) * SB_BLK:(h * nblk + c + 1) * SB_BLK]
                    accs[c] = blk[:, :SB_BLK] + run
                    run = run + blk[:, SB_BLK:]
                rows.append(jnp.concatenate(accs, axis=1))
                run_ref[h] = run
                alive = run if alive is None else jnp.maximum(alive, run)
            a = jnp.where(mask2, jnp.exp(lbs[p] + jnp.concatenate(rows, axis=0)), 0.0)
            o = jnp.dot(a.astype(BF16), vws[p], preferred_element_type=F32)
            acc_ref[2 * p] = o[:SB_BLK]
            acc_ref[2 * p + 1] = o[SB_BLK:]

        def cond(c):
            return jnp.logical_and(c[0] >= 0, c[1] > SB_EXIT)

        def body(c):
            j = c[0]
            kstart = pl.multiple_of(front + j * SB_BLK, SB_BLK)
            alive = None
            for p in range(pairs):
                cols = slice(p * LANES, (p + 1) * LANES)
                kb = kpad[pl.ds(kstart, SB_BLK), cols]
                vb = vpad[pl.ds(kstart, SB_BLK), cols]
                for hh, qm in enumerate(heads_of(p)):
                    z = lax.dot_general(qm, kb, nt, preferred_element_type=F32)
                    log_beta, log_1mb = _sb_logs(z)
                    cs = jnp.dot(log_1mb.astype(BF16), tri, preferred_element_type=F32)
                    run = run_ref[2 * p + hh]
                    a = jnp.exp(log_beta + run + cs[:, :SB_BLK])
                    acc_ref[2 * p + hh] += jnp.dot(a.astype(BF16), vb, preferred_element_type=F32)
                    run = run + cs[:, SB_BLK:]
                    run_ref[2 * p + hh] = run
                    alive = run if alive is None else jnp.maximum(alive, run)
            return j - 1, jnp.max(alive)

        lax.while_loop(cond, body, (iq - nblk, jnp.max(alive)))
        for p in range(pairs):
            o_ref[pl.ds(start, SB_BLK), p * LANES:(p + 1) * LANES] = jnp.where(
                first, acc_ref[2 * p], acc_ref[2 * p + 1]).astype(BF16)
        return carry

    lax.fori_loop(0, q_ref.shape[0] // SB_BLK, query_block, 0)


def _stick_breaking(proj, bsz, seq):
    n = proj.shape[0]
    groups = SB_HEADS * SB_DIM // HALF
    front = SB_WIN - SB_BLK
    j = np.arange(SB_BLK)
    tri = np.concatenate([(j[:, None] > j[None, :]).astype(np.float32),
                          np.ones((SB_BLK, SB_BLK), np.float32)], axis=1)
    return pl.pallas_call(
        _sb_kernel,
        grid=(bsz, groups),
        in_specs=[pl.BlockSpec((seq, HALF), lambda b, g: (b, g)),
                  pl.BlockSpec((seq, HALF), lambda b, g: (b, groups + g)),
                  pl.BlockSpec((seq, HALF), lambda b, g: (b, 2 * groups + g)),
                  pl.BlockSpec((SB_BLK, 2 * SB_BLK), lambda b, g: (0, 0))],
        out_specs=pl.BlockSpec((seq, HALF), lambda b, g: (b, g)),
        out_shape=jax.ShapeDtypeStruct((n, SB_HEADS * SB_DIM), BF16),
        scratch_shapes=[pltpu.VMEM((front + seq, HALF), BF16), pltpu.VMEM((front + seq, HALF), BF16),
                        pltpu.VMEM((HALF // SB_DIM, SB_BLK, SB_BLK), F32),
                        pltpu.VMEM((HALF // SB_DIM, SB_BLK, LANES), F32)],
        compiler_params=_cparams(("arbitrary", "arbitrary")), name="stick_breaking",
    )(proj, proj, proj, jnp.asarray(tri, BF16))


def _out_route_kernel(ma_ref, mb_ref, wa_ref, wb_ref, x_ref, g1_ref, g_ref, sc_ref, sh_ref,
                      rw_ref, rb_ref, tri_ref, xo_ref, he_ref, rt_ref, cnt_ref, carry_ref):
    @pl.when(pl.program_id(0) == 0)
    def _():
        carry_ref[...] = jnp.zeros_like(carry_ref)

    tm = x_ref.shape[0]
    mix = (jnp.dot(ma_ref[...], wa_ref[...], preferred_element_type=F32)
           + jnp.dot(mb_ref[...], wb_ref[...], preferred_element_type=F32))
    x = x_ref[...] + g1_ref[0] * mix
    xo_ref[...] = x
    h = _norm_mod(x, g_ref[...], sc_ref[0], sh_ref[0])
    for jt in range(ROW_SUB):
        he_ref[pl.ds(jt, tm, stride=ROW_SUB), :] = h[:, jt * LANES:(jt + 1) * LANES]

    nt = (((1,), (1,)), ((), ()))
    h_hi = h.astype(BF16)
    h_lo = (h - h_hi.astype(F32)).astype(BF16)
    w = rw_ref[...]
    w_hi = w.astype(BF16)
    w_lo = (w - w_hi.astype(F32)).astype(BF16)
    logits = (lax.dot_general(w_hi, h_hi, nt, preferred_element_type=F32)
              + lax.dot_general(w_hi, h_lo, nt, preferred_element_type=F32)
              + lax.dot_general(w_lo, h_hi, nt, preferred_element_type=F32))
    sel = jax.nn.sigmoid(logits) + rb_ref[...]

    best = None
    for g in range(N_GROUPS):
        for p in range(len(PAIR_A)):
            ea, eb = 4 * g + PAIR_A[p], 4 * g + PAIR_B[p]
            val = sel[ea:ea + 1, :] + sel[eb:eb + 1, :]
            cid = jnp.full((1, tm), float(len(PAIR_A) * g + p), F32)
            cand = (val, cid)
            if best is None:
                best = cand
            else:
                take = val > best[0]
                best = tuple(jnp.where(take, c_, b_) for c_, b_ in zip(cand, best))
    cls = best[1]

    crow = lax.broadcasted_iota(jnp.int32, (CLASS_ROWS, tm), 0).astype(F32)
    onehot = jnp.where(crow == cls, 1.0, 0.0)
    before = jnp.dot(onehot.astype(BF16), tri_ref[...], preferred_element_type=F32)
    carry = carry_ref[...]
    rank = jnp.sum(onehot * (before + carry[:, :1]), axis=0, keepdims=True)
    carry = carry + jnp.sum(onehot, axis=1, keepdims=True)
    carry_ref[...] = carry
    cnt_ref[0] = carry

    srow = lax.broadcasted_iota(jnp.int32, (8, tm), 0)
    rt_ref[0] = jnp.where(srow == 0, cls, jnp.where(srow == 1, rank, 0.0))


def _out_route(mix_a, mix_b, col_a, col_b, w_out, x, g1, gain, sc, sh, router_w, router_b, seq):
    n = x.shape[0]
    tm = ROW_TILE
    nt = n // tm
    tpb = seq // tm
    t = np.arange(tm)
    tri = jnp.asarray((t[:, None] < t[None, :]).astype(np.float32), BF16)
    row = pl.BlockSpec((tm, D_MODEL), lambda i: (i, 0))
    per_b = pl.BlockSpec((1, 1, D_MODEL), lambda i: (i // tpb, 0, 0))
    rb = jnp.broadcast_to(router_b.astype(F32)[:, None], (N_EXPERTS, tm))
    return pl.pallas_call(
        _out_route_kernel,
        grid=(nt,),
        in_specs=[pl.BlockSpec((tm, HALF), lambda i: (i, col_a)),
                  pl.BlockSpec((tm, HALF), lambda i: (i, col_b)),
                  pl.BlockSpec((HALF, D_MODEL), lambda i: (0, 0)),
                  pl.BlockSpec((HALF, D_MODEL), lambda i: (1, 0)),
                  row, per_b,
                  pl.BlockSpec((1, D_MODEL), lambda i: (0, 0)), per_b, per_b,
                  pl.BlockSpec((N_EXPERTS, D_MODEL), lambda i: (0, 0)),
                  pl.BlockSpec((N_EXPERTS, tm), lambda i: (0, 0)),
                  pl.BlockSpec((tm, tm), lambda i: (0, 0))],
        out_specs=[row,
                   pl.BlockSpec((tm * ROW_SUB, LANES), lambda i: (i, 0)),
                   pl.BlockSpec((1, 8, tm), lambda i: (i, 0, 0)),
                   pl.BlockSpec((1, CLASS_ROWS, LANES), lambda i: (i, 0, 0))],
        out_shape=[jax.ShapeDtypeStruct((n, D_MODEL), F32),
                   jax.ShapeDtypeStruct((n * ROW_SUB, LANES), F32),
                   jax.ShapeDtypeStruct((nt, 8, tm), F32),
                   jax.ShapeDtypeStruct((nt, CLASS_ROWS, LANES), F32)],
        scratch_shapes=[pltpu.VMEM((CLASS_ROWS, LANES), F32)],
        compiler_params=_cparams(("arbitrary",)), name="out_proj_route",
    )(mix_a, mix_b, w_out, w_out, x, g1, gain.reshape(1, D_MODEL), sc, sh,
      router_w.T.astype(F32), rb, tri)


def _fill_slots(pos_ref, nv_ref, inv_ref):
    batch = 16

    def fill_tile(j, c):
        def fill(g, c2):
            for u in range(batch):
                inv_ref[j * MOE_TILE + g * batch + u] = 0
            return c2

        return lax.fori_loop(nv_ref[j] // batch, MOE_TILE // batch, fill, c)

    lax.fori_loop(0, nv_ref.shape[0], fill_tile, 0)

    def place(g, c):
        slots = [pos_ref[g * batch + u] for u in range(batch)]
        for u in range(batch):
            inv_ref[slots[u]] = g * batch + u
        return c

    lax.fori_loop(0, pos_ref.shape[0] // batch, place, 0)


def _routing_plan(route, counts, n):
    nt, _, tm = route.shape
    cls = route[:, 0, :].reshape(n).astype(jnp.int32)
    rank = route[:, 1, :].reshape(n).astype(jnp.int32)
    cnt = counts[-1, :N_CLASS, 0].astype(jnp.int32)
    tiles_c = (cnt + MOE_TILE - 1) // MOE_TILE
    ends = jnp.cumsum(tiles_c)
    starts = ends - tiles_c
    pos = starts[cls] * MOE_TILE + rank
    max_tiles = n // MOE_TILE + N_CLASS
    total = ends[-1]
    j = jnp.arange(max_tiles, dtype=jnp.int32)
    jj = jnp.minimum(j, total - 1)
    tcls = jnp.sum((ends[None, :] <= jj[:, None]).astype(jnp.int32), axis=1)
    grp = tcls // len(PAIR_A)
    pair = tcls % len(PAIR_A)
    ea = 4 * grp + jnp.asarray(PAIR_A, jnp.int32)[pair]
    eb = 4 * grp + jnp.asarray(PAIR_B, jnp.int32)[pair]
    valid_rows = jnp.clip(cnt[tcls] - (jj - starts[tcls]) * MOE_TILE, 0, MOE_TILE)
    valid_rows = jnp.where(j < total, valid_rows, 0).astype(jnp.int32)
    return pos, ea, eb, valid_rows, total.reshape(1)


def _moe_kernel(pos_ref, ea_ref, eb_ref, nv_ref, tot_ref, h_hbm, rw_ref, wg_a, wu_a, wd_a,
                wg_b, wu_b, wd_b, y_hbm, gbuf, ybuf, inv_ref, gsem, ssem):
    j = pl.program_id(0)
    total = tot_ref[0]
    gslot = j % GATHER_BUFS
    yslot = j % 2
    unroll = 8

    def gather_row(tile, s, r, prio):
        src = pl.multiple_of(inv_ref[tile * MOE_TILE + r] * ROW_SUB, ROW_SUB)
        dst = pl.multiple_of(r * ROW_SUB, ROW_SUB)
        pltpu.make_async_copy(h_hbm.at[pl.ds(src, ROW_SUB), :], gbuf.at[s, pl.ds(dst, ROW_SUB), :],
                              gsem.at[s]).start(priority=prio)

    def scatter_row(tile, s, r, prio):
        src = pl.multiple_of(r * ROW_SUB, ROW_SUB)
        dst = pl.multiple_of(inv_ref[tile * MOE_TILE + r] * ROW_SUB, ROW_SUB)
        pltpu.make_async_copy(ybuf.at[s, pl.ds(src, ROW_SUB), :], y_hbm.at[pl.ds(dst, ROW_SUB), :],
                              ssem.at[s]).start(priority=prio)

    def gather_rows(tile, s, first_group):
        def group(g, c):
            for u in range(unroll):
                gather_row(tile, s, g * unroll + u, u % 2)
            return c

        lax.fori_loop(first_group, MOE_TILE // unroll, group, 0)

    def gather_wait(s):
        pltpu.make_async_copy(h_hbm.at[pl.ds(0, MOE_TILE * ROW_SUB), :], gbuf.at[s],
                              gsem.at[s]).wait()

    def issue_step(with_gather):
        nv = nv_ref[j]
        groups = nv // unroll
        ahead = j + GATHER_BUFS - 1
        aslot = ahead % GATHER_BUFS

        def group(g, c):
            for u in range(unroll):
                scatter_row(j, yslot, g * unroll + u, u % 2)
                if with_gather:
                    gather_row(ahead, aslot, g * unroll + u, (u + 1) % 2)
            return c

        lax.fori_loop(0, groups, group, 0)
        for u in range(unroll):
            @pl.when(groups * unroll + u < nv)
            def _():
                scatter_row(j, yslot, groups * unroll + u, u % 2)
        if with_gather:
            gather_rows(ahead, aslot, groups)

    def scatter_wait(tile, s):
        nv = nv_ref[tile]
        for bit in range(MOE_TILE.bit_length()):
            rows = 1 << bit

            @pl.when((nv >> bit) & 1 == 1)
            def _():
                pltpu.make_async_copy(ybuf.at[s, pl.ds(0, rows * ROW_SUB), :],
                                      y_hbm.at[pl.ds(0, rows * ROW_SUB), :], ssem.at[s]).wait()

    @pl.when(j == 0)
    def _():
        _fill_slots(pos_ref, nv_ref, inv_ref)
        for t in range(GATHER_BUFS - 1):
            @pl.when(t < total)
            def _():
                gather_rows(t, t, 0)

    @pl.when(j < total)
    def _():
        gather_wait(gslot)
        x = jnp.concatenate([gbuf[gslot, pl.ds(jt, MOE_TILE, stride=ROW_SUB), :]
                             for jt in range(ROW_SUB)], axis=1)
        xb = x.astype(BF16)
        scores = [jax.nn.sigmoid(jnp.sum(x * rw_ref[pl.ds(e_ref[j], 1), :], axis=-1, keepdims=True))
                  for e_ref in (ea_ref, eb_ref)]
        shares = [sc_ / (scores[0] + scores[1]) for sc_ in scores]
        y = None
        for lane, (wg, wu, wd) in enumerate(((wg_a, wu_a, wd_a), (wg_b, wu_b, wd_b))):
            g = jnp.dot(xb, wg[0], preferred_element_type=F32)
            u = jnp.dot(xb, wu[0], preferred_element_type=F32)
            he = ((g * jax.nn.sigmoid(g)) * u * shares[lane]).astype(BF16)
            part = jnp.dot(he, wd[0], preferred_element_type=F32)
            y = part if y is None else y + part

        @pl.when(j >= 2)
        def _():
            scatter_wait(j - 2, yslot)

        for jt in range(ROW_SUB):
            chunk = y[:, jt * LANES:(jt + 1) * LANES]
            ybuf[yslot, pl.ds(jt, MOE_TILE, stride=ROW_SUB), :] = chunk

        @pl.when(j + GATHER_BUFS - 1 < total)
        def _():
            issue_step(True)

        @pl.when(j + GATHER_BUFS - 1 >= total)
        def _():
            issue_step(False)

        @pl.when(j == total - 1)
        def _():
            @pl.when(j >= 1)
            def _():
                scatter_wait(j - 1, 1 - yslot)

            scatter_wait(j, yslot)


def _moe(hext, pos, ea, eb, valid_rows, total, router_wt, layer, w_gate, w_up, w_down):
    n = hext.shape[0] // ROW_SUB
    max_tiles = ea.shape[0]

    def wspec(shape, which):
        if which == 0:
            return pl.BlockSpec((None, 1) + shape,
                                lambda j, inv_, ea_, eb_, nv_, t_: (layer, ea_[j], 0, 0))
        return pl.BlockSpec((None, 1) + shape, lambda j, inv_, ea_, eb_, nv_, t_: (layer, eb_[j], 0, 0))

    up = (D_MODEL, D_EXPERT)
    down = (D_EXPERT, D_MODEL)
    grid_spec = pltpu.PrefetchScalarGridSpec(
        num_scalar_prefetch=5,
        grid=(max_tiles,),
        in_specs=[pl.BlockSpec(memory_space=pl.ANY),
                  pl.BlockSpec((N_EXPERTS, D_MODEL), lambda j, inv_, ea_, eb_, nv_, t_: (0, 0)),
                  wspec(up, 0), wspec(up, 0), wspec(down, 0),
                  wspec(up, 1), wspec(up, 1), wspec(down, 1)],
        out_specs=pl.BlockSpec(memory_space=pl.ANY),
        scratch_shapes=[pltpu.VMEM((GATHER_BUFS, MOE_TILE * ROW_SUB, LANES), F32),
                        pltpu.VMEM((2, MOE_TILE * ROW_SUB, LANES), F32),
                        pltpu.SMEM((max_tiles * MOE_TILE,), jnp.int32),
                        pltpu.SemaphoreType.DMA((GATHER_BUFS,)),
                        pltpu.SemaphoreType.DMA((2,))],
    )
    return pl.pallas_call(
        _moe_kernel,
        grid_spec=grid_spec,
        out_shape=jax.ShapeDtypeStruct((n * ROW_SUB, LANES), F32),
        compiler_params=_cparams(("arbitrary",)), name="moe_experts",
    )(pos, ea, eb, valid_rows, total, hext, router_wt, w_gate, w_up, w_down, w_gate, w_up, w_down)


def _final_kernel(x_ref, y_ref, g_ref, o_ref):
    o_ref[...] = x_ref[...] + g_ref[0] * _token_rows(y_ref)


def _final(x, y, gate, seq):
    n = x.shape[0]
    tpb = seq // ROW_TILE
    row = pl.BlockSpec((ROW_TILE, D_MODEL), lambda i: (i, 0))
    return pl.pallas_call(
        _final_kernel, grid=(n // ROW_TILE,),
        in_specs=[row, pl.BlockSpec((ROW_TILE * ROW_SUB, LANES), lambda i: (i, 0)),
                  pl.BlockSpec((1, 1, D_MODEL), lambda i: (i // tpb, 0, 0))],
        out_specs=row, out_shape=jax.ShapeDtypeStruct((n, D_MODEL), F32),
        compiler_params=_cparams(("arbitrary",)), name="final_residual",
    )(x, y, gate)


def kernel(x, c, ada_w, ada_b, norm1_g, norm2_g, even_w_in, even_w_out, att_q_norm_g, att_k_norm_g,
           att_rel_bias, odd_w_in, odd_w_out, router_w, router_b, exp_w_gate, exp_w_up, exp_w_down):
    bsz, seq, d = x.shape
    n = bsz * seq
    mod = _modulation(c, ada_w, ada_b)
    mods = [[mod[l, :, k * d:(k + 1) * d].reshape(bsz, 1, d) for k in range(6)] for l in range(2)]
    xf = x.reshape(n, d)
    w_gate, w_up, w_down = (w.astype(BF16) for w in (exp_w_gate, exp_w_up, exp_w_down))

    def moe_layer(layer, mix_a, mix_b, col_a, col_b, w_out, xin):
        sh1, sc1, g1, sh2, sc2, g2 = mods[layer]
        x1, hext, route, counts = _out_route(mix_a, mix_b, col_a, col_b, w_out.astype(BF16), xin, g1,
                                             norm2_g[layer], sc2, sh2, router_w, router_b, seq)
        pos, ea, eb, valid_rows, total = _routing_plan(route, counts, n)
        y = _moe(hext, pos, ea, eb, valid_rows, total, router_w.T.astype(F32), layer,
                 w_gate, w_up, w_down)
        return x1, y, g2

    sh1, sc1, _, _, _, _ = mods[0]
    qk_gain = jnp.stack([att_q_norm_g[0], att_k_norm_g[0]])
    proj0 = _project(xf, norm1_g[0], sc1, sh1, even_w_in[0].astype(BF16), seq, qk_gain=qk_gain)
    ret = _retention(proj0, bsz, seq)
    att = _chunk_attention(proj0, att_rel_bias[0], bsz, seq)
    x1, y0, g2_0 = moe_layer(0, ret, att, 0, 0, even_w_out[0], xf)

    sh1, sc1, _, _, _, _ = mods[1]
    x2, proj1 = _project(x1, norm1_g[1], sc1, sh1, odd_w_in[0].astype(BF16), seq, y=y0, gate=g2_0)
    sbo = _stick_breaking(proj1, bsz, seq)
    x3, y1, g2_1 = moe_layer(1, sbo, sbo, 0, 1, odd_w_out[0], x2)
    return _final(x3, y1, g2_1, seq).reshape(bsz, seq, d)
```

```python
import functools

import numpy as np
import jax
import jax.numpy as jnp
from jax import lax
from jax.experimental import pallas as pl
from jax.experimental.pallas import tpu as pltpu

F32 = jnp.float32
BF16 = jnp.bfloat16

D_MODEL = 1024
CHUNK = 64
EPS = 1e-6
ROPE_BASE = 10000.0
RET_HEADS = 4
RET_DIM = 128
ATT_HEADS = 8
ATT_DIM = 64
ATT_LEFT_CHUNKS = 8
REL_CLIP = 128
SB_HEADS = 16
SB_DIM = 64
N_EXPERTS = 16
N_GROUPS = 4
D_EXPERT = 512
HALF = 512

VMEM_LIMIT = 48 * 1024 * 1024
LANES = 128

ROW_TILE = 512
RET_TILE = 256
ATT_QB = 256
ATT_WIN = ATT_LEFT_CHUNKS * CHUNK + ATT_QB
ATT_STEP_BLOCKS = 4
SB_BLK = 128
SB_ROWS = 64
SB_UNROLL = 8
SB_WIN = 3 * SB_BLK
SB_EXIT = -110.0

PAIR_A = (0, 0, 0, 1, 1, 2)
PAIR_B = (1, 2, 3, 2, 3, 3)
N_CLASS = N_GROUPS * len(PAIR_A)
CLASS_ROWS = 32
MOE_TILE = 256
ROW_SUB = D_MODEL // LANES
GATHER_BUFS = 3


def _cparams(sem):
    return pltpu.CompilerParams(dimension_semantics=sem, vmem_limit_bytes=VMEM_LIMIT)


def _mod_kernel(c_ref, w_ref, b_ref, o_ref):
    c = c_ref[...]
    ca = c * jax.nn.sigmoid(c)
    o_ref[0] = jnp.dot(ca.astype(BF16), w_ref[0].astype(BF16), preferred_element_type=F32) + b_ref[0]


def _modulation(c, ada_w, ada_b):
    depth, _, width = ada_w.shape
    bsz = c.shape[0]
    rows = 8
    cp = jnp.zeros((rows, D_MODEL), F32).at[:bsz].set(c)
    tn = 1536
    out = pl.pallas_call(
        _mod_kernel,
        grid=(depth, width // tn),
        in_specs=[pl.BlockSpec((rows, D_MODEL), lambda l, j: (0, 0)),
                  pl.BlockSpec((1, D_MODEL, tn), lambda l, j: (l, 0, j)),
                  pl.BlockSpec((1, 1, tn), lambda l, j: (l, 0, j))],
        out_specs=pl.BlockSpec((1, rows, tn), lambda l, j: (l, 0, j)),
        out_shape=jax.ShapeDtypeStruct((depth, rows, width), F32),
        compiler_params=_cparams(("arbitrary", "arbitrary")),
        name="adaln_mod",
    )(cp, ada_w, ada_b.reshape(depth, 1, width))
    return out[:, :bsz]


def _norm_mod(x, g, sc, sh):
    ms = jnp.mean(x * x, axis=-1, keepdims=True)
    return (x * lax.rsqrt(ms + EPS) * g) * (1.0 + sc) + sh


def _project_chunks(h, w_ref, o_ref, qk_ref, bd_ref, q_scale):
    for ci, n0 in enumerate(range(0, o_ref.shape[1], HALF)):
        p = jnp.dot(h, w_ref[:, n0:n0 + HALF], preferred_element_type=F32)
        if qk_ref is not None and ci in (4, 5):
            ms = jnp.dot((p * p).astype(BF16), bd_ref[...], preferred_element_type=F32)
            p = p * lax.rsqrt(ms + EPS) * qk_ref[ci - 4:ci - 3, :]
            if ci == 4:
                p = p * q_scale
        elif qk_ref is None and n0 < SB_HEADS * SB_DIM:
            p = p * q_scale
        o_ref[:, n0:n0 + HALF] = p.astype(BF16)


def _proj_kernel(x_ref, g_ref, sc_ref, sh_ref, w_ref, qk_ref, bd_ref, o_ref):
    h = _norm_mod(x_ref[...], g_ref[...], sc_ref[0], sh_ref[0]).astype(BF16)
    _project_chunks(h, w_ref, o_ref, qk_ref, bd_ref, ATT_DIM ** -0.5)


def _token_rows(y_ref):
    rows = y_ref.shape[0] // ROW_SUB
    return jnp.concatenate([y_ref[pl.ds(jt, rows, stride=ROW_SUB), :] for jt in range(ROW_SUB)],
                           axis=1)


def _proj_res_kernel(x_ref, y_ref, gt_ref, g_ref, sc_ref, sh_ref, w_ref, xo_ref, o_ref):
    x = x_ref[...] + gt_ref[0] * _token_rows(y_ref)
    xo_ref[...] = x
    h = _norm_mod(x, g_ref[...], sc_ref[0], sh_ref[0]).astype(BF16)
    _project_chunks(h, w_ref, o_ref, None, None, SB_DIM ** -0.5)


def _project(x, gain, sc, sh, w, seq, qk_gain=None, y=None, gate=None):
    n = x.shape[0]
    nout = w.shape[1]
    tpb = seq // ROW_TILE
    row = pl.BlockSpec((ROW_TILE, D_MODEL), lambda i: (i, 0))
    per_b = pl.BlockSpec((1, 1, D_MODEL), lambda i: (i // tpb, 0, 0))
    gspec = pl.BlockSpec((1, D_MODEL), lambda i: (0, 0))
    wspec = pl.BlockSpec((D_MODEL, nout), lambda i: (0, 0))
    ospec = pl.BlockSpec((ROW_TILE, nout), lambda i: (i, 0))
    oshape = jax.ShapeDtypeStruct((n, nout), BF16)
    if y is None:
        head = np.arange(HALF) // ATT_DIM
        bd = jnp.asarray((head[:, None] == head[None, :]).astype(np.float32) / ATT_DIM, BF16)
        qk = jnp.tile(qk_gain.astype(F32), (1, ATT_HEADS))
        return pl.pallas_call(
            _proj_kernel, grid=(n // ROW_TILE,),
            in_specs=[row, gspec, per_b, per_b, wspec,
                      pl.BlockSpec((2, HALF), lambda i: (0, 0)),
                      pl.BlockSpec((HALF, HALF), lambda i: (0, 0))],
            out_specs=ospec, out_shape=oshape,
            compiler_params=_cparams(("arbitrary",)), name="norm_proj",
        )(x, gain.reshape(1, D_MODEL), sc, sh, w, qk, bd)
    return pl.pallas_call(
        _proj_res_kernel, grid=(n // ROW_TILE,),
        in_specs=[row, pl.BlockSpec((ROW_TILE * ROW_SUB, LANES), lambda i: (i, 0)), per_b, gspec,
                  per_b, per_b, wspec],
        out_specs=[row, ospec],
        out_shape=[jax.ShapeDtypeStruct((n, D_MODEL), F32), oshape],
        compiler_params=_cparams(("arbitrary",)), name="res_norm_proj",
    )(x, y, gate, gain.reshape(1, D_MODEL), sc, sh, w)


def _ret_tables(seq):
    inv = ROPE_BASE ** (-np.arange(0, RET_DIM, 2, dtype=np.float64) / RET_DIM)
    ang = np.arange(seq, dtype=np.float64)[:, None] * inv[None, :]
    cos = np.concatenate([np.cos(ang), np.cos(ang)], axis=1)
    sin = np.concatenate([-np.sin(ang), np.sin(ang)], axis=1)
    log_g = np.log(1.0 - 2.0 ** (-5.0 - np.arange(RET_HEADS, dtype=np.float64)))
    idx = np.arange(RET_TILE)
    same_or_earlier = (idx[None, :] // CHUNK) <= (idx[:, None] // CHUNK)
    dec = np.exp(log_g[:, None, None] * np.abs(idx[:, None] - idx[None, :])) * same_or_earlier
    loc = idx.astype(np.float64)
    qdec = np.exp(log_g[:, None] * (loc[None, :] + 1.0))
    kdec = np.exp(log_g[:, None] * (RET_TILE - 1.0 - loc[None, :]))
    tdec = np.exp(log_g * RET_TILE)
    qk = np.stack([qdec, kdec], axis=1)[..., None] * np.ones((1, 1, 1, RET_DIM))
    return (jnp.asarray(cos, F32), jnp.asarray(sin, F32), jnp.asarray(dec, F32),
            jnp.asarray(qk, F32), tuple(float(t) for t in tdec))


def _ret_kernel(tdec, q_ref, k_ref, v_ref, g_ref, cos_ref, sin_ref, dec_ref, qk_ref, o_ref, st_ref):
    @pl.when(pl.program_id(1) == 0)
    def _():
        st_ref[...] = jnp.zeros_like(st_ref)

    cos = cos_ref[...]
    sin = sin_ref[...]
    half = RET_DIM // 2
    for h in range(RET_HEADS):
        cols = slice(h * RET_DIM, (h + 1) * RET_DIM)
        q = q_ref[:, cols].astype(F32)
        k = k_ref[:, cols].astype(F32)
        q = q * cos + pltpu.roll(q, half, 1) * sin
        k = (k * cos + pltpu.roll(k, half, 1) * sin) * (RET_DIM ** -0.5)
        v = v_ref[:, cols]
        qb = q.astype(BF16)
        kb = k.astype(BF16)
        s = lax.dot_general(qb, kb, (((1,), (1,)), ((), ())), preferred_element_type=F32)
        s = s * dec_ref[h]
        o = jnp.dot(s.astype(BF16), v, preferred_element_type=F32)
        st = st_ref[h]
        o = o + jnp.dot((q * qk_ref[h, 0]).astype(BF16), st.astype(BF16), preferred_element_type=F32)
        kd = (k * qk_ref[h, 1]).astype(BF16)
        st_ref[h] = st * tdec[h] + lax.dot_general(kd, v, (((0,), (0,)), ((), ())),
                                                   preferred_element_type=F32)
        o = o * lax.rsqrt(jnp.mean(o * o, axis=-1, keepdims=True) + EPS)
        g = g_ref[:, cols].astype(F32)
        o_ref[:, cols] = (o * (g * jax.nn.sigmoid(g))).astype(BF16)


def _retention(proj, bsz, seq):
    n = proj.shape[0]
    nt = seq // RET_TILE
    cos, sin, dec, qk, tdec = _ret_tables(seq)

    def col(c):
        return pl.BlockSpec((RET_TILE, HALF), lambda b, i: (b * nt + i, c))

    pos = pl.BlockSpec((RET_TILE, RET_DIM), lambda b, i: (i, 0))
    return pl.pallas_call(
        functools.partial(_ret_kernel, tdec),
        grid=(bsz, nt),
        in_specs=[col(0), col(1), col(2), col(3), pos, pos,
                  pl.BlockSpec((RET_HEADS, RET_TILE, RET_TILE), lambda b, i: (0, 0, 0)),
                  pl.BlockSpec((RET_HEADS, 2, RET_TILE, RET_DIM), lambda b, i: (0, 0, 0, 0))],
        out_specs=pl.BlockSpec((RET_TILE, HALF), lambda b, i: (b * nt + i, 0)),
        out_shape=jax.ShapeDtypeStruct((n, HALF), BF16),
        scratch_shapes=[pltpu.VMEM((RET_HEADS, RET_DIM, RET_DIM), F32)],
        compiler_params=_cparams(("arbitrary", "arbitrary")), name="retention",
    )(proj, proj, proj, proj, cos, sin, dec, qk)


def _att_bias(rel_bias):
    span = ATT_WIN + ATT_QB
    m = np.concatenate([np.arange(0, ATT_WIN), np.zeros(1, np.int64), np.arange(-(ATT_QB - 1), 0)])
    idx = np.clip((ATT_WIN - ATT_QB) - m, -(CHUNK - 1), REL_CLIP) + (CHUNK - 1)
    vec = rel_bias[:, idx].astype(F32)
    heads = rel_bias.shape[0]
    toep = jnp.tile(vec, (1, ATT_QB))[:, :ATT_QB * (span - 1)].reshape(heads, ATT_QB, span - 1)
    qc = np.arange(ATT_QB)[:, None] // CHUNK
    kc = np.arange(ATT_WIN)[None, :] // CHUNK - ATT_LEFT_CHUNKS
    ok = (kc <= qc) & (kc >= qc - ATT_LEFT_CHUNKS)
    return jnp.where(jnp.asarray(ok)[None], toep[:, :, :ATT_WIN], -jnp.inf)


def _fill_padded(pad_ref, src_ref, front):
    pad_ref[:front, :] = jnp.zeros((front, pad_ref.shape[1]), pad_ref.dtype)
    pad_ref[front:, :] = src_ref[...]


def _att_kernel(q_ref, k_ref, v_ref, bias_ref, o_ref, kpad, vpad):
    i = pl.program_id(1)
    front = ATT_WIN - ATT_QB

    @pl.when(i == 0)
    def _():
        _fill_padded(kpad, k_ref, front)
        _fill_padded(vpad, v_ref, front)

    col = lax.broadcasted_iota(jnp.int32, (1, ATT_WIN), 1)
    first = lax.broadcasted_iota(jnp.int32, (ATT_QB, LANES), 1) < ATT_DIM
    nt = (((1,), (1,)), ((), ()))
    for sub in range(ATT_STEP_BLOCKS):
        rows = slice(sub * ATT_QB, (sub + 1) * ATT_QB)
        start = pl.multiple_of((i * ATT_STEP_BLOCKS + sub) * ATT_QB, ATT_QB)
        pen = jnp.where(col >= front - start, 0.0, -jnp.inf)
        for p in range(ATT_HEADS * ATT_DIM // LANES):
            cols = slice(p * LANES, (p + 1) * LANES)
            q = q_ref[rows, cols]
            kw = kpad[pl.ds(start, ATT_WIN), cols]
            vw = vpad[pl.ds(start, ATT_WIN), cols]
            zero = jnp.zeros_like(q)
            qq = jnp.concatenate([jnp.where(first, q, zero), jnp.where(first, zero, q)], axis=0)
            bias = jnp.concatenate([bias_ref[2 * p], bias_ref[2 * p + 1]], axis=0)
            s = lax.dot_general(qq, kw, nt, preferred_element_type=F32) + bias + pen
            e = jnp.exp(s - jnp.max(s, axis=-1, keepdims=True))
            den = jnp.sum(e, axis=-1, keepdims=True)
            o = jnp.dot(e.astype(BF16), vw, preferred_element_type=F32) / den
            o_ref[rows, cols] = jnp.where(first, o[:ATT_QB], o[ATT_QB:]).astype(BF16)


def _chunk_attention(proj, rel_bias, bsz, seq):
    n = proj.shape[0]
    rows = ATT_STEP_BLOCKS * ATT_QB
    nq = seq // rows
    front = ATT_WIN - ATT_QB
    return pl.pallas_call(
        _att_kernel,
        grid=(bsz, nq),
        in_specs=[pl.BlockSpec((rows, HALF), lambda b, i: (b * nq + i, 4)),
                  pl.BlockSpec((seq, HALF), lambda b, i: (b, 5)),
                  pl.BlockSpec((seq, HALF), lambda b, i: (b, 6)),
                  pl.BlockSpec((ATT_HEADS, ATT_QB, ATT_WIN), lambda b, i: (0, 0, 0))],
        out_specs=pl.BlockSpec((rows, HALF), lambda b, i: (b * nq + i, 0)),
        out_shape=jax.ShapeDtypeStruct((n, HALF), BF16),
        scratch_shapes=[pltpu.VMEM((front + seq, HALF), BF16), pltpu.VMEM((front + seq, HALF), BF16)],
        compiler_params=_cparams(("arbitrary", "arbitrary")), name="chunk_attention",
    )(proj, proj, proj, _att_bias(rel_bias))


def _sb_logs(z):
    log_beta = jnp.minimum(z, 0.0) - jnp.log(1.0 + jnp.exp(-jnp.abs(z)))
    return log_beta, log_beta - z


def _sb_kernel(q_ref, k_ref, v_ref, tri_ref, o_ref, kpad, vpad, run_ref, acc_ref):
    front = SB_WIN - SB_ROWS
    nblk = SB_WIN // SB_BLK
    pairs = HALF // LANES
    _fill_padded(kpad, k_ref, front)
    _fill_padded(vpad, v_ref, front)
    row = lax.broadcasted_iota(jnp.int32, (SB_ROWS, SB_WIN), 0)
    col = lax.broadcasted_iota(jnp.int32, (SB_ROWS, SB_WIN), 1)
    causal_col = jnp.where(col - row < front, col, -1)
    first = lax.broadcasted_iota(jnp.int32, (SB_ROWS, LANES), 1) < SB_DIM
    nt = (((1,), (1,)), ((), ()))

    def first_window(iq, u):
        start = pl.multiple_of(iq * SB_ROWS, SB_ROWS)
        mask = causal_col >= jnp.maximum(front - start, 0)
        tri = tri_ref[...]

        def heads_of(p):
            q = q_ref[pl.ds(start, SB_ROWS), p * LANES:(p + 1) * LANES]
            zero = jnp.zeros_like(q)
            return jnp.where(first, q, zero), jnp.where(first, zero, q)

        alive = None
        mask2 = jnp.concatenate([mask, mask], axis=0)
        lbs, l1s, vws = [], [], []
        for p in range(pairs):
            cols = slice(p * LANES, (p + 1) * LANES)
            kw = kpad[pl.ds(start, SB_WIN), cols]
            vws.append(vpad[pl.ds(start, SB_WIN), cols])
            qq = jnp.concatenate(heads_of(p), axis=0)
            z = lax.dot_general(qq, kw, nt, preferred_element_type=F32)
            log_beta, log_1mb = _sb_logs(z)
            lbs.append(log_beta)
            l1 = jnp.where(mask2, log_1mb, 0.0).astype(BF16)
            l1s += [l1[hh * SB_ROWS:(hh + 1) * SB_ROWS, c * SB_BLK:(c + 1) * SB_BLK]
                    for hh in range(2) for c in range(nblk)]
        cs = jnp.dot(jnp.concatenate(l1s, axis=0), tri, preferred_element_type=F32)
        for p in range(pairs):
            rows = []
            for hh in range(2):
                h = 2 * p + hh
                run = jnp.zeros((SB_ROWS, SB_BLK), F32)
                accs = [None] * nblk
                for c in reversed(range(nblk)):
                    blk = cs[(h * nblk + c) * SB_ROWS:(h * nblk + c + 1) * SB_ROWS]
                    accs[c] = blk[:, :SB_BLK] + run
                    run = run + blk[:, SB_BLK:]
                rows.append(jnp.concatenate(accs, axis=1))
                run_ref[u, h] = run
                alive = run if alive is None else jnp.maximum(alive, run)
            a = jnp.where(mask2, jnp.exp(lbs[p] + jnp.concatenate(rows, axis=0)), 0.0)
            o = jnp.dot(a.astype(BF16), vws[p], preferred_element_type=F32)
            acc_ref[u, 2 * p] = o[:SB_ROWS]
            acc_ref[u, 2 * p + 1] = o[SB_ROWS:]
        return jnp.max(alive)

    def older_keys(iq, u, alive0):
        start = pl.multiple_of(iq * SB_ROWS, SB_ROWS)
        tri = tri_ref[...]

        def heads_of(p):
            q = q_ref[pl.ds(start, SB_ROWS), p * LANES:(p + 1) * LANES]
            zero = jnp.zeros_like(q)
            return jnp.where(first, q, zero), jnp.where(first, zero, q)

        def cond(c):
            return jnp.logical_and(c[0] >= 0, c[1] > SB_EXIT)

        def body(c):
            j = c[0]
            kstart = pl.multiple_of(front + j * SB_ROWS, SB_ROWS)
            alive = None
            for p in range(pairs):
                cols = slice(p * LANES, (p + 1) * LANES)
                kb = kpad[pl.ds(kstart, SB_ROWS), cols]
                vb = vpad[pl.ds(kstart, SB_ROWS), cols]
                for hh, qm in enumerate(heads_of(p)):
                    z = lax.dot_general(qm, kb, nt, preferred_element_type=F32)
                    log_beta, log_1mb = _sb_logs(z)
                    cs = jnp.dot(log_1mb.astype(BF16), tri[:SB_ROWS], preferred_element_type=F32)
                    run = run_ref[u, 2 * p + hh]
                    a = jnp.exp(log_beta + run[:, :SB_ROWS] + cs[:, :SB_ROWS])
                    acc_ref[u, 2 * p + hh] += jnp.dot(a.astype(BF16), vb, preferred_element_type=F32)
                    run = run + cs[:, SB_BLK:]
                    run_ref[u, 2 * p + hh] = run
                    alive = run if alive is None else jnp.maximum(alive, run)
            return j - 1, jnp.max(alive)

        lax.while_loop(cond, body, (iq - SB_WIN // SB_ROWS, alive0))
        for p in range(pairs):
            o_ref[pl.ds(start, SB_ROWS), p * LANES:(p + 1) * LANES] = jnp.where(
                first, acc_ref[u, 2 * p], acc_ref[u, 2 * p + 1]).astype(BF16)

    def query_blocks(it, carry):
        alive = [first_window(it * SB_UNROLL + u, u) for u in range(SB_UNROLL)]
        for u in range(SB_UNROLL):
            older_keys(it * SB_UNROLL + u, u, alive[u])
        return carry

    lax.fori_loop(0, q_ref.shape[0] // (SB_ROWS * SB_UNROLL), query_blocks, 0)


def _stick_breaking(proj, bsz, seq):
    n = proj.shape[0]
    groups = SB_HEADS * SB_DIM // HALF
    front = SB_WIN - SB_ROWS
    j = np.arange(SB_BLK)
    tri = np.concatenate([(j[:, None] > j[None, :]).astype(np.float32),
                          np.ones((SB_BLK, SB_BLK), np.float32)], axis=1)
    return pl.pallas_call(
        _sb_kernel,
        grid=(bsz, groups),
        in_specs=[pl.BlockSpec((seq, HALF), lambda b, g: (b, g)),
                  pl.BlockSpec((seq, HALF), lambda b, g: (b, groups + g)),
                  pl.BlockSpec((seq, HALF), lambda b, g: (b, 2 * groups + g)),
                  pl.BlockSpec((SB_BLK, 2 * SB_BLK), lambda b, g: (0, 0))],
        out_specs=pl.BlockSpec((seq, HALF), lambda b, g: (b, g)),
        out_shape=jax.ShapeDtypeStruct((n, SB_HEADS * SB_DIM), BF16),
        scratch_shapes=[pltpu.VMEM((front + seq, HALF), BF16), pltpu.VMEM((front + seq, HALF), BF16),
                        pltpu.VMEM((SB_UNROLL, HALF // SB_DIM, SB_ROWS, SB_BLK), F32),
                        pltpu.VMEM((SB_UNROLL, HALF // SB_DIM, SB_ROWS, LANES), F32)],
        compiler_params=_cparams(("arbitrary", "arbitrary")), name="stick_breaking",
    )(proj, proj, proj, jnp.asarray(tri, BF16))


def _out_route_kernel(ma_ref, mb_ref, wa_ref, wb_ref, x_ref, g1_ref, g_ref, sc_ref, sh_ref,
                      rw_ref, rb_ref, tri_ref, xo_ref, he_ref, rt_ref, cnt_ref, carry_ref):
    @pl.when(pl.program_id(0) == 0)
    def _():
        carry_ref[...] = jnp.zeros_like(carry_ref)

    tm = x_ref.shape[0]
    mix = (jnp.dot(ma_ref[...], wa_ref[...], preferred_element_type=F32)
           + jnp.dot(mb_ref[...], wb_ref[...], preferred_element_type=F32))
    x = x_ref[...] + g1_ref[0] * mix
    xo_ref[...] = x
    h = _norm_mod(x, g_ref[...], sc_ref[0], sh_ref[0])
    for jt in range(ROW_SUB):
        he_ref[pl.ds(jt, tm, stride=ROW_SUB), :] = h[:, jt * LANES:(jt + 1) * LANES]

    nt = (((1,), (1,)), ((), ()))
    h_hi = h.astype(BF16)
    h_lo = (h - h_hi.astype(F32)).astype(BF16)
    w = rw_ref[...]
    w_hi = w.astype(BF16)
    w_lo = (w - w_hi.astype(F32)).astype(BF16)
    logits = (lax.dot_general(w_hi, h_hi, nt, preferred_element_type=F32)
              + lax.dot_general(w_hi, h_lo, nt, preferred_element_type=F32)
              + lax.dot_general(w_lo, h_hi, nt, preferred_element_type=F32))
    sel = jax.nn.sigmoid(logits) + rb_ref[...]

    best = None
    for g in range(N_GROUPS):
        for p in range(len(PAIR_A)):
            ea, eb = 4 * g + PAIR_A[p], 4 * g + PAIR_B[p]
            val = sel[ea:ea + 1, :] + sel[eb:eb + 1, :]
            cid = jnp.full((1, tm), float(len(PAIR_A) * g + p), F32)
            cand = (val, cid)
            if best is None:
                best = cand
            else:
                take = val > best[0]
                best = tuple(jnp.where(take, c_, b_) for c_, b_ in zip(cand, best))
    cls = best[1]

    crow = lax.broadcasted_iota(jnp.int32, (CLASS_ROWS, tm), 0).astype(F32)
    onehot = jnp.where(crow == cls, 1.0, 0.0)
    before = jnp.dot(onehot.astype(BF16), tri_ref[...], preferred_element_type=F32)
    carry = carry_ref[...]
    rank = jnp.sum(onehot * (before + carry[:, :1]), axis=0, keepdims=True)
    carry = carry + jnp.sum(onehot, axis=1, keepdims=True)
    carry_ref[...] = carry
    cnt_ref[0] = carry

    srow = lax.broadcasted_iota(jnp.int32, (8, tm), 0)
    rt_ref[0] = jnp.where(srow == 0, cls, jnp.where(srow == 1, rank, 0.0))


def _out_route(mix_a, mix_b, col_a, col_b, w_out, x, g1, gain, sc, sh, router_w, router_b, seq):
    n = x.shape[0]
    tm = ROW_TILE
    nt = n // tm
    tpb = seq // tm
    t = np.arange(tm)
    tri = jnp.asarray((t[:, None] < t[None, :]).astype(np.float32), BF16)
    row = pl.BlockSpec((tm, D_MODEL), lambda i: (i, 0))
    per_b = pl.BlockSpec((1, 1, D_MODEL), lambda i: (i // tpb, 0, 0))
    rb = jnp.broadcast_to(router_b.astype(F32)[:, None], (N_EXPERTS, tm))
    return pl.pallas_call(
        _out_route_kernel,
        grid=(nt,),
        in_specs=[pl.BlockSpec((tm, HALF), lambda i: (i, col_a)),
                  pl.BlockSpec((tm, HALF), lambda i: (i, col_b)),
                  pl.BlockSpec((HALF, D_MODEL), lambda i: (0, 0)),
                  pl.BlockSpec((HALF, D_MODEL), lambda i: (1, 0)),
                  row, per_b,
                  pl.BlockSpec((1, D_MODEL), lambda i: (0, 0)), per_b, per_b,
                  pl.BlockSpec((N_EXPERTS, D_MODEL), lambda i: (0, 0)),
                  pl.BlockSpec((N_EXPERTS, tm), lambda i: (0, 0)),
                  pl.BlockSpec((tm, tm), lambda i: (0, 0))],
        out_specs=[row,
                   pl.BlockSpec((tm * ROW_SUB, LANES), lambda i: (i, 0)),
                   pl.BlockSpec((1, 8, tm), lambda i: (i, 0, 0)),
                   pl.BlockSpec((1, CLASS_ROWS, LANES), lambda i: (i, 0, 0))],
        out_shape=[jax.ShapeDtypeStruct((n, D_MODEL), F32),
                   jax.ShapeDtypeStruct((n * ROW_SUB, LANES), F32),
                   jax.ShapeDtypeStruct((nt, 8, tm), F32),
                   jax.ShapeDtypeStruct((nt, CLASS_ROWS, LANES), F32)],
        scratch_shapes=[pltpu.VMEM((CLASS_ROWS, LANES), F32)],
        compiler_params=_cparams(("arbitrary",)), name="out_proj_route",
    )(mix_a, mix_b, w_out, w_out, x, g1, gain.reshape(1, D_MODEL), sc, sh,
      router_w.T.astype(F32), rb, tri)


def _fill_slots(pos_ref, nv_ref, inv_ref):
    batch = 16

    def fill_tile(j, c):
        def fill(g, c2):
            for u in range(batch):
                inv_ref[j * MOE_TILE + g * batch + u] = 0
            return c2

        return lax.fori_loop(nv_ref[j] // batch, MOE_TILE // batch, fill, c)

    lax.fori_loop(0, nv_ref.shape[0], fill_tile, 0)

    def place(g, c):
        slots = [pos_ref[g * batch + u] for u in range(batch)]
        for u in range(batch):
            inv_ref[slots[u]] = g * batch + u
        return c

    lax.fori_loop(0, pos_ref.shape[0] // batch, place, 0)


def _routing_plan(route, counts, n):
    nt, _, tm = route.shape
    cls = route[:, 0, :].reshape(n).astype(jnp.int32)
    rank = route[:, 1, :].reshape(n).astype(jnp.int32)
    cnt = counts[-1, :N_CLASS, 0].astype(jnp.int32)
    tiles_c = (cnt + MOE_TILE - 1) // MOE_TILE
    ends = jnp.cumsum(tiles_c)
    starts = ends - tiles_c
    pos = starts[cls] * MOE_TILE + rank
    max_tiles = n // MOE_TILE + N_CLASS
    total = ends[-1]
    j = jnp.arange(max_tiles, dtype=jnp.int32)
    jj = jnp.minimum(j, total - 1)
    tcls = jnp.sum((ends[None, :] <= jj[:, None]).astype(jnp.int32), axis=1)
    grp = tcls // len(PAIR_A)
    pair = tcls % len(PAIR_A)
    ea = 4 * grp + jnp.asarray(PAIR_A, jnp.int32)[pair]
    eb = 4 * grp + jnp.asarray(PAIR_B, jnp.int32)[pair]
    valid_rows = jnp.clip(cnt[tcls] - (jj - starts[tcls]) * MOE_TILE, 0, MOE_TILE)
    valid_rows = jnp.where(j < total, valid_rows, 0).astype(jnp.int32)
    return pos, ea, eb, valid_rows, total.reshape(1)


def _moe_kernel(pos_ref, ea_ref, eb_ref, nv_ref, tot_ref, h_hbm, rw_ref, wg_a, wu_a, wd_a,
                wg_b, wu_b, wd_b, y_hbm, gbuf, ybuf, inv_ref, gsem, ssem):
    j = pl.program_id(0)
    total = tot_ref[0]
    gslot = j % GATHER_BUFS
    yslot = j % 2
    unroll = 8

    def gather_row(tile, s, r, prio):
        src = pl.multiple_of(inv_ref[tile * MOE_TILE + r] * ROW_SUB, ROW_SUB)
        dst = pl.multiple_of(r * ROW_SUB, ROW_SUB)
        pltpu.make_async_copy(h_hbm.at[pl.ds(src, ROW_SUB), :], gbuf.at[s, pl.ds(dst, ROW_SUB), :],
                              gsem.at[s]).start(priority=prio)

    def scatter_row(tile, s, r, prio):
        src = pl.multiple_of(r * ROW_SUB, ROW_SUB)
        dst = pl.multiple_of(inv_ref[tile * MOE_TILE + r] * ROW_SUB, ROW_SUB)
        pltpu.make_async_copy(ybuf.at[s, pl.ds(src, ROW_SUB), :], y_hbm.at[pl.ds(dst, ROW_SUB), :],
                              ssem.at[s]).start(priority=prio)

    def gather_rows(tile, s, first_group):
        def group(g, c):
            for u in range(unroll):
                gather_row(tile, s, g * unroll + u, u % 2)
            return c

        lax.fori_loop(first_group, MOE_TILE // unroll, group, 0)

    def gather_wait(s):
        pltpu.make_async_copy(h_hbm.at[pl.ds(0, MOE_TILE * ROW_SUB), :], gbuf.at[s],
                              gsem.at[s]).wait()

    def issue_step(with_gather):
        nv = nv_ref[j]
        groups = nv // unroll
        ahead = j + GATHER_BUFS - 1
        aslot = ahead % GATHER_BUFS

        def group(g, c):
            for u in range(unroll):
                scatter_row(j, yslot, g * unroll + u, u % 2)
                if with_gather:
                    gather_row(ahead, aslot, g * unroll + u, (u + 1) % 2)
            return c

        lax.fori_loop(0, groups, group, 0)
        for u in range(unroll):
            @pl.when(groups * unroll + u < nv)
            def _():
                scatter_row(j, yslot, groups * unroll + u, u % 2)
        if with_gather:
            gather_rows(ahead, aslot, groups)

    def scatter_wait(tile, s):
        nv = nv_ref[tile]
        for bit in range(MOE_TILE.bit_length()):
            rows = 1 << bit

            @pl.when((nv >> bit) & 1 == 1)
            def _():
                pltpu.make_async_copy(ybuf.at[s, pl.ds(0, rows * ROW_SUB), :],
                                      y_hbm.at[pl.ds(0, rows * ROW_SUB), :], ssem.at[s]).wait()

    @pl.when(j == 0)
    def _():
        _fill_slots(pos_ref, nv_ref, inv_ref)
        for t in range(GATHER_BUFS - 1):
            @pl.when(t < total)
            def _():
                gather_rows(t, t, 0)

    @pl.when(j < total)
    def _():
        gather_wait(gslot)
        x = jnp.concatenate([gbuf[gslot, pl.ds(jt, MOE_TILE, stride=ROW_SUB), :]
                             for jt in range(ROW_SUB)], axis=1)
        xb = x.astype(BF16)
        scores = [jax.nn.sigmoid(jnp.sum(x * rw_ref[pl.ds(e_ref[j], 1), :], axis=-1, keepdims=True))
                  for e_ref in (ea_ref, eb_ref)]
        shares = [sc_ / (scores[0] + scores[1]) for sc_ in scores]
        y = None
        for lane, (wg, wu, wd) in enumerate(((wg_a, wu_a, wd_a), (wg_b, wu_b, wd_b))):
            g = jnp.dot(xb, wg[0], preferred_element_type=F32)
            u = jnp.dot(xb, wu[0], preferred_element_type=F32)
            he = ((g * jax.nn.sigmoid(g)) * u * shares[lane]).astype(BF16)
            part = jnp.dot(he, wd[0], preferred_element_type=F32)
            y = part if y is None else y + part

        @pl.when(j >= 2)
        def _():
            scatter_wait(j - 2, yslot)

        for jt in range(ROW_SUB):
            chunk = y[:, jt * LANES:(jt + 1) * LANES]
            ybuf[yslot, pl.ds(jt, MOE_TILE, stride=ROW_SUB), :] = chunk

        @pl.when(j + GATHER_BUFS - 1 < total)
        def _():
            issue_step(True)

        @pl.when(j + GATHER_BUFS - 1 >= total)
        def _():
            issue_step(False)

        @pl.when(j == total - 1)
        def _():
            @pl.when(j >= 1)
            def _():
                scatter_wait(j - 1, 1 - yslot)

            scatter_wait(j, yslot)


def _moe(hext, pos, ea, eb, valid_rows, total, router_wt, layer, w_gate, w_up, w_down):
    n = hext.shape[0] // ROW_SUB
    max_tiles = ea.shape[0]

    def wspec(shape, which):
        if which == 0:
            return pl.BlockSpec((None, 1) + shape,
                                lambda j, inv_, ea_, eb_, nv_, t_: (layer, ea_[j], 0, 0))
        return pl.BlockSpec((None, 1) + shape, lambda j, inv_, ea_, eb_, nv_, t_: (layer, eb_[j], 0, 0))

    up = (D_MODEL, D_EXPERT)
    down = (D_EXPERT, D_MODEL)
    grid_spec = pltpu.PrefetchScalarGridSpec(
        num_scalar_prefetch=5,
        grid=(max_tiles,),
        in_specs=[pl.BlockSpec(memory_space=pl.ANY),
                  pl.BlockSpec((N_EXPERTS, D_MODEL), lambda j, inv_, ea_, eb_, nv_, t_: (0, 0)),
                  wspec(up, 0), wspec(up, 0), wspec(down, 0),
                  wspec(up, 1), wspec(up, 1), wspec(down, 1)],
        out_specs=pl.BlockSpec(memory_space=pl.ANY),
        scratch_shapes=[pltpu.VMEM((GATHER_BUFS, MOE_TILE * ROW_SUB, LANES), F32),
                        pltpu.VMEM((2, MOE_TILE * ROW_SUB, LANES), F32),
                        pltpu.SMEM((max_tiles * MOE_TILE,), jnp.int32),
                        pltpu.SemaphoreType.DMA((GATHER_BUFS,)),
                        pltpu.SemaphoreType.DMA((2,))],
    )
    return pl.pallas_call(
        _moe_kernel,
        grid_spec=grid_spec,
        out_shape=jax.ShapeDtypeStruct((n * ROW_SUB, LANES), F32),
        compiler_params=_cparams(("arbitrary",)), name="moe_experts",
    )(pos, ea, eb, valid_rows, total, hext, router_wt, w_gate, w_up, w_down, w_gate, w_up, w_down)


def _final_kernel(x_ref, y_ref, g_ref, o_ref):
    o_ref[...] = x_ref[...] + g_ref[0] * _token_rows(y_ref)


def _final(x, y, gate, seq):
    n = x.shape[0]
    tpb = seq // ROW_TILE
    row = pl.BlockSpec((ROW_TILE, D_MODEL), lambda i: (i, 0))
    return pl.pallas_call(
        _final_kernel, grid=(n // ROW_TILE,),
        in_specs=[row, pl.BlockSpec((ROW_TILE * ROW_SUB, LANES), lambda i: (i, 0)),
                  pl.BlockSpec((1, 1, D_MODEL), lambda i: (i // tpb, 0, 0))],
        out_specs=row, out_shape=jax.ShapeDtypeStruct((n, D_MODEL), F32),
        compiler_params=_cparams(("arbitrary",)), name="final_residual",
    )(x, y, gate)


def kernel(x, c, ada_w, ada_b, norm1_g, norm2_g, even_w_in, even_w_out, att_q_norm_g, att_k_norm_g,
           att_rel_bias, odd_w_in, odd_w_out, router_w, router_b, exp_w_gate, exp_w_up, exp_w_down):
    bsz, seq, d = x.shape
    n = bsz * seq
    mod = _modulation(c, ada_w, ada_b)
    mods = [[mod[l, :, k * d:(k + 1) * d].reshape(bsz, 1, d) for k in range(6)] for l in range(2)]
    xf = x.reshape(n, d)
    w_gate, w_up, w_down = (w.astype(BF16) for w in (exp_w_gate, exp_w_up, exp_w_down))

    def moe_layer(layer, mix_a, mix_b, col_a, col_b, w_out, xin):
        sh1, sc1, g1, sh2, sc2, g2 = mods[layer]
        x1, hext, route, counts = _out_route(mix_a, mix_b, col_a, col_b, w_out.astype(BF16), xin, g1,
                                             norm2_g[layer], sc2, sh2, router_w, router_b, seq)
        pos, ea, eb, valid_rows, total = _routing_plan(route, counts, n)
        y = _moe(hext, pos, ea, eb, valid_rows, total, router_w.T.astype(F32), layer,
                 w_gate, w_up, w_down)
        return x1, y, g2

    sh1, sc1, _, _, _, _ = mods[0]
    qk_gain = jnp.stack([att_q_norm_g[0], att_k_norm_g[0]])
    proj0 = _project(xf, norm1_g[0], sc1, sh1, even_w_in[0].astype(BF16), seq, qk_gain=qk_gain)
    ret = _retention(proj0, bsz, seq)
    att = _chunk_attention(proj0, att_rel_bias[0], bsz, seq)
    x1, y0, g2_0 = moe_layer(0, ret, att, 0, 0, even_w_out[0], xf)

    sh1, sc1, _, _, _, _ = mods[1]
    x2, proj1 = _project(x1, norm1_g[1], sc1, sh1, odd_w_in[0].astype(BF16), seq, y=y0, gate=g2_0)
    sbo = _stick_breaking(proj1, bsz, seq)
    x3, y1, g2_1 = moe_layer(1, sbo, sbo, 0, 1, odd_w_out[0], x2)
    return _final(x3, y1, g2_1, seq).reshape(bsz, seq, d)
```

```python
import functools

import numpy as np
import jax
import jax.numpy as jnp
from jax import lax
from jax.experimental import pallas as pl
from jax.experimental.pallas import tpu as pltpu

F32 = jnp.float32
BF16 = jnp.bfloat16

D_MODEL = 1024
CHUNK = 64
EPS = 1e-6
ROPE_BASE = 10000.0
RET_HEADS = 4
RET_DIM = 128
ATT_HEADS = 8
ATT_DIM = 64
ATT_LEFT_CHUNKS = 8
REL_CLIP = 128
SB_HEADS = 16
SB_DIM = 64
N_EXPERTS = 16
N_GROUPS = 4
D_EXPERT = 512
HALF = 512

VMEM_LIMIT = 48 * 1024 * 1024
LANES = 128

ROW_TILE = 512
RET_TILE = 256
ATT_QB = 256
ATT_WIN = ATT_LEFT_CHUNKS * CHUNK + ATT_QB
ATT_STEP_BLOCKS = 4
SB_BLK = 128
SB_ROWS = 64
SB_UNROLL = 8
SB_WIN = 3 * SB_BLK
SB_EXIT = -110.0

PAIR_A = (0, 0, 0, 1, 1, 2)
PAIR_B = (1, 2, 3, 2, 3, 3)
N_CLASS = N_GROUPS * len(PAIR_A)
CLASS_ROWS = 32
MOE_TILE = 256
ROW_SUB = D_MODEL // LANES
GATHER_BUFS = 3


def _cparams(sem):
    return pltpu.CompilerParams(dimension_semantics=sem, vmem_limit_bytes=VMEM_LIMIT)


def _mod_kernel(c_ref, w_ref, b_ref, o_ref):
    c = c_ref[...]
    ca = c * jax.nn.sigmoid(c)
    o_ref[0] = jnp.dot(ca.astype(BF16), w_ref[0].astype(BF16), preferred_element_type=F32) + b_ref[0]


def _modulation(c, ada_w, ada_b):
    depth, _, width = ada_w.shape
    bsz = c.shape[0]
    rows = 8
    cp = jnp.zeros((rows, D_MODEL), F32).at[:bsz].set(c)
    tn = 1536
    out = pl.pallas_call(
        _mod_kernel,
        grid=(depth, width // tn),
        in_specs=[pl.BlockSpec((rows, D_MODEL), lambda l, j: (0, 0)),
                  pl.BlockSpec((1, D_MODEL, tn), lambda l, j: (l, 0, j)),
                  pl.BlockSpec((1, 1, tn), lambda l, j: (l, 0, j))],
        out_specs=pl.BlockSpec((1, rows, tn), lambda l, j: (l, 0, j)),
        out_shape=jax.ShapeDtypeStruct((depth, rows, width), F32),
        compiler_params=_cparams(("arbitrary", "arbitrary")),
        name="adaln_mod",
    )(cp, ada_w, ada_b.reshape(depth, 1, width))
    return out[:, :bsz]


def _norm_mod(x, g, sc, sh):
    ms = jnp.mean(x * x, axis=-1, keepdims=True)
    return (x * lax.rsqrt(ms + EPS) * g) * (1.0 + sc) + sh


def _project_chunks(h, w_ref, o_ref, qk_ref, bd_ref, q_scale):
    for ci, n0 in enumerate(range(0, o_ref.shape[1], HALF)):
        p = jnp.dot(h, w_ref[:, n0:n0 + HALF], preferred_element_type=F32)
        if qk_ref is not None and ci in (4, 5):
            ms = jnp.dot((p * p).astype(BF16), bd_ref[...], preferred_element_type=F32)
            p = p * lax.rsqrt(ms + EPS) * qk_ref[ci - 4:ci - 3, :]
            if ci == 4:
                p = p * q_scale
        elif qk_ref is None and n0 < SB_HEADS * SB_DIM:
            p = p * q_scale
        o_ref[:, n0:n0 + HALF] = p.astype(BF16)


def _proj_kernel(x_ref, g_ref, sc_ref, sh_ref, w_ref, qk_ref, bd_ref, o_ref):
    h = _norm_mod(x_ref[...], g_ref[...], sc_ref[0], sh_ref[0]).astype(BF16)
    _project_chunks(h, w_ref, o_ref, qk_ref, bd_ref, ATT_DIM ** -0.5)


def _token_rows(y_ref):
    rows = y_ref.shape[0] // ROW_SUB
    return jnp.concatenate([y_ref[pl.ds(jt, rows, stride=ROW_SUB), :] for jt in range(ROW_SUB)],
                           axis=1)


def _proj_res_kernel(x_ref, y_ref, gt_ref, g_ref, sc_ref, sh_ref, w_ref, xo_ref, o_ref):
    x = x_ref[...] + gt_ref[0] * _token_rows(y_ref)
    xo_ref[...] = x
    h = _norm_mod(x, g_ref[...], sc_ref[0], sh_ref[0]).astype(BF16)
    _project_chunks(h, w_ref, o_ref, None, None, SB_DIM ** -0.5)


def _project(x, gain, sc, sh, w, seq, qk_gain=None, y=None, gate=None):
    n = x.shape[0]
    nout = w.shape[1]
    tpb = seq // ROW_TILE
    row = pl.BlockSpec((ROW_TILE, D_MODEL), lambda i: (i, 0))
    per_b = pl.BlockSpec((1, 1, D_MODEL), lambda i: (i // tpb, 0, 0))
    gspec = pl.BlockSpec((1, D_MODEL), lambda i: (0, 0))
    wspec = pl.BlockSpec((D_MODEL, nout), lambda i: (0, 0))
    ospec = pl.BlockSpec((ROW_TILE, nout), lambda i: (i, 0))
    oshape = jax.ShapeDtypeStruct((n, nout), BF16)
    if y is None:
        head = np.arange(HALF) // ATT_DIM
        bd = jnp.asarray((head[:, None] == head[None, :]).astype(np.float32) / ATT_DIM, BF16)
        qk = jnp.tile(qk_gain.astype(F32), (1, ATT_HEADS))
        return pl.pallas_call(
            _proj_kernel, grid=(n // ROW_TILE,),
            in_specs=[row, gspec, per_b, per_b, wspec,
                      pl.BlockSpec((2, HALF), lambda i: (0, 0)),
                      pl.BlockSpec((HALF, HALF), lambda i: (0, 0))],
            out_specs=ospec, out_shape=oshape,
            compiler_params=_cparams(("arbitrary",)), name="norm_proj",
        )(x, gain.reshape(1, D_MODEL), sc, sh, w, qk, bd)
    return pl.pallas_call(
        _proj_res_kernel, grid=(n // ROW_TILE,),
        in_specs=[row, pl.BlockSpec((ROW_TILE * ROW_SUB, LANES), lambda i: (i, 0)), per_b, gspec,
                  per_b, per_b, wspec],
        out_specs=[row, ospec],
        out_shape=[jax.ShapeDtypeStruct((n, D_MODEL), F32), oshape],
        compiler_params=_cparams(("arbitrary",)), name="res_norm_proj",
    )(x, y, gate, gain.reshape(1, D_MODEL), sc, sh, w)


def _ret_tables(seq):
    inv = ROPE_BASE ** (-np.arange(0, RET_DIM, 2, dtype=np.float64) / RET_DIM)
    ang = np.arange(seq, dtype=np.float64)[:, None] * inv[None, :]
    cos = np.concatenate([np.cos(ang), np.cos(ang)], axis=1)
    sin = np.concatenate([-np.sin(ang), np.sin(ang)], axis=1)
    log_g = np.log(1.0 - 2.0 ** (-5.0 - np.arange(RET_HEADS, dtype=np.float64)))
    idx = np.arange(RET_TILE)
    same_or_earlier = (idx[None, :] // CHUNK) <= (idx[:, None] // CHUNK)
    dec = np.exp(log_g[:, None, None] * np.abs(idx[:, None] - idx[None, :])) * same_or_earlier
    loc = idx.astype(np.float64)
    qdec = np.exp(log_g[:, None] * (loc[None, :] + 1.0))
    kdec = np.exp(log_g[:, None] * (RET_TILE - 1.0 - loc[None, :]))
    tdec = np.exp(log_g * RET_TILE)
    qk = np.stack([qdec, kdec], axis=1)[..., None] * np.ones((1, 1, 1, RET_DIM))
    return (jnp.asarray(cos, F32), jnp.asarray(sin, F32), jnp.asarray(dec, F32),
            jnp.asarray(qk, F32), tuple(float(t) for t in tdec))


def _ret_kernel(tdec, q_ref, k_ref, v_ref, g_ref, cos_ref, sin_ref, dec_ref, qk_ref, o_ref, st_ref):
    @pl.when(pl.program_id(1) == 0)
    def _():
        st_ref[...] = jnp.zeros_like(st_ref)

    cos = cos_ref[...]
    sin = sin_ref[...]
    half = RET_DIM // 2
    for h in range(RET_HEADS):
        cols = slice(h * RET_DIM, (h + 1) * RET_DIM)
        q = q_ref[:, cols].astype(F32)
        k = k_ref[:, cols].astype(F32)
        q = q * cos + pltpu.roll(q, half, 1) * sin
        k = (k * cos + pltpu.roll(k, half, 1) * sin) * (RET_DIM ** -0.5)
        v = v_ref[:, cols]
        qb = q.astype(BF16)
        kb = k.astype(BF16)
        s = lax.dot_general(qb, kb, (((1,), (1,)), ((), ())), preferred_element_type=F32)
        s = s * dec_ref[h]
        o = jnp.dot(s.astype(BF16), v, preferred_element_type=F32)
        st = st_ref[h]
        o = o + jnp.dot((q * qk_ref[h, 0]).astype(BF16), st.astype(BF16), preferred_element_type=F32)
        kd = (k * qk_ref[h, 1]).astype(BF16)
        st_ref[h] = st * tdec[h] + lax.dot_general(kd, v, (((0,), (0,)), ((), ())),
                                                   preferred_element_type=F32)
        o = o * lax.rsqrt(jnp.mean(o * o, axis=-1, keepdims=True) + EPS)
        g = g_ref[:, cols].astype(F32)
        o_ref[:, cols] = (o * (g * jax.nn.sigmoid(g))).astype(BF16)


def _retention(proj, bsz, seq):
    n = proj.shape[0]
    nt = seq // RET_TILE
    cos, sin, dec, qk, tdec = _ret_tables(seq)

    def col(c):
        return pl.BlockSpec((RET_TILE, HALF), lambda b, i: (b * nt + i, c))

    pos = pl.BlockSpec((RET_TILE, RET_DIM), lambda b, i: (i, 0))
    return pl.pallas_call(
        functools.partial(_ret_kernel, tdec),
        grid=(bsz, nt),
        in_specs=[col(0), col(1), col(2), col(3), pos, pos,
                  pl.BlockSpec((RET_HEADS, RET_TILE, RET_TILE), lambda b, i: (0, 0, 0)),
                  pl.BlockSpec((RET_HEADS, 2, RET_TILE, RET_DIM), lambda b, i: (0, 0, 0, 0))],
        out_specs=pl.BlockSpec((RET_TILE, HALF), lambda b, i: (b * nt + i, 0)),
        out_shape=jax.ShapeDtypeStruct((n, HALF), BF16),
        scratch_shapes=[pltpu.VMEM((RET_HEADS, RET_DIM, RET_DIM), F32)],
        compiler_params=_cparams(("arbitrary", "arbitrary")), name="retention",
    )(proj, proj, proj, proj, cos, sin, dec, qk)


def _att_bias(rel_bias):
    span = ATT_WIN + ATT_QB
    m = np.concatenate([np.arange(0, ATT_WIN), np.zeros(1, np.int64), np.arange(-(ATT_QB - 1), 0)])
    idx = np.clip((ATT_WIN - ATT_QB) - m, -(CHUNK - 1), REL_CLIP) + (CHUNK - 1)
    vec = rel_bias[:, idx].astype(F32)
    heads = rel_bias.shape[0]
    toep = jnp.tile(vec, (1, ATT_QB))[:, :ATT_QB * (span - 1)].reshape(heads, ATT_QB, span - 1)
    qc = np.arange(ATT_QB)[:, None] // CHUNK
    kc = np.arange(ATT_WIN)[None, :] // CHUNK - ATT_LEFT_CHUNKS
    ok = (kc <= qc) & (kc >= qc - ATT_LEFT_CHUNKS)
    return jnp.where(jnp.asarray(ok)[None], toep[:, :, :ATT_WIN], -jnp.inf)


def _fill_padded(pad_ref, src_ref, front):
    pad_ref[:front, :] = jnp.zeros((front, pad_ref.shape[1]), pad_ref.dtype)
    pad_ref[front:, :] = src_ref[...]


def _att_kernel(q_ref, k_ref, v_ref, bias_ref, o_ref, kpad, vpad):
    i = pl.program_id(1)
    front = ATT_WIN - ATT_QB

    @pl.when(i == 0)
    def _():
        _fill_padded(kpad, k_ref, front)
        _fill_padded(vpad, v_ref, front)

    col = lax.broadcasted_iota(jnp.int32, (1, ATT_WIN), 1)
    first = lax.broadcasted_iota(jnp.int32, (ATT_QB, LANES), 1) < ATT_DIM
    nt = (((1,), (1,)), ((), ()))
    for sub in range(ATT_STEP_BLOCKS):
        rows = slice(sub * ATT_QB, (sub + 1) * ATT_QB)
        start = pl.multiple_of((i * ATT_STEP_BLOCKS + sub) * ATT_QB, ATT_QB)
        pen = jnp.where(col >= front - start, 0.0, -jnp.inf)
        for p in range(ATT_HEADS * ATT_DIM // LANES):
            cols = slice(p * LANES, (p + 1) * LANES)
            q = q_ref[rows, cols]
            kw = kpad[pl.ds(start, ATT_WIN), cols]
            vw = vpad[pl.ds(start, ATT_WIN), cols]
            zero = jnp.zeros_like(q)
            qq = jnp.concatenate([jnp.where(first, q, zero), jnp.where(first, zero, q)], axis=0)
            bias = jnp.concatenate([bias_ref[2 * p], bias_ref[2 * p + 1]], axis=0)
            s = lax.dot_general(qq, kw, nt, preferred_element_type=F32) + bias + pen
            e = jnp.exp(s - jnp.max(s, axis=-1, keepdims=True))
            den = jnp.sum(e, axis=-1, keepdims=True)
            o = jnp.dot(e.astype(BF16), vw, preferred_element_type=F32) / den
            o_ref[rows, cols] = jnp.where(first, o[:ATT_QB], o[ATT_QB:]).astype(BF16)


def _chunk_attention(proj, rel_bias, bsz, seq):
    n = proj.shape[0]
    rows = ATT_STEP_BLOCKS * ATT_QB
    nq = seq // rows
    front = ATT_WIN - ATT_QB
    return pl.pallas_call(
        _att_kernel,
        grid=(bsz, nq),
        in_specs=[pl.BlockSpec((rows, HALF), lambda b, i: (b * nq + i, 4)),
                  pl.BlockSpec((seq, HALF), lambda b, i: (b, 5)),
                  pl.BlockSpec((seq, HALF), lambda b, i: (b, 6)),
                  pl.BlockSpec((ATT_HEADS, ATT_QB, ATT_WIN), lambda b, i: (0, 0, 0))],
        out_specs=pl.BlockSpec((rows, HALF), lambda b, i: (b * nq + i, 0)),
        out_shape=jax.ShapeDtypeStruct((n, HALF), BF16),
        scratch_shapes=[pltpu.VMEM((front + seq, HALF), BF16), pltpu.VMEM((front + seq, HALF), BF16)],
        compiler_params=_cparams(("arbitrary", "arbitrary")), name="chunk_attention",
    )(proj, proj, proj, _att_bias(rel_bias))


def _sb_logs(z):
    log_beta = jnp.minimum(z, 0.0) - jnp.log(1.0 + jnp.exp(-jnp.abs(z)))
    return log_beta, log_beta - z


def _sb_kernel(q_ref, k_ref, v_ref, tri_ref, o_ref, kpad, vpad, run_ref, acc_ref):
    front = SB_WIN - SB_ROWS
    nblk = SB_WIN // SB_BLK
    pairs = HALF // LANES
    _fill_padded(kpad, k_ref, front)
    _fill_padded(vpad, v_ref, front)
    row = lax.broadcasted_iota(jnp.int32, (SB_ROWS, SB_WIN), 0)
    col = lax.broadcasted_iota(jnp.int32, (SB_ROWS, SB_WIN), 1)
    causal_col = jnp.where(col - row < front, col, -1)
    first = lax.broadcasted_iota(jnp.int32, (SB_ROWS, LANES), 1) < SB_DIM
    nt = (((1,), (1,)), ((), ()))

    def first_window(iq, u):
        start = pl.multiple_of(iq * SB_ROWS, SB_ROWS)
        mask = causal_col >= jnp.maximum(front - start, 0)
        tri = tri_ref[...]

        def heads_of(p):
            q = q_ref[pl.ds(start, SB_ROWS), p * LANES:(p + 1) * LANES]
            zero = jnp.zeros_like(q)
            return jnp.where(first, q, zero), jnp.where(first, zero, q)

        alive = None
        mask2 = jnp.concatenate([mask, mask], axis=0)
        lbs, l1s, vws = [], [], []
        for p in range(pairs):
            cols = slice(p * LANES, (p + 1) * LANES)
            kw = kpad[pl.ds(start, SB_WIN), cols]
            vws.append(vpad[pl.ds(start, SB_WIN), cols])
            qq = jnp.concatenate(heads_of(p), axis=0)
            z = lax.dot_general(qq, kw, nt, preferred_element_type=F32)
            log_beta, log_1mb = _sb_logs(z)
            lbs.append(log_beta)
            l1 = jnp.where(mask2, log_1mb, 0.0).astype(BF16)
            l1s += [l1[hh * SB_ROWS:(hh + 1) * SB_ROWS, c * SB_BLK:(c + 1) * SB_BLK]
                    for hh in range(2) for c in range(nblk)]
        cs = jnp.dot(jnp.concatenate(l1s, axis=0), tri, preferred_element_type=F32)
        for p in range(pairs):
            rows = []
            for hh in range(2):
                h = 2 * p + hh
                run = jnp.zeros((SB_ROWS, SB_BLK), F32)
                accs = [None] * nblk
                for c in reversed(range(nblk)):
                    blk = cs[(h * nblk + c) * SB_ROWS:(h * nblk + c + 1) * SB_ROWS]
                    accs[c] = blk[:, :SB_BLK] + run
                    run = run + blk[:, SB_BLK:]
                rows.append(jnp.concatenate(accs, axis=1))
                run_ref[u, h] = run
                alive = run if alive is None else jnp.maximum(alive, run)
            a = jnp.where(mask2, jnp.exp(lbs[p] + jnp.concatenate(rows, axis=0)), 0.0)
            o = jnp.dot(a.astype(BF16), vws[p], preferred_element_type=F32)
            acc_ref[u, 2 * p] = o[:SB_ROWS]
            acc_ref[u, 2 * p + 1] = o[SB_ROWS:]
        return jnp.max(alive)

    def older_keys(iq, u, alive0):
        start = pl.multiple_of(iq * SB_ROWS, SB_ROWS)
        tri = tri_ref[...]

        def heads_of(p):
            q = q_ref[pl.ds(start, SB_ROWS), p * LANES:(p + 1) * LANES]
            zero = jnp.zeros_like(q)
            return jnp.where(first, q, zero), jnp.where(first, zero, q)

        def cond(c):
            return jnp.logical_and(c[0] >= 0, c[1] > SB_EXIT)

        def body(c):
            j = c[0]
            kstart = pl.multiple_of(front + j * SB_ROWS, SB_ROWS)
            alive = None
            for p in range(pairs):
                cols = slice(p * LANES, (p + 1) * LANES)
                kb = kpad[pl.ds(kstart, SB_ROWS), cols]
                vb = vpad[pl.ds(kstart, SB_ROWS), cols]
                for hh, qm in enumerate(heads_of(p)):
                    z = lax.dot_general(qm, kb, nt, preferred_element_type=F32)
                    log_beta, log_1mb = _sb_logs(z)
                    cs = jnp.dot(log_1mb.astype(BF16), tri[:SB_ROWS], preferred_element_type=F32)
                    run = run_ref[u, 2 * p + hh]
                    a = jnp.exp(log_beta + run[:, :SB_ROWS] + cs[:, :SB_ROWS])
                    acc_ref[u, 2 * p + hh] += jnp.dot(a.astype(BF16), vb, preferred_element_type=F32)
                    run = run + cs[:, SB_BLK:]
                    run_ref[u, 2 * p + hh] = run
                    alive = run if alive is None else jnp.maximum(alive, run)
            return j - 1, jnp.max(alive)

        lax.while_loop(cond, body, (iq - SB_WIN // SB_ROWS, alive0))
        for p in range(pairs):
            o_ref[pl.ds(start, SB_ROWS), p * LANES:(p + 1) * LANES] = jnp.where(
                first, acc_ref[u, 2 * p], acc_ref[u, 2 * p + 1]).astype(BF16)

    def query_blocks(it, carry):
        alive = [first_window(it * SB_UNROLL + u, u) for u in range(SB_UNROLL)]
        for u in range(SB_UNROLL):
            older_keys(it * SB_UNROLL + u, u, alive[u])
        return carry

    lax.fori_loop(0, q_ref.shape[0] // (SB_ROWS * SB_UNROLL), query_blocks, 0)


def _stick_breaking(proj, bsz, seq):
    n = proj.shape[0]
    groups = SB_HEADS * SB_DIM // HALF
    front = SB_WIN - SB_ROWS
    j = np.arange(SB_BLK)
    tri = np.concatenate([(j[:, None] > j[None, :]).astype(np.float32),
                          np.ones((SB_BLK, SB_BLK), np.float32)], axis=1)
    return pl.pallas_call(
        _sb_kernel,
        grid=(bsz, groups),
        in_specs=[pl.BlockSpec((seq, HALF), lambda b, g: (b, g)),
                  pl.BlockSpec((seq, HALF), lambda b, g: (b, groups + g)),
                  pl.BlockSpec((seq, HALF), lambda b, g: (b, 2 * groups + g)),
                  pl.BlockSpec((SB_BLK, 2 * SB_BLK), lambda b, g: (0, 0))],
        out_specs=pl.BlockSpec((seq, HALF), lambda b, g: (b, g)),
        out_shape=jax.ShapeDtypeStruct((n, SB_HEADS * SB_DIM), BF16),
        scratch_shapes=[pltpu.VMEM((front + seq, HALF), BF16), pltpu.VMEM((front + seq, HALF), BF16),
                        pltpu.VMEM((SB_UNROLL, HALF // SB_DIM, SB_ROWS, SB_BLK), F32),
                        pltpu.VMEM((SB_UNROLL, HALF // SB_DIM, SB_ROWS, LANES), F32)],
        compiler_params=_cparams(("arbitrary", "arbitrary")), name="stick_breaking",
    )(proj, proj, proj, jnp.asarray(tri, BF16))


def _out_route_kernel(ma_ref, mb_ref, wa_ref, wb_ref, x_ref, g1_ref, g_ref, sc_ref, sh_ref,
                      rw_ref, rb_ref, tri_ref, xo_ref, he_ref, rt_ref, cnt_ref, carry_ref):
    @pl.when(pl.program_id(0) == 0)
    def _():
        carry_ref[...] = jnp.zeros_like(carry_ref)

    tm = x_ref.shape[0]
    mix = (jnp.dot(ma_ref[...], wa_ref[...], preferred_element_type=F32)
           + jnp.dot(mb_ref[...], wb_ref[...], preferred_element_type=F32))
    x = x_ref[...] + g1_ref[0] * mix
    xo_ref[...] = x
    h = _norm_mod(x, g_ref[...], sc_ref[0], sh_ref[0])
    for jt in range(ROW_SUB):
        he_ref[pl.ds(jt, tm, stride=ROW_SUB), :] = h[:, jt * LANES:(jt + 1) * LANES]

    nt = (((1,), (1,)), ((), ()))
    h_hi = h.astype(BF16)
    h_lo = (h - h_hi.astype(F32)).astype(BF16)
    w = rw_ref[...]
    w_hi = w.astype(BF16)
    w_lo = (w - w_hi.astype(F32)).astype(BF16)
    logits = (lax.dot_general(w_hi, h_hi, nt, preferred_element_type=F32)
              + lax.dot_general(w_hi, h_lo, nt, preferred_element_type=F32)
              + lax.dot_general(w_lo, h_hi, nt, preferred_element_type=F32))
    sel = jax.nn.sigmoid(logits) + rb_ref[...]

    best = None
    for g in range(N_GROUPS):
        for p in range(len(PAIR_A)):
            ea, eb = 4 * g + PAIR_A[p], 4 * g + PAIR_B[p]
            val = sel[ea:ea + 1, :] + sel[eb:eb + 1, :]
            cid = jnp.full((1, tm), float(len(PAIR_A) * g + p), F32)
            cand = (val, cid)
            if best is None:
                best = cand
            else:
                take = val > best[0]
                best = tuple(jnp.where(take, c_, b_) for c_, b_ in zip(cand, best))
    cls = best[1]

    crow = lax.broadcasted_iota(jnp.int32, (CLASS_ROWS, tm), 0).astype(F32)
    onehot = jnp.where(crow == cls, 1.0, 0.0)
    before = jnp.dot(onehot.astype(BF16), tri_ref[...], preferred_element_type=F32)
    carry = carry_ref[...]
    rank = jnp.sum(onehot * (before + carry[:, :1]), axis=0, keepdims=True)
    carry = carry + jnp.sum(onehot, axis=1, keepdims=True)
    carry_ref[...] = carry
    cnt_ref[0] = carry

    srow = lax.broadcasted_iota(jnp.int32, (8, tm), 0)
    rt_ref[0] = jnp.where(srow == 0, cls, jnp.where(srow == 1, rank, 0.0))


def _out_route(mix_a, mix_b, col_a, col_b, w_out, x, g1, gain, sc, sh, router_w, router_b, seq):
    n = x.shape[0]
    tm = ROW_TILE
    nt = n // tm
    tpb = seq // tm
    t = np.arange(tm)
    tri = jnp.asarray((t[:, None] < t[None, :]).astype(np.float32), BF16)
    row = pl.BlockSpec((tm, D_MODEL), lambda i: (i, 0))
    per_b = pl.BlockSpec((1, 1, D_MODEL), lambda i: (i // tpb, 0, 0))
    rb = jnp.broadcast_to(router_b.astype(F32)[:, None], (N_EXPERTS, tm))
    return pl.pallas_call(
        _out_route_kernel,
        grid=(nt,),
        in_specs=[pl.BlockSpec((tm, HALF), lambda i: (i, col_a)),
                  pl.BlockSpec((tm, HALF), lambda i: (i, col_b)),
                  pl.BlockSpec((HALF, D_MODEL), lambda i: (0, 0)),
                  pl.BlockSpec((HALF, D_MODEL), lambda i: (1, 0)),
                  row, per_b,
                  pl.BlockSpec((1, D_MODEL), lambda i: (0, 0)), per_b, per_b,
                  pl.BlockSpec((N_EXPERTS, D_MODEL), lambda i: (0, 0)),
                  pl.BlockSpec((N_EXPERTS, tm), lambda i: (0, 0)),
                  pl.BlockSpec((tm, tm), lambda i: (0, 0))],
        out_specs=[row,
                   pl.BlockSpec((tm * ROW_SUB, LANES), lambda i: (i, 0)),
                   pl.BlockSpec((1, 8, tm), lambda i: (i, 0, 0)),
                   pl.BlockSpec((1, CLASS_ROWS, LANES), lambda i: (i, 0, 0))],
        out_shape=[jax.ShapeDtypeStruct((n, D_MODEL), F32),
                   jax.ShapeDtypeStruct((n * ROW_SUB, LANES), F32),
                   jax.ShapeDtypeStruct((nt, 8, tm), F32),
                   jax.ShapeDtypeStruct((nt, CLASS_ROWS, LANES), F32)],
        scratch_shapes=[pltpu.VMEM((CLASS_ROWS, LANES), F32)],
        compiler_params=_cparams(("arbitrary",)), name="out_proj_route",
    )(mix_a, mix_b, w_out, w_out, x, g1, gain.reshape(1, D_MODEL), sc, sh,
      router_w.T.astype(F32), rb, tri)


def _fill_slots(pos_ref, nv_ref, inv_ref):
    batch = 16

    def fill_tile(j, c):
        def fill(g, c2):
            for u in range(batch):
                inv_ref[j * MOE_TILE + g * batch + u] = 0
            return c2

        return lax.fori_loop(nv_ref[j] // batch, MOE_TILE // batch, fill, c)

    lax.fori_loop(0, nv_ref.shape[0], fill_tile, 0)

    def place(g, c):
        slots = [pos_ref[g * batch + u] for u in range(batch)]
        for u in range(batch):
            inv_ref[slots[u]] = g * batch + u
        return c

    lax.fori_loop(0, pos_ref.shape[0] // batch, place, 0)


def _routing_plan(route, counts, n):
    nt, _, tm = route.shape
    cls = route[:, 0, :].reshape(n).astype(jnp.int32)
    rank = route[:, 1, :].reshape(n).astype(jnp.int32)
    cnt = counts[-1, :N_CLASS, 0].astype(jnp.int32)
    tiles_c = (cnt + MOE_TILE - 1) // MOE_TILE
    ends = jnp.cumsum(tiles_c)
    starts = ends - tiles_c
    pos = starts[cls] * MOE_TILE + rank
    max_tiles = n // MOE_TILE + N_CLASS
    total = ends[-1]
    j = jnp.arange(max_tiles, dtype=jnp.int32)
    jj = jnp.minimum(j, total - 1)
    tcls = jnp.sum((ends[None, :] <= jj[:, None]).astype(jnp.int32), axis=1)
    grp = tcls // len(PAIR_A)
    pair = tcls % len(PAIR_A)
    ea = 4 * grp + jnp.asarray(PAIR_A, jnp.int32)[pair]
    eb = 4 * grp + jnp.asarray(PAIR_B, jnp.int32)[pair]
    valid_rows = jnp.clip(cnt[tcls] - (jj - starts[tcls]) * MOE_TILE, 0, MOE_TILE)
    valid_rows = jnp.where(j < total, valid_rows, 0).astype(jnp.int32)
    return pos, ea, eb, valid_rows, total.reshape(1)


def _moe_kernel(pos_ref, ea_ref, eb_ref, nv_ref, tot_ref, h_hbm, rw_ref, wg_a, wu_a, wd_a,
                wg_b, wu_b, wd_b, y_hbm, gbuf, ybuf, inv_ref, gsem, ssem):
    j = pl.program_id(0)
    total = tot_ref[0]
    gslot = j % GATHER_BUFS
    yslot = j % 2
    unroll = 8

    def gather_row(tile, s, r, prio):
        src = pl.multiple_of(inv_ref[tile * MOE_TILE + r] * ROW_SUB, ROW_SUB)
        dst = pl.multiple_of(r * ROW_SUB, ROW_SUB)
        pltpu.make_async_copy(h_hbm.at[pl.ds(src, ROW_SUB), :], gbuf.at[s, pl.ds(dst, ROW_SUB), :],
                              gsem.at[s]).start(priority=prio)

    def scatter_row(tile, s, r, prio):
        src = pl.multiple_of(r * ROW_SUB, ROW_SUB)
        dst = pl.multiple_of(inv_ref[tile * MOE_TILE + r] * ROW_SUB, ROW_SUB)
        pltpu.make_async_copy(ybuf.at[s, pl.ds(src, ROW_SUB), :], y_hbm.at[pl.ds(dst, ROW_SUB), :],
                              ssem.at[s]).start(priority=prio)

    def gathered_rows(tile):
        return (nv_ref[tile] + unroll - 1) // unroll * unroll

    def gather_rows(tile, s):
        def group(g, c):
            for u in range(unroll):
                gather_row(tile, s, g * unroll + u, u % 2)
            return c

        lax.fori_loop(0, gathered_rows(tile) // unroll, group, 0)

    def wait_rows(count, make_copy):
        for bit in range(MOE_TILE.bit_length()):
            @pl.when((count >> bit) & 1 == 1)
            def _():
                make_copy((1 << bit) * ROW_SUB).wait()

    def gather_wait(tile, s):
        wait_rows(gathered_rows(tile), lambda size: pltpu.make_async_copy(
            h_hbm.at[pl.ds(0, size), :], gbuf.at[s, pl.ds(0, size), :], gsem.at[s]))

    def scatter_rows(tile, s):
        nv = nv_ref[tile]
        groups = nv // unroll

        def group(g, c):
            for u in range(unroll):
                scatter_row(tile, s, g * unroll + u, u % 2)
            return c

        lax.fori_loop(0, groups, group, 0)
        for u in range(unroll):
            @pl.when(groups * unroll + u < nv)
            def _():
                scatter_row(tile, s, groups * unroll + u, u % 2)

    def scatter_wait(tile, s):
        wait_rows(nv_ref[tile], lambda size: pltpu.make_async_copy(
            ybuf.at[s, pl.ds(0, size), :], y_hbm.at[pl.ds(0, size), :], ssem.at[s]))

    @pl.when(j == 0)
    def _():
        _fill_slots(pos_ref, nv_ref, inv_ref)
        gbuf[...] = jnp.zeros_like(gbuf)
        for t in range(GATHER_BUFS - 1):
            @pl.when(t < total)
            def _():
                gather_rows(t, t)

    @pl.when(j < total)
    def _():
        gather_wait(j, gslot)
        x = jnp.concatenate([gbuf[gslot, pl.ds(jt, MOE_TILE, stride=ROW_SUB), :]
                             for jt in range(ROW_SUB)], axis=1)
        xb = x.astype(BF16)
        scores = [jax.nn.sigmoid(jnp.sum(x * rw_ref[pl.ds(e_ref[j], 1), :], axis=-1, keepdims=True))
                  for e_ref in (ea_ref, eb_ref)]
        shares = [sc_ / (scores[0] + scores[1]) for sc_ in scores]
        y = None
        for lane, (wg, wu, wd) in enumerate(((wg_a, wu_a, wd_a), (wg_b, wu_b, wd_b))):
            g = jnp.dot(xb, wg[0], preferred_element_type=F32)
            u = jnp.dot(xb, wu[0], preferred_element_type=F32)
            he = ((g * jax.nn.sigmoid(g)) * u * shares[lane]).astype(BF16)
            part = jnp.dot(he, wd[0], preferred_element_type=F32)
            y = part if y is None else y + part

        @pl.when(j >= 2)
        def _():
            scatter_wait(j - 2, yslot)

        for jt in range(ROW_SUB):
            chunk = y[:, jt * LANES:(jt + 1) * LANES]
            ybuf[yslot, pl.ds(jt, MOE_TILE, stride=ROW_SUB), :] = chunk

        scatter_rows(j, yslot)
        ahead = j + GATHER_BUFS - 1

        @pl.when(ahead < total)
        def _():
            gather_rows(ahead, ahead % GATHER_BUFS)

        @pl.when(j == total - 1)
        def _():
            @pl.when(j >= 1)
            def _():
                scatter_wait(j - 1, 1 - yslot)

            scatter_wait(j, yslot)


def _moe(hext, pos, ea, eb, valid_rows, total, router_wt, layer, w_gate, w_up, w_down):
    n = hext.shape[0] // ROW_SUB
    max_tiles = ea.shape[0]

    def wspec(shape, which):
        if which == 0:
            return pl.BlockSpec((None, 1) + shape,
                                lambda j, inv_, ea_, eb_, nv_, t_: (layer, ea_[j], 0, 0))
        return pl.BlockSpec((None, 1) + shape, lambda j, inv_, ea_, eb_, nv_, t_: (layer, eb_[j], 0, 0))

    up = (D_MODEL, D_EXPERT)
    down = (D_EXPERT, D_MODEL)
    grid_spec = pltpu.PrefetchScalarGridSpec(
        num_scalar_prefetch=5,
        grid=(max_tiles,),
        in_specs=[pl.BlockSpec(memory_space=pl.ANY),
                  pl.BlockSpec((N_EXPERTS, D_MODEL), lambda j, inv_, ea_, eb_, nv_, t_: (0, 0)),
                  wspec(up, 0), wspec(up, 0), wspec(down, 0),
                  wspec(up, 1), wspec(up, 1), wspec(down, 1)],
        out_specs=pl.BlockSpec(memory_space=pl.ANY),
        scratch_shapes=[pltpu.VMEM((GATHER_BUFS, MOE_TILE * ROW_SUB, LANES), F32),
                        pltpu.VMEM((2, MOE_TILE * ROW_SUB, LANES), F32),
                        pltpu.SMEM((max_tiles * MOE_TILE,), jnp.int32),
                        pltpu.SemaphoreType.DMA((GATHER_BUFS,)),
                        pltpu.SemaphoreType.DMA((2,))],
    )
    return pl.pallas_call(
        _moe_kernel,
        grid_spec=grid_spec,
        out_shape=jax.ShapeDtypeStruct((n * ROW_SUB, LANES), F32),
        compiler_params=_cparams(("arbitrary",)), name="moe_experts",
    )(pos, ea, eb, valid_rows, total, hext, router_wt, w_gate, w_up, w_down, w_gate, w_up, w_down)


def _final_kernel(x_ref, y_ref, g_ref, o_ref):
    o_ref[...] = x_ref[...] + g_ref[0] * _token_rows(y_ref)


def _final(x, y, gate, seq):
    n = x.shape[0]
    tpb = seq // ROW_TILE
    row = pl.BlockSpec((ROW_TILE, D_MODEL), lambda i: (i, 0))
    return pl.pallas_call(
        _final_kernel, grid=(n // ROW_TILE,),
        in_specs=[row, pl.BlockSpec((ROW_TILE * ROW_SUB, LANES), lambda i: (i, 0)),
                  pl.BlockSpec((1, 1, D_MODEL), lambda i: (i // tpb, 0, 0))],
        out_specs=row, out_shape=jax.ShapeDtypeStruct((n, D_MODEL), F32),
        compiler_params=_cparams(("arbitrary",)), name="final_residual",
    )(x, y, gate)


def kernel(x, c, ada_w, ada_b, norm1_g, norm2_g, even_w_in, even_w_out, att_q_norm_g, att_k_norm_g,
           att_rel_bias, odd_w_in, odd_w_out, router_w, router_b, exp_w_gate, exp_w_up, exp_w_down):
    bsz, seq, d = x.shape
    n = bsz * seq
    mod = _modulation(c, ada_w, ada_b)
    mods = [[mod[l, :, k * d:(k + 1) * d].reshape(bsz, 1, d) for k in range(6)] for l in range(2)]
    xf = x.reshape(n, d)
    w_gate, w_up, w_down = (w.astype(BF16) for w in (exp_w_gate, exp_w_up, exp_w_down))

    def moe_layer(layer, mix_a, mix_b, col_a, col_b, w_out, xin):
        sh1, sc1, g1, sh2, sc2, g2 = mods[layer]
        x1, hext, route, counts = _out_route(mix_a, mix_b, col_a, col_b, w_out.astype(BF16), xin, g1,
                                             norm2_g[layer], sc2, sh2, router_w, router_b, seq)
        pos, ea, eb, valid_rows, total = _routing_plan(route, counts, n)
        y = _moe(hext, pos, ea, eb, valid_rows, total, router_w.T.astype(F32), layer,
                 w_gate, w_up, w_down)
        return x1, y, g2

    sh1, sc1, _, _, _, _ = mods[0]
    qk_gain = jnp.stack([att_q_norm_g[0], att_k_norm_g[0]])
    proj0 = _project(xf, norm1_g[0], sc1, sh1, even_w_in[0].astype(BF16), seq, qk_gain=qk_gain)
    ret = _retention(proj0, bsz, seq)
    att = _chunk_attention(proj0, att_rel_bias[0], bsz, seq)
    x1, y0, g2_0 = moe_layer(0, ret, att, 0, 0, even_w_out[0], xf)

    sh1, sc1, _, _, _, _ = mods[1]
    x2, proj1 = _project(x1, norm1_g[1], sc1, sh1, odd_w_in[0].astype(BF16), seq, y=y0, gate=g2_0)
    sbo = _stick_breaking(proj1, bsz, seq)
    x3, y1, g2_1 = moe_layer(1, sbo, sbo, 0, 1, odd_w_out[0], x2)
    return _final(x3, y1, g2_1, seq).reshape(bsz, seq, d)
```

```python
import functools

import numpy as np
import jax
import jax.numpy as jnp
from jax import lax
from jax.experimental import pallas as pl
from jax.experimental.pallas import tpu as pltpu

F32 = jnp.float32
BF16 = jnp.bfloat16

D_MODEL = 1024
CHUNK = 64
EPS = 1e-6
ROPE_BASE = 10000.0
RET_HEADS = 4
RET_DIM = 128
ATT_HEADS = 8
ATT_DIM = 64
ATT_LEFT_CHUNKS = 8
REL_CLIP = 128
SB_HEADS = 16
SB_DIM = 64
N_EXPERTS = 16
N_GROUPS = 4
D_EXPERT = 512
HALF = 512

VMEM_LIMIT = 48 * 1024 * 1024
LANES = 128

ROW_TILE = 1024
RET_TILE = 256
ATT_QB = 256
ATT_WIN = ATT_LEFT_CHUNKS * CHUNK + ATT_QB
ATT_STEP_BLOCKS = 4
SB_BLK = 128
SB_ROWS = 64
SB_UNROLL = 8
SB_WIN = 3 * SB_BLK
SB_EXIT = -110.0

PAIR_A = (0, 0, 0, 1, 1, 2)
PAIR_B = (1, 2, 3, 2, 3, 3)
N_CLASS = N_GROUPS * len(PAIR_A)
CLASS_ROWS = 32
MOE_TILE = 256
ROW_SUB = D_MODEL // LANES
GATHER_BUFS = 3


def _cparams(sem):
    return pltpu.CompilerParams(dimension_semantics=sem, vmem_limit_bytes=VMEM_LIMIT)


def _mod_kernel(c_ref, w_ref, b_ref, o_ref):
    c = c_ref[...]
    ca = c * jax.nn.sigmoid(c)
    o_ref[0] = jnp.dot(ca.astype(BF16), w_ref[0].astype(BF16), preferred_element_type=F32) + b_ref[0]


def _modulation(c, ada_w, ada_b):
    depth, _, width = ada_w.shape
    bsz = c.shape[0]
    rows = 8
    cp = jnp.zeros((rows, D_MODEL), F32).at[:bsz].set(c)
    tn = 1536
    out = pl.pallas_call(
        _mod_kernel,
        grid=(depth, width // tn),
        in_specs=[pl.BlockSpec((rows, D_MODEL), lambda l, j: (0, 0)),
                  pl.BlockSpec((1, D_MODEL, tn), lambda l, j: (l, 0, j)),
                  pl.BlockSpec((1, 1, tn), lambda l, j: (l, 0, j))],
        out_specs=pl.BlockSpec((1, rows, tn), lambda l, j: (l, 0, j)),
        out_shape=jax.ShapeDtypeStruct((depth, rows, width), F32),
        compiler_params=_cparams(("arbitrary", "arbitrary")),
        name="adaln_mod",
    )(cp, ada_w, ada_b.reshape(depth, 1, width))
    return out[:, :bsz]


def _norm_mod(x, g, sc, sh):
    ms = jnp.mean(x * x, axis=-1, keepdims=True)
    return (x * lax.rsqrt(ms + EPS) * g) * (1.0 + sc) + sh


def _project_chunks(h, w_ref, o_ref, qk_ref, bd_ref, q_scale):
    for ci, n0 in enumerate(range(0, o_ref.shape[1], HALF)):
        p = jnp.dot(h, w_ref[:, n0:n0 + HALF], preferred_element_type=F32)
        if qk_ref is not None and ci in (4, 5):
            ms = jnp.dot((p * p).astype(BF16), bd_ref[...], preferred_element_type=F32)
            p = p * lax.rsqrt(ms + EPS) * qk_ref[ci - 4:ci - 3, :]
            if ci == 4:
                p = p * q_scale
        elif qk_ref is None and n0 < SB_HEADS * SB_DIM:
            p = p * q_scale
        o_ref[:, n0:n0 + HALF] = p.astype(BF16)


def _proj_kernel(x_ref, g_ref, sc_ref, sh_ref, w_ref, qk_ref, bd_ref, o_ref):
    h = _norm_mod(x_ref[...], g_ref[...], sc_ref[0], sh_ref[0]).astype(BF16)
    _project_chunks(h, w_ref, o_ref, qk_ref, bd_ref, ATT_DIM ** -0.5)


def _token_rows(y_ref):
    rows = y_ref.shape[0] // ROW_SUB
    return jnp.concatenate([y_ref[pl.ds(jt, rows, stride=ROW_SUB), :] for jt in range(ROW_SUB)],
                           axis=1)


def _proj_res_kernel(x_ref, y_ref, gt_ref, g_ref, sc_ref, sh_ref, w_ref, xo_ref, o_ref):
    x = x_ref[...] + gt_ref[0] * _token_rows(y_ref)
    xo_ref[...] = x
    h = _norm_mod(x, g_ref[...], sc_ref[0], sh_ref[0]).astype(BF16)
    _project_chunks(h, w_ref, o_ref, None, None, SB_DIM ** -0.5)


def _project(x, gain, sc, sh, w, seq, qk_gain=None, y=None, gate=None):
    n = x.shape[0]
    nout = w.shape[1]
    tpb = seq // ROW_TILE
    row = pl.BlockSpec((ROW_TILE, D_MODEL), lambda i: (i, 0))
    per_b = pl.BlockSpec((1, 1, D_MODEL), lambda i: (i // tpb, 0, 0))
    gspec = pl.BlockSpec((1, D_MODEL), lambda i: (0, 0))
    wspec = pl.BlockSpec((D_MODEL, nout), lambda i: (0, 0))
    ospec = pl.BlockSpec((ROW_TILE, nout), lambda i: (i, 0))
    oshape = jax.ShapeDtypeStruct((n, nout), BF16)
    if y is None:
        head = np.arange(HALF) // ATT_DIM
        bd = jnp.asarray((head[:, None] == head[None, :]).astype(np.float32) / ATT_DIM, BF16)
        qk = jnp.tile(qk_gain.astype(F32), (1, ATT_HEADS))
        return pl.pallas_call(
            _proj_kernel, grid=(n // ROW_TILE,),
            in_specs=[row, gspec, per_b, per_b, wspec,
                      pl.BlockSpec((2, HALF), lambda i: (0, 0)),
                      pl.BlockSpec((HALF, HALF), lambda i: (0, 0))],
            out_specs=ospec, out_shape=oshape,
            compiler_params=_cparams(("arbitrary",)), name="norm_proj",
        )(x, gain.reshape(1, D_MODEL), sc, sh, w, qk, bd)
    return pl.pallas_call(
        _proj_res_kernel, grid=(n // ROW_TILE,),
        in_specs=[row, pl.BlockSpec((ROW_TILE * ROW_SUB, LANES), lambda i: (i, 0)), per_b, gspec,
                  per_b, per_b, wspec],
        out_specs=[row, ospec],
        out_shape=[jax.ShapeDtypeStruct((n, D_MODEL), F32), oshape],
        compiler_params=_cparams(("arbitrary",)), name="res_norm_proj",
    )(x, y, gate, gain.reshape(1, D_MODEL), sc, sh, w)


def _ret_tables(seq):
    inv = ROPE_BASE ** (-np.arange(0, RET_DIM, 2, dtype=np.float64) / RET_DIM)
    ang = np.arange(seq, dtype=np.float64)[:, None] * inv[None, :]
    cos = np.concatenate([np.cos(ang), np.cos(ang)], axis=1)
    sin = np.concatenate([-np.sin(ang), np.sin(ang)], axis=1)
    log_g = np.log(1.0 - 2.0 ** (-5.0 - np.arange(RET_HEADS, dtype=np.float64)))
    idx = np.arange(RET_TILE)
    same_or_earlier = (idx[None, :] // CHUNK) <= (idx[:, None] // CHUNK)
    dec = np.exp(log_g[:, None, None] * np.abs(idx[:, None] - idx[None, :])) * same_or_earlier
    loc = idx.astype(np.float64)
    qdec = np.exp(log_g[:, None] * (loc[None, :] + 1.0))
    kdec = np.exp(log_g[:, None] * (RET_TILE - 1.0 - loc[None, :]))
    tdec = np.exp(log_g * RET_TILE)
    qk = np.stack([qdec, kdec], axis=1)[..., None] * np.ones((1, 1, 1, RET_DIM))
    return (jnp.asarray(cos, F32), jnp.asarray(sin, F32), jnp.asarray(dec, F32),
            jnp.asarray(qk, F32), tuple(float(t) for t in tdec))


def _ret_kernel(tdec, q_ref, k_ref, v_ref, g_ref, cos_ref, sin_ref, dec_ref, qk_ref, o_ref, st_ref):
    @pl.when(pl.program_id(1) == 0)
    def _():
        st_ref[...] = jnp.zeros_like(st_ref)

    cos = cos_ref[...]
    sin = sin_ref[...]
    half = RET_DIM // 2
    for h in range(RET_HEADS):
        cols = slice(h * RET_DIM, (h + 1) * RET_DIM)
        q = q_ref[:, cols].astype(F32)
        k = k_ref[:, cols].astype(F32)
        q = q * cos + pltpu.roll(q, half, 1) * sin
        k = (k * cos + pltpu.roll(k, half, 1) * sin) * (RET_DIM ** -0.5)
        v = v_ref[:, cols]
        qb = q.astype(BF16)
        kb = k.astype(BF16)
        s = lax.dot_general(qb, kb, (((1,), (1,)), ((), ())), preferred_element_type=F32)
        s = s * dec_ref[h]
        o = jnp.dot(s.astype(BF16), v, preferred_element_type=F32)
        st = st_ref[h]
        o = o + jnp.dot((q * qk_ref[h, 0]).astype(BF16), st.astype(BF16), preferred_element_type=F32)
        kd = (k * qk_ref[h, 1]).astype(BF16)
        st_ref[h] = st * tdec[h] + lax.dot_general(kd, v, (((0,), (0,)), ((), ())),
                                                   preferred_element_type=F32)
        o = o * lax.rsqrt(jnp.mean(o * o, axis=-1, keepdims=True) + EPS)
        g = g_ref[:, cols].astype(F32)
        o_ref[:, cols] = (o * (g * jax.nn.sigmoid(g))).astype(BF16)


def _retention(proj, bsz, seq):
    n = proj.shape[0]
    nt = seq // RET_TILE
    cos, sin, dec, qk, tdec = _ret_tables(seq)

    def col(c):
        return pl.BlockSpec((RET_TILE, HALF), lambda b, i: (b * nt + i, c))

    pos = pl.BlockSpec((RET_TILE, RET_DIM), lambda b, i: (i, 0))
    return pl.pallas_call(
        functools.partial(_ret_kernel, tdec),
        grid=(bsz, nt),
        in_specs=[col(0), col(1), col(2), col(3), pos, pos,
                  pl.BlockSpec((RET_HEADS, RET_TILE, RET_TILE), lambda b, i: (0, 0, 0)),
                  pl.BlockSpec((RET_HEADS, 2, RET_TILE, RET_DIM), lambda b, i: (0, 0, 0, 0))],
        out_specs=pl.BlockSpec((RET_TILE, HALF), lambda b, i: (b * nt + i, 0)),
        out_shape=jax.ShapeDtypeStruct((n, HALF), BF16),
        scratch_shapes=[pltpu.VMEM((RET_HEADS, RET_DIM, RET_DIM), F32)],
        compiler_params=_cparams(("arbitrary", "arbitrary")), name="retention",
    )(proj, proj, proj, proj, cos, sin, dec, qk)


def _att_bias(rel_bias):
    span = ATT_WIN + ATT_QB
    m = np.concatenate([np.arange(0, ATT_WIN), np.zeros(1, np.int64), np.arange(-(ATT_QB - 1), 0)])
    idx = np.clip((ATT_WIN - ATT_QB) - m, -(CHUNK - 1), REL_CLIP) + (CHUNK - 1)
    vec = rel_bias[:, idx].astype(F32)
    heads = rel_bias.shape[0]
    toep = jnp.tile(vec, (1, ATT_QB))[:, :ATT_QB * (span - 1)].reshape(heads, ATT_QB, span - 1)
    qc = np.arange(ATT_QB)[:, None] // CHUNK
    kc = np.arange(ATT_WIN)[None, :] // CHUNK - ATT_LEFT_CHUNKS
    ok = (kc <= qc) & (kc >= qc - ATT_LEFT_CHUNKS)
    return jnp.where(jnp.asarray(ok)[None], toep[:, :, :ATT_WIN], -jnp.inf)


def _fill_padded(pad_ref, src_ref, front):
    pad_ref[:front, :] = jnp.zeros((front, pad_ref.shape[1]), pad_ref.dtype)
    pad_ref[front:, :] = src_ref[...]


def _att_kernel(q_ref, k_ref, v_ref, bias_ref, o_ref, kpad, vpad):
    i = pl.program_id(1)
    front = ATT_WIN - ATT_QB

    @pl.when(i == 0)
    def _():
        _fill_padded(kpad, k_ref, front)
        _fill_padded(vpad, v_ref, front)

    col = lax.broadcasted_iota(jnp.int32, (1, ATT_WIN), 1)
    first = lax.broadcasted_iota(jnp.int32, (ATT_QB, LANES), 1) < ATT_DIM
    nt = (((1,), (1,)), ((), ()))
    for sub in range(ATT_STEP_BLOCKS):
        rows = slice(sub * ATT_QB, (sub + 1) * ATT_QB)
        start = pl.multiple_of((i * ATT_STEP_BLOCKS + sub) * ATT_QB, ATT_QB)
        pen = jnp.where(col >= front - start, 0.0, -jnp.inf)
        for p in range(ATT_HEADS * ATT_DIM // LANES):
            cols = slice(p * LANES, (p + 1) * LANES)
            q = q_ref[rows, cols]
            kw = kpad[pl.ds(start, ATT_WIN), cols]
            vw = vpad[pl.ds(start, ATT_WIN), cols]
            zero = jnp.zeros_like(q)
            qq = jnp.concatenate([jnp.where(first, q, zero), jnp.where(first, zero, q)], axis=0)
            bias = jnp.concatenate([bias_ref[2 * p], bias_ref[2 * p + 1]], axis=0)
            s = lax.dot_general(qq, kw, nt, preferred_element_type=F32) + bias + pen
            e = jnp.exp(s - jnp.max(s, axis=-1, keepdims=True))
            den = jnp.sum(e, axis=-1, keepdims=True)
            o = jnp.dot(e.astype(BF16), vw, preferred_element_type=F32) / den
            o_ref[rows, cols] = jnp.where(first, o[:ATT_QB], o[ATT_QB:]).astype(BF16)


def _chunk_attention(proj, rel_bias, bsz, seq):
    n = proj.shape[0]
    rows = ATT_STEP_BLOCKS * ATT_QB
    nq = seq // rows
    front = ATT_WIN - ATT_QB
    return pl.pallas_call(
        _att_kernel,
        grid=(bsz, nq),
        in_specs=[pl.BlockSpec((rows, HALF), lambda b, i: (b * nq + i, 4)),
                  pl.BlockSpec((seq, HALF), lambda b, i: (b, 5)),
                  pl.BlockSpec((seq, HALF), lambda b, i: (b, 6)),
                  pl.BlockSpec((ATT_HEADS, ATT_QB, ATT_WIN), lambda b, i: (0, 0, 0))],
        out_specs=pl.BlockSpec((rows, HALF), lambda b, i: (b * nq + i, 0)),
        out_shape=jax.ShapeDtypeStruct((n, HALF), BF16),
        scratch_shapes=[pltpu.VMEM((front + seq, HALF), BF16), pltpu.VMEM((front + seq, HALF), BF16)],
        compiler_params=_cparams(("arbitrary", "arbitrary")), name="chunk_attention",
    )(proj, proj, proj, _att_bias(rel_bias))


def _sb_logs(z):
    log_beta = jnp.minimum(z, 0.0) - jnp.log(1.0 + jnp.exp(-jnp.abs(z)))
    return log_beta, log_beta - z


def _sb_kernel(q_ref, k_ref, v_ref, tri_ref, o_ref, kpad, vpad, run_ref, acc_ref):
    front = SB_WIN - SB_ROWS
    nblk = SB_WIN // SB_BLK
    pairs = HALF // LANES
    _fill_padded(kpad, k_ref, front)
    _fill_padded(vpad, v_ref, front)
    row = lax.broadcasted_iota(jnp.int32, (SB_ROWS, SB_WIN), 0)
    col = lax.broadcasted_iota(jnp.int32, (SB_ROWS, SB_WIN), 1)
    causal_col = jnp.where(col - row < front, col, -1)
    first = lax.broadcasted_iota(jnp.int32, (SB_ROWS, LANES), 1) < SB_DIM
    nt = (((1,), (1,)), ((), ()))

    def first_window(iq, u):
        start = pl.multiple_of(iq * SB_ROWS, SB_ROWS)
        mask = causal_col >= jnp.maximum(front - start, 0)
        tri = tri_ref[...]

        def heads_of(p):
            q = q_ref[pl.ds(start, SB_ROWS), p * LANES:(p + 1) * LANES]
            zero = jnp.zeros_like(q)
            return jnp.where(first, q, zero), jnp.where(first, zero, q)

        alive = None
        mask2 = jnp.concatenate([mask, mask], axis=0)
        lbs, l1s, vws = [], [], []
        for p in range(pairs):
            cols = slice(p * LANES, (p + 1) * LANES)
            kw = kpad[pl.ds(start, SB_WIN), cols]
            vws.append(vpad[pl.ds(start, SB_WIN), cols])
            qq = jnp.concatenate(heads_of(p), axis=0)
            z = lax.dot_general(qq, kw, nt, preferred_element_type=F32)
            log_beta, log_1mb = _sb_logs(z)
            lbs.append(log_beta)
            l1 = jnp.where(mask2, log_1mb, 0.0).astype(BF16)
            l1s += [l1[hh * SB_ROWS:(hh + 1) * SB_ROWS, c * SB_BLK:(c + 1) * SB_BLK]
                    for hh in range(2) for c in range(nblk)]
        cs = jnp.dot(jnp.concatenate(l1s, axis=0), tri, preferred_element_type=F32)
        for p in range(pairs):
            rows = []
            for hh in range(2):
                h = 2 * p + hh
                run = jnp.zeros((SB_ROWS, SB_BLK), F32)
                accs = [None] * nblk
                for c in reversed(range(nblk)):
                    blk = cs[(h * nblk + c) * SB_ROWS:(h * nblk + c + 1) * SB_ROWS]
                    accs[c] = blk[:, :SB_BLK] + run
                    run = run + blk[:, SB_BLK:]
                rows.append(jnp.concatenate(accs, axis=1))
                run_ref[u, h] = run
                alive = run if alive is None else jnp.maximum(alive, run)
            a = jnp.where(mask2, jnp.exp(lbs[p] + jnp.concatenate(rows, axis=0)), 0.0)
            o = jnp.dot(a.astype(BF16), vws[p], preferred_element_type=F32)
            acc_ref[u, 2 * p] = o[:SB_ROWS]
            acc_ref[u, 2 * p + 1] = o[SB_ROWS:]
        return jnp.max(alive)

    def older_keys(iq, u, alive0):
        start = pl.multiple_of(iq * SB_ROWS, SB_ROWS)
        tri = tri_ref[...]

        def heads_of(p):
            q = q_ref[pl.ds(start, SB_ROWS), p * LANES:(p + 1) * LANES]
            zero = jnp.zeros_like(q)
            return jnp.where(first, q, zero), jnp.where(first, zero, q)

        def cond(c):
            return jnp.logical_and(c[0] >= 0, c[1] > SB_EXIT)

        def body(c):
            j = c[0]
            kstart = pl.multiple_of(front + j * SB_ROWS, SB_ROWS)
            alive = None
            for p in range(pairs):
                cols = slice(p * LANES, (p + 1) * LANES)
                kb = kpad[pl.ds(kstart, SB_ROWS), cols]
                vb = vpad[pl.ds(kstart, SB_ROWS), cols]
                for hh, qm in enumerate(heads_of(p)):
                    z = lax.dot_general(qm, kb, nt, preferred_element_type=F32)
                    log_beta, log_1mb = _sb_logs(z)
                    cs = jnp.dot(log_1mb.astype(BF16), tri[:SB_ROWS], preferred_element_type=F32)
                    run = run_ref[u, 2 * p + hh]
                    a = jnp.exp(log_beta + run[:, :SB_ROWS] + cs[:, :SB_ROWS])
                    acc_ref[u, 2 * p + hh] += jnp.dot(a.astype(BF16), vb, preferred_element_type=F32)
                    run = run + cs[:, SB_BLK:]
                    run_ref[u, 2 * p + hh] = run
                    alive = run if alive is None else jnp.maximum(alive, run)
            return j - 1, jnp.max(alive)

        lax.while_loop(cond, body, (iq - SB_WIN // SB_ROWS, alive0))
        for p in range(pairs):
            o_ref[pl.ds(start, SB_ROWS), p * LANES:(p + 1) * LANES] = jnp.where(
                first, acc_ref[u, 2 * p], acc_ref[u, 2 * p + 1]).astype(BF16)

    def query_blocks(it, carry):
        alive = [first_window(it * SB_UNROLL + u, u) for u in range(SB_UNROLL)]
        for u in range(SB_UNROLL):
            older_keys(it * SB_UNROLL + u, u, alive[u])
        return carry

    lax.fori_loop(0, q_ref.shape[0] // (SB_ROWS * SB_UNROLL), query_blocks, 0)


def _stick_breaking(proj, bsz, seq):
    n = proj.shape[0]
    groups = SB_HEADS * SB_DIM // HALF
    front = SB_WIN - SB_ROWS
    j = np.arange(SB_BLK)
    tri = np.concatenate([(j[:, None] > j[None, :]).astype(np.float32),
                          np.ones((SB_BLK, SB_BLK), np.float32)], axis=1)
    return pl.pallas_call(
        _sb_kernel,
        grid=(bsz, groups),
        in_specs=[pl.BlockSpec((seq, HALF), lambda b, g: (b, g)),
                  pl.BlockSpec((seq, HALF), lambda b, g: (b, groups + g)),
                  pl.BlockSpec((seq, HALF), lambda b, g: (b, 2 * groups + g)),
                  pl.BlockSpec((SB_BLK, 2 * SB_BLK), lambda b, g: (0, 0))],
        out_specs=pl.BlockSpec((seq, HALF), lambda b, g: (b, g)),
        out_shape=jax.ShapeDtypeStruct((n, SB_HEADS * SB_DIM), BF16),
        scratch_shapes=[pltpu.VMEM((front + seq, HALF), BF16), pltpu.VMEM((front + seq, HALF), BF16),
                        pltpu.VMEM((SB_UNROLL, HALF // SB_DIM, SB_ROWS, SB_BLK), F32),
                        pltpu.VMEM((SB_UNROLL, HALF // SB_DIM, SB_ROWS, LANES), F32)],
        compiler_params=_cparams(("arbitrary", "arbitrary")), name="stick_breaking",
    )(proj, proj, proj, jnp.asarray(tri, BF16))


def _out_route_kernel(ma_ref, mb_ref, wa_ref, wb_ref, x_ref, g1_ref, g_ref, sc_ref, sh_ref,
                      rw_ref, rb_ref, tri_ref, xo_ref, he_ref, rt_ref, cnt_ref, carry_ref):
    @pl.when(pl.program_id(0) == 0)
    def _():
        carry_ref[...] = jnp.zeros_like(carry_ref)

    tm = x_ref.shape[0]
    mix = (jnp.dot(ma_ref[...], wa_ref[...], preferred_element_type=F32)
           + jnp.dot(mb_ref[...], wb_ref[...], preferred_element_type=F32))
    x = x_ref[...] + g1_ref[0] * mix
    xo_ref[...] = x
    h = _norm_mod(x, g_ref[...], sc_ref[0], sh_ref[0])
    for jt in range(ROW_SUB):
        he_ref[pl.ds(jt, tm, stride=ROW_SUB), :] = h[:, jt * LANES:(jt + 1) * LANES]

    nt = (((1,), (1,)), ((), ()))
    h_hi = h.astype(BF16)
    h_lo = (h - h_hi.astype(F32)).astype(BF16)
    w = rw_ref[...]
    w_hi = w.astype(BF16)
    w_lo = (w - w_hi.astype(F32)).astype(BF16)
    logits = (lax.dot_general(w_hi, h_hi, nt, preferred_element_type=F32)
              + lax.dot_general(w_hi, h_lo, nt, preferred_element_type=F32)
              + lax.dot_general(w_lo, h_hi, nt, preferred_element_type=F32))
    sel = jax.nn.sigmoid(logits) + rb_ref[...]

    best = None
    for g in range(N_GROUPS):
        for p in range(len(PAIR_A)):
            ea, eb = 4 * g + PAIR_A[p], 4 * g + PAIR_B[p]
            val = sel[ea:ea + 1, :] + sel[eb:eb + 1, :]
            cid = jnp.full((1, tm), float(len(PAIR_A) * g + p), F32)
            cand = (val, cid)
            if best is None:
                best = cand
            else:
                take = val > best[0]
                best = tuple(jnp.where(take, c_, b_) for c_, b_ in zip(cand, best))
    cls = best[1]

    crow = lax.broadcasted_iota(jnp.int32, (CLASS_ROWS, tm), 0).astype(F32)
    onehot = jnp.where(crow == cls, 1.0, 0.0)
    before = jnp.dot(onehot.astype(BF16), tri_ref[...], preferred_element_type=F32)
    carry = carry_ref[...]
    rank = jnp.sum(onehot * (before + carry[:, :1]), axis=0, keepdims=True)
    carry = carry + jnp.sum(onehot, axis=1, keepdims=True)
    carry_ref[...] = carry
    cnt_ref[0] = carry

    srow = lax.broadcasted_iota(jnp.int32, (8, tm), 0)
    rt_ref[0] = jnp.where(srow == 0, cls, jnp.where(srow == 1, rank, 0.0))


def _out_route(mix_a, mix_b, col_a, col_b, w_out, x, g1, gain, sc, sh, router_w, router_b, seq):
    n = x.shape[0]
    tm = ROW_TILE
    nt = n // tm
    tpb = seq // tm
    t = np.arange(tm)
    tri = jnp.asarray((t[:, None] < t[None, :]).astype(np.float32), BF16)
    row = pl.BlockSpec((tm, D_MODEL), lambda i: (i, 0))
    per_b = pl.BlockSpec((1, 1, D_MODEL), lambda i: (i // tpb, 0, 0))
    rb = jnp.broadcast_to(router_b.astype(F32)[:, None], (N_EXPERTS, tm))
    return pl.pallas_call(
        _out_route_kernel,
        grid=(nt,),
        in_specs=[pl.BlockSpec((tm, HALF), lambda i: (i, col_a)),
                  pl.BlockSpec((tm, HALF), lambda i: (i, col_b)),
                  pl.BlockSpec((HALF, D_MODEL), lambda i: (0, 0)),
                  pl.BlockSpec((HALF, D_MODEL), lambda i: (1, 0)),
                  row, per_b,
                  pl.BlockSpec((1, D_MODEL), lambda i: (0, 0)), per_b, per_b,
                  pl.BlockSpec((N_EXPERTS, D_MODEL), lambda i: (0, 0)),
                  pl.BlockSpec((N_EXPERTS, tm), lambda i: (0, 0)),
                  pl.BlockSpec((tm, tm), lambda i: (0, 0))],
        out_specs=[row,
                   pl.BlockSpec((tm * ROW_SUB, LANES), lambda i: (i, 0)),
                   pl.BlockSpec((1, 8, tm), lambda i: (i, 0, 0)),
                   pl.BlockSpec((1, CLASS_ROWS, LANES), lambda i: (i, 0, 0))],
        out_shape=[jax.ShapeDtypeStruct((n, D_MODEL), F32),
                   jax.ShapeDtypeStruct((n * ROW_SUB, LANES), F32),
                   jax.ShapeDtypeStruct((nt, 8, tm), F32),
                   jax.ShapeDtypeStruct((nt, CLASS_ROWS, LANES), F32)],
        scratch_shapes=[pltpu.VMEM((CLASS_ROWS, LANES), F32)],
        compiler_params=_cparams(("arbitrary",)), name="out_proj_route",
    )(mix_a, mix_b, w_out, w_out, x, g1, gain.reshape(1, D_MODEL), sc, sh,
      router_w.T.astype(F32), rb, tri)


def _fill_slots(pos_ref, nv_ref, inv_ref):
    batch = 16

    def fill_tile(j, c):
        def fill(g, c2):
            for u in range(batch):
                inv_ref[j * MOE_TILE + g * batch + u] = 0
            return c2

        return lax.fori_loop(nv_ref[j] // batch, MOE_TILE // batch, fill, c)

    lax.fori_loop(0, nv_ref.shape[0], fill_tile, 0)

    def place(g, c):
        slots = [pos_ref[g * batch + u] for u in range(batch)]
        for u in range(batch):
            inv_ref[slots[u]] = g * batch + u
        return c

    lax.fori_loop(0, pos_ref.shape[0] // batch, place, 0)


def _routing_plan(route, counts, n):
    nt, _, tm = route.shape
    cls = route[:, 0, :].reshape(n).astype(jnp.int32)
    rank = route[:, 1, :].reshape(n).astype(jnp.int32)
    cnt = counts[-1, :N_CLASS, 0].astype(jnp.int32)
    tiles_c = (cnt + MOE_TILE - 1) // MOE_TILE
    ends = jnp.cumsum(tiles_c)
    starts = ends - tiles_c
    pos = starts[cls] * MOE_TILE + rank
    max_tiles = n // MOE_TILE + N_CLASS
    total = ends[-1]
    j = jnp.arange(max_tiles, dtype=jnp.int32)
    jj = jnp.minimum(j, total - 1)
    tcls = jnp.sum((ends[None, :] <= jj[:, None]).astype(jnp.int32), axis=1)
    grp = tcls // len(PAIR_A)
    pair = tcls % len(PAIR_A)
    ea = 4 * grp + jnp.asarray(PAIR_A, jnp.int32)[pair]
    eb = 4 * grp + jnp.asarray(PAIR_B, jnp.int32)[pair]
    valid_rows = jnp.clip(cnt[tcls] - (jj - starts[tcls]) * MOE_TILE, 0, MOE_TILE)
    valid_rows = jnp.where(j < total, valid_rows, 0).astype(jnp.int32)
    return pos, ea, eb, valid_rows, total.reshape(1)


def _moe_kernel(pos_ref, ea_ref, eb_ref, nv_ref, tot_ref, h_hbm, rw_ref, wg_a, wu_a, wd_a,
                wg_b, wu_b, wd_b, y_hbm, gbuf, ybuf, inv_ref, gsem, ssem):
    j = pl.program_id(0)
    total = tot_ref[0]
    gslot = j % GATHER_BUFS
    yslot = j % 2
    unroll = 8

    def gather_row(tile, s, r, prio):
        src = pl.multiple_of(inv_ref[tile * MOE_TILE + r] * ROW_SUB, ROW_SUB)
        dst = pl.multiple_of(r * ROW_SUB, ROW_SUB)
        pltpu.make_async_copy(h_hbm.at[pl.ds(src, ROW_SUB), :], gbuf.at[s, pl.ds(dst, ROW_SUB), :],
                              gsem.at[s]).start(priority=prio)

    def scatter_row(tile, s, r, prio):
        src = pl.multiple_of(r * ROW_SUB, ROW_SUB)
        dst = pl.multiple_of(inv_ref[tile * MOE_TILE + r] * ROW_SUB, ROW_SUB)
        pltpu.make_async_copy(ybuf.at[s, pl.ds(src, ROW_SUB), :], y_hbm.at[pl.ds(dst, ROW_SUB), :],
                              ssem.at[s]).start(priority=prio)

    def gathered_rows(tile):
        return (nv_ref[tile] + unroll - 1) // unroll * unroll

    def gather_rows(tile, s):
        def group(g, c):
            for u in range(unroll):
                gather_row(tile, s, g * unroll + u, u % 2)
            return c

        lax.fori_loop(0, gathered_rows(tile) // unroll, group, 0)

    def wait_rows(count, make_copy):
        for bit in range(MOE_TILE.bit_length()):
            @pl.when((count >> bit) & 1 == 1)
            def _():
                make_copy((1 << bit) * ROW_SUB).wait()

    def gather_wait(tile, s):
        wait_rows(gathered_rows(tile), lambda size: pltpu.make_async_copy(
            h_hbm.at[pl.ds(0, size), :], gbuf.at[s, pl.ds(0, size), :], gsem.at[s]))

    def scatter_rows(tile, s):
        nv = nv_ref[tile]
        groups = nv // unroll

        def group(g, c):
            for u in range(unroll):
                scatter_row(tile, s, g * unroll + u, u % 2)
            return c

        lax.fori_loop(0, groups, group, 0)
        for u in range(unroll):
            @pl.when(groups * unroll + u < nv)
            def _():
                scatter_row(tile, s, groups * unroll + u, u % 2)

    def scatter_wait(tile, s):
        wait_rows(nv_ref[tile], lambda size: pltpu.make_async_copy(
            ybuf.at[s, pl.ds(0, size), :], y_hbm.at[pl.ds(0, size), :], ssem.at[s]))

    @pl.when(j == 0)
    def _():
        _fill_slots(pos_ref, nv_ref, inv_ref)
        gbuf[...] = jnp.zeros_like(gbuf)
        for t in range(GATHER_BUFS - 1):
            @pl.when(t < total)
            def _():
                gather_rows(t, t)

    @pl.when(j < total)
    def _():
        gather_wait(j, gslot)
        x = jnp.concatenate([gbuf[gslot, pl.ds(jt, MOE_TILE, stride=ROW_SUB), :]
                             for jt in range(ROW_SUB)], axis=1)
        xb = x.astype(BF16)
        scores = [jax.nn.sigmoid(jnp.sum(x * rw_ref[pl.ds(e_ref[j], 1), :], axis=-1, keepdims=True))
                  for e_ref in (ea_ref, eb_ref)]
        shares = [sc_ / (scores[0] + scores[1]) for sc_ in scores]
        y = None
        for lane, (wg, wu, wd) in enumerate(((wg_a, wu_a, wd_a), (wg_b, wu_b, wd_b))):
            g = jnp.dot(xb, wg[0], preferred_element_type=F32)
            u = jnp.dot(xb, wu[0], preferred_element_type=F32)
            he = ((g * jax.nn.sigmoid(g)) * u * shares[lane]).astype(BF16)
            part = jnp.dot(he, wd[0], preferred_element_type=F32)
            y = part if y is None else y + part

        @pl.when(j >= 2)
        def _():
            scatter_wait(j - 2, yslot)

        for jt in range(ROW_SUB):
            chunk = y[:, jt * LANES:(jt + 1) * LANES]
            ybuf[yslot, pl.ds(jt, MOE_TILE, stride=ROW_SUB), :] = chunk

        scatter_rows(j, yslot)
        ahead = j + GATHER_BUFS - 1

        @pl.when(ahead < total)
        def _():
            gather_rows(ahead, ahead % GATHER_BUFS)

        @pl.when(j == total - 1)
        def _():
            @pl.when(j >= 1)
            def _():
                scatter_wait(j - 1, 1 - yslot)

            scatter_wait(j, yslot)


def _moe(hext, pos, ea, eb, valid_rows, total, router_wt, layer, w_gate, w_up, w_down):
    n = hext.shape[0] // ROW_SUB
    max_tiles = ea.shape[0]

    def wspec(shape, which):
        if which == 0:
            return pl.BlockSpec((None, 1) + shape,
                                lambda j, inv_, ea_, eb_, nv_, t_: (layer, ea_[j], 0, 0))
        return pl.BlockSpec((None, 1) + shape, lambda j, inv_, ea_, eb_, nv_, t_: (layer, eb_[j], 0, 0))

    up = (D_MODEL, D_EXPERT)
    down = (D_EXPERT, D_MODEL)
    grid_spec = pltpu.PrefetchScalarGridSpec(
        num_scalar_prefetch=5,
        grid=(max_tiles,),
        in_specs=[pl.BlockSpec(memory_space=pl.ANY),
                  pl.BlockSpec((N_EXPERTS, D_MODEL), lambda j, inv_, ea_, eb_, nv_, t_: (0, 0)),
                  wspec(up, 0), wspec(up, 0), wspec(down, 0),
                  wspec(up, 1), wspec(up, 1), wspec(down, 1)],
        out_specs=pl.BlockSpec(memory_space=pl.ANY),
        scratch_shapes=[pltpu.VMEM((GATHER_BUFS, MOE_TILE * ROW_SUB, LANES), F32),
                        pltpu.VMEM((2, MOE_TILE * ROW_SUB, LANES), F32),
                        pltpu.SMEM((max_tiles * MOE_TILE,), jnp.int32),
                        pltpu.SemaphoreType.DMA((GATHER_BUFS,)),
                        pltpu.SemaphoreType.DMA((2,))],
    )
    return pl.pallas_call(
        _moe_kernel,
        grid_spec=grid_spec,
        out_shape=jax.ShapeDtypeStruct((n * ROW_SUB, LANES), F32),
        compiler_params=_cparams(("arbitrary",)), name="moe_experts",
    )(pos, ea, eb, valid_rows, total, hext, router_wt, w_gate, w_up, w_down, w_gate, w_up, w_down)


def _final_kernel(x_ref, y_ref, g_ref, o_ref):
    o_ref[...] = x_ref[...] + g_ref[0] * _token_rows(y_ref)


def _final(x, y, gate, seq):
    n = x.shape[0]
    tpb = seq // ROW_TILE
    row = pl.BlockSpec((ROW_TILE, D_MODEL), lambda i: (i, 0))
    return pl.pallas_call(
        _final_kernel, grid=(n // ROW_TILE,),
        in_specs=[row, pl.BlockSpec((ROW_TILE * ROW_SUB, LANES), lambda i: (i, 0)),
                  pl.BlockSpec((1, 1, D_MODEL), lambda i: (i // tpb, 0, 0))],
        out_specs=row, out_shape=jax.ShapeDtypeStruct((n, D_MODEL), F32),
        compiler_params=_cparams(("arbitrary",)), name="final_residual",
    )(x, y, gate)


def kernel(x, c, ada_w, ada_b, norm1_g, norm2_g, even_w_in, even_w_out, att_q_norm_g, att_k_norm_g,
           att_rel_bias, odd_w_in, odd_w_out, router_w, router_b, exp_w_gate, exp_w_up, exp_w_down):
    bsz, seq, d = x.shape
    n = bsz * seq
    mod = _modulation(c, ada_w, ada_b)
    mods = [[mod[l, :, k * d:(k + 1) * d].reshape(bsz, 1, d) for k in range(6)] for l in range(2)]
    xf = x.reshape(n, d)
    w_gate, w_up, w_down = (w.astype(BF16) for w in (exp_w_gate, exp_w_up, exp_w_down))

    def moe_layer(layer, mix_a, mix_b, col_a, col_b, w_out, xin):
        sh1, sc1, g1, sh2, sc2, g2 = mods[layer]
        x1, hext, route, counts = _out_route(mix_a, mix_b, col_a, col_b, w_out.astype(BF16), xin, g1,
                                             norm2_g[layer], sc2, sh2, router_w, router_b, seq)
        pos, ea, eb, valid_rows, total = _routing_plan(route, counts, n)
        y = _moe(hext, pos, ea, eb, valid_rows, total, router_w.T.astype(F32), layer,
                 w_gate, w_up, w_down)
        return x1, y, g2

    sh1, sc1, _, _, _, _ = mods[0]
    qk_gain = jnp.stack([att_q_norm_g[0], att_k_norm_g[0]])
    proj0 = _project(xf, norm1_g[0], sc1, sh1, even_w_in[0].astype(BF16), seq, qk_gain=qk_gain)
    ret = _retention(proj0, bsz, seq)
    att = _chunk_attention(proj0, att_rel_bias[0], bsz, seq)
    x1, y0, g2_0 = moe_layer(0, ret, att, 0, 0, even_w_out[0], xf)

    sh1, sc1, _, _, _, _ = mods[1]
    x2, proj1 = _project(x1, norm1_g[1], sc1, sh1, odd_w_in[0].astype(BF16), seq, y=y0, gate=g2_0)
    sbo = _stick_breaking(proj1, bsz, seq)
    x3, y1, g2_1 = moe_layer(1, sbo, sbo, 0, 1, odd_w_out[0], x2)
    return _final(x3, y1, g2_1, seq).reshape(bsz, seq, d)
```

```python
import functools

import numpy as np
import jax
import jax.numpy as jnp
from jax import lax
from jax.experimental import pallas as pl
from jax.experimental.pallas import tpu as pltpu

F32 = jnp.float32
BF16 = jnp.bfloat16

D_MODEL = 1024
CHUNK = 64
EPS = 1e-6
ROPE_BASE = 10000.0
RET_HEADS = 4
RET_DIM = 128
ATT_HEADS = 8
ATT_DIM = 64
ATT_LEFT_CHUNKS = 8
REL_CLIP = 128
SB_HEADS = 16
SB_DIM = 64
N_EXPERTS = 16
N_GROUPS = 4
D_EXPERT = 512
HALF = 512

VMEM_LIMIT = 48 * 1024 * 1024
LANES = 128

ROW_TILE = 1024
RET_TILE = 256
ATT_QB = 256
ATT_WIN = ATT_LEFT_CHUNKS * CHUNK + ATT_QB
ATT_STEP_BLOCKS = 4
SB_BLK = 128
SB_ROWS = 64
SB_UNROLL = 8
SB_WIN = 3 * SB_BLK
SB_EXIT = -110.0

PAIR_A = (0, 0, 0, 1, 1, 2)
PAIR_B = (1, 2, 3, 2, 3, 3)
N_CLASS = N_GROUPS * len(PAIR_A)
CLASS_ROWS = 32
MOE_TILE = 256
ROW_SUB = D_MODEL // LANES
GATHER_BUFS = 3


def _cparams(sem):
    return pltpu.CompilerParams(dimension_semantics=sem, vmem_limit_bytes=VMEM_LIMIT)


def _mod_kernel(c_ref, w_ref, b_ref, o_ref):
    c = c_ref[...]
    ca = c * jax.nn.sigmoid(c)
    o_ref[0] = jnp.dot(ca.astype(BF16), w_ref[0].astype(BF16), preferred_element_type=F32) + b_ref[0]


def _modulation(c, ada_w, ada_b):
    depth, _, width = ada_w.shape
    bsz = c.shape[0]
    rows = 8
    cp = jnp.zeros((rows, D_MODEL), F32).at[:bsz].set(c)
    tn = 1536
    out = pl.pallas_call(
        _mod_kernel,
        grid=(depth, width // tn),
        in_specs=[pl.BlockSpec((rows, D_MODEL), lambda l, j: (0, 0)),
                  pl.BlockSpec((1, D_MODEL, tn), lambda l, j: (l, 0, j)),
                  pl.BlockSpec((1, 1, tn), lambda l, j: (l, 0, j))],
        out_specs=pl.BlockSpec((1, rows, tn), lambda l, j: (l, 0, j)),
        out_shape=jax.ShapeDtypeStruct((depth, rows, width), F32),
        compiler_params=_cparams(("arbitrary", "arbitrary")),
        name="adaln_mod",
    )(cp, ada_w, ada_b.reshape(depth, 1, width))
    return out[:, :bsz]


def _norm_mod(x, g, sc, sh):
    ms = jnp.mean(x * x, axis=-1, keepdims=True)
    return (x * lax.rsqrt(ms + EPS) * g) * (1.0 + sc) + sh


def _project_chunks(h, w_ref, o_ref, qk_ref, bd_ref, q_scale):
    for ci, n0 in enumerate(range(0, o_ref.shape[1], HALF)):
        p = jnp.dot(h, w_ref[:, n0:n0 + HALF], preferred_element_type=F32)
        if qk_ref is not None and ci in (4, 5):
            ms = jnp.dot((p * p).astype(BF16), bd_ref[...], preferred_element_type=F32)
            p = p * lax.rsqrt(ms + EPS) * qk_ref[ci - 4:ci - 3, :]
            if ci == 4:
                p = p * q_scale
        elif qk_ref is None and n0 < SB_HEADS * SB_DIM:
            p = p * q_scale
        o_ref[:, n0:n0 + HALF] = p.astype(BF16)


def _proj_kernel(x_ref, g_ref, sc_ref, sh_ref, w_ref, qk_ref, bd_ref, o_ref):
    h = _norm_mod(x_ref[...], g_ref[...], sc_ref[0], sh_ref[0]).astype(BF16)
    _project_chunks(h, w_ref, o_ref, qk_ref, bd_ref, ATT_DIM ** -0.5)


def _token_rows(y_ref):
    rows = y_ref.shape[0] // ROW_SUB
    return jnp.concatenate([y_ref[pl.ds(jt, rows, stride=ROW_SUB), :] for jt in range(ROW_SUB)],
                           axis=1)


def _proj_res_kernel(x_ref, y_ref, gt_ref, g_ref, sc_ref, sh_ref, w_ref, xo_ref, o_ref):
    x = x_ref[...] + gt_ref[0] * _token_rows(y_ref)
    xo_ref[...] = x
    h = _norm_mod(x, g_ref[...], sc_ref[0], sh_ref[0]).astype(BF16)
    _project_chunks(h, w_ref, o_ref, None, None, SB_DIM ** -0.5)


def _project(x, gain, sc, sh, w, seq, qk_gain=None, y=None, gate=None):
    n = x.shape[0]
    nout = w.shape[1]
    tpb = seq // ROW_TILE
    row = pl.BlockSpec((ROW_TILE, D_MODEL), lambda i: (i, 0))
    per_b = pl.BlockSpec((1, 1, D_MODEL), lambda i: (i // tpb, 0, 0))
    gspec = pl.BlockSpec((1, D_MODEL), lambda i: (0, 0))
    wspec = pl.BlockSpec((D_MODEL, nout), lambda i: (0, 0))
    ospec = pl.BlockSpec((ROW_TILE, nout), lambda i: (i, 0))
    oshape = jax.ShapeDtypeStruct((n, nout), BF16)
    if y is None:
        head = np.arange(HALF) // ATT_DIM
        bd = jnp.asarray((head[:, None] == head[None, :]).astype(np.float32) / ATT_DIM, BF16)
        qk = jnp.tile(qk_gain.astype(F32), (1, ATT_HEADS))
        return pl.pallas_call(
            _proj_kernel, grid=(n // ROW_TILE,),
            in_specs=[row, gspec, per_b, per_b, wspec,
                      pl.BlockSpec((2, HALF), lambda i: (0, 0)),
                      pl.BlockSpec((HALF, HALF), lambda i: (0, 0))],
            out_specs=ospec, out_shape=oshape,
            compiler_params=_cparams(("arbitrary",)), name="norm_proj",
        )(x, gain.reshape(1, D_MODEL), sc, sh, w, qk, bd)
    return pl.pallas_call(
        _proj_res_kernel, grid=(n // ROW_TILE,),
        in_specs=[row, pl.BlockSpec((ROW_TILE * ROW_SUB, LANES), lambda i: (i, 0)), per_b, gspec,
                  per_b, per_b, wspec],
        out_specs=[row, ospec],
        out_shape=[jax.ShapeDtypeStruct((n, D_MODEL), F32), oshape],
        compiler_params=_cparams(("arbitrary",)), name="res_norm_proj",
    )(x, y, gate, gain.reshape(1, D_MODEL), sc, sh, w)


def _ret_tables(seq):
    inv = ROPE_BASE ** (-np.arange(0, RET_DIM, 2, dtype=np.float64) / RET_DIM)
    ang = np.arange(seq, dtype=np.float64)[:, None] * inv[None, :]
    cos = np.concatenate([np.cos(ang), np.cos(ang)], axis=1)
    sin = np.concatenate([-np.sin(ang), np.sin(ang)], axis=1)
    log_g = np.log(1.0 - 2.0 ** (-5.0 - np.arange(RET_HEADS, dtype=np.float64)))
    idx = np.arange(RET_TILE)
    same_or_earlier = (idx[None, :] // CHUNK) <= (idx[:, None] // CHUNK)
    dec = np.exp(log_g[:, None, None] * np.abs(idx[:, None] - idx[None, :])) * same_or_earlier
    loc = idx.astype(np.float64)
    qdec = np.exp(log_g[:, None] * (loc[None, :] + 1.0))
    kdec = np.exp(log_g[:, None] * (RET_TILE - 1.0 - loc[None, :]))
    tdec = np.exp(log_g * RET_TILE)
    qk = np.stack([qdec, kdec], axis=1)[..., None] * np.ones((1, 1, 1, RET_DIM))
    return (jnp.asarray(cos, F32), jnp.asarray(sin, F32), jnp.asarray(dec, F32),
            jnp.asarray(qk, F32), tuple(float(t) for t in tdec))


def _ret_kernel(tdec, q_ref, k_ref, v_ref, g_ref, cos_ref, sin_ref, dec_ref, qk_ref, o_ref, st_ref):
    @pl.when(pl.program_id(1) == 0)
    def _():
        st_ref[...] = jnp.zeros_like(st_ref)

    cos = cos_ref[...]
    sin = sin_ref[...]
    half = RET_DIM // 2
    for h in range(RET_HEADS):
        cols = slice(h * RET_DIM, (h + 1) * RET_DIM)
        q = q_ref[:, cols].astype(F32)
        k = k_ref[:, cols].astype(F32)
        q = q * cos + pltpu.roll(q, half, 1) * sin
        k = (k * cos + pltpu.roll(k, half, 1) * sin) * (RET_DIM ** -0.5)
        v = v_ref[:, cols]
        qb = q.astype(BF16)
        kb = k.astype(BF16)
        s = lax.dot_general(qb, kb, (((1,), (1,)), ((), ())), preferred_element_type=F32)
        s = s * dec_ref[h]
        o = jnp.dot(s.astype(BF16), v, preferred_element_type=F32)
        st = st_ref[h]
        o = o + jnp.dot((q * qk_ref[h, 0]).astype(BF16), st.astype(BF16), preferred_element_type=F32)
        kd = (k * qk_ref[h, 1]).astype(BF16)
        st_ref[h] = st * tdec[h] + lax.dot_general(kd, v, (((0,), (0,)), ((), ())),
                                                   preferred_element_type=F32)
        o = o * lax.rsqrt(jnp.mean(o * o, axis=-1, keepdims=True) + EPS)
        g = g_ref[:, cols].astype(F32)
        o_ref[:, cols] = (o * (g * jax.nn.sigmoid(g))).astype(BF16)


def _retention(proj, bsz, seq):
    n = proj.shape[0]
    nt = seq // RET_TILE
    cos, sin, dec, qk, tdec = _ret_tables(seq)

    def col(c):
        return pl.BlockSpec((RET_TILE, HALF), lambda b, i: (b * nt + i, c))

    pos = pl.BlockSpec((RET_TILE, RET_DIM), lambda b, i: (i, 0))
    return pl.pallas_call(
        functools.partial(_ret_kernel, tdec),
        grid=(bsz, nt),
        in_specs=[col(0), col(1), col(2), col(3), pos, pos,
                  pl.BlockSpec((RET_HEADS, RET_TILE, RET_TILE), lambda b, i: (0, 0, 0)),
                  pl.BlockSpec((RET_HEADS, 2, RET_TILE, RET_DIM), lambda b, i: (0, 0, 0, 0))],
        out_specs=pl.BlockSpec((RET_TILE, HALF), lambda b, i: (b * nt + i, 0)),
        out_shape=jax.ShapeDtypeStruct((n, HALF), BF16),
        scratch_shapes=[pltpu.VMEM((RET_HEADS, RET_DIM, RET_DIM), F32)],
        compiler_params=_cparams(("arbitrary", "arbitrary")), name="retention",
    )(proj, proj, proj, proj, cos, sin, dec, qk)


def _att_bias(rel_bias):
    span = ATT_WIN + ATT_QB
    m = np.concatenate([np.arange(0, ATT_WIN), np.zeros(1, np.int64), np.arange(-(ATT_QB - 1), 0)])
    idx = np.clip((ATT_WIN - ATT_QB) - m, -(CHUNK - 1), REL_CLIP) + (CHUNK - 1)
    vec = rel_bias[:, idx].astype(F32)
    heads = rel_bias.shape[0]
    toep = jnp.tile(vec, (1, ATT_QB))[:, :ATT_QB * (span - 1)].reshape(heads, ATT_QB, span - 1)
    qc = np.arange(ATT_QB)[:, None] // CHUNK
    kc = np.arange(ATT_WIN)[None, :] // CHUNK - ATT_LEFT_CHUNKS
    ok = (kc <= qc) & (kc >= qc - ATT_LEFT_CHUNKS)
    return jnp.where(jnp.asarray(ok)[None], toep[:, :, :ATT_WIN], -jnp.inf)


def _fill_padded(pad_ref, src_ref, front):
    pad_ref[:front, :] = jnp.zeros((front, pad_ref.shape[1]), pad_ref.dtype)
    pad_ref[front:, :] = src_ref[...]


def _att_kernel(q_ref, k_ref, v_ref, bias_ref, o_ref, kpad, vpad):
    i = pl.program_id(1)
    front = ATT_WIN - ATT_QB

    @pl.when(i == 0)
    def _():
        _fill_padded(kpad, k_ref, front)
        _fill_padded(vpad, v_ref, front)

    col = lax.broadcasted_iota(jnp.int32, (1, ATT_WIN), 1)
    first = lax.broadcasted_iota(jnp.int32, (ATT_QB, LANES), 1) < ATT_DIM
    nt = (((1,), (1,)), ((), ()))
    for sub in range(ATT_STEP_BLOCKS):
        rows = slice(sub * ATT_QB, (sub + 1) * ATT_QB)
        start = pl.multiple_of((i * ATT_STEP_BLOCKS + sub) * ATT_QB, ATT_QB)
        pen = jnp.where(col >= front - start, 0.0, -jnp.inf)
        for p in range(ATT_HEADS * ATT_DIM // LANES):
            cols = slice(p * LANES, (p + 1) * LANES)
            q = q_ref[rows, cols]
            kw = kpad[pl.ds(start, ATT_WIN), cols]
            vw = vpad[pl.ds(start, ATT_WIN), cols]
            zero = jnp.zeros_like(q)
            qq = jnp.concatenate([jnp.where(first, q, zero), jnp.where(first, zero, q)], axis=0)
            bias = jnp.concatenate([bias_ref[2 * p], bias_ref[2 * p + 1]], axis=0)
            s = lax.dot_general(qq, kw, nt, preferred_element_type=F32) + bias + pen
            e = jnp.exp(s - jnp.max(s, axis=-1, keepdims=True))
            den = jnp.sum(e, axis=-1, keepdims=True)
            o = jnp.dot(e.astype(BF16), vw, preferred_element_type=F32) / den
            o_ref[rows, cols] = jnp.where(first, o[:ATT_QB], o[ATT_QB:]).astype(BF16)


def _chunk_attention(proj, rel_bias, bsz, seq):
    n = proj.shape[0]
    rows = ATT_STEP_BLOCKS * ATT_QB
    nq = seq // rows
    front = ATT_WIN - ATT_QB
    return pl.pallas_call(
        _att_kernel,
        grid=(bsz, nq),
        in_specs=[pl.BlockSpec((rows, HALF), lambda b, i: (b * nq + i, 4)),
                  pl.BlockSpec((seq, HALF), lambda b, i: (b, 5)),
                  pl.BlockSpec((seq, HALF), lambda b, i: (b, 6)),
                  pl.BlockSpec((ATT_HEADS, ATT_QB, ATT_WIN), lambda b, i: (0, 0, 0))],
        out_specs=pl.BlockSpec((rows, HALF), lambda b, i: (b * nq + i, 0)),
        out_shape=jax.ShapeDtypeStruct((n, HALF), BF16),
        scratch_shapes=[pltpu.VMEM((front + seq, HALF), BF16), pltpu.VMEM((front + seq, HALF), BF16)],
        compiler_params=_cparams(("arbitrary", "arbitrary")), name="chunk_attention",
    )(proj, proj, proj, _att_bias(rel_bias))


def _sb_logs(z):
    log_beta = jnp.minimum(z, 0.0) - jnp.log(1.0 + jnp.exp(-jnp.abs(z)))
    return log_beta, log_beta - z


def _sb_kernel(q_ref, k_ref, v_ref, tri_ref, o_ref, kpad, vpad, run_ref, acc_ref):
    front = SB_WIN - SB_ROWS
    nblk = SB_WIN // SB_BLK
    pairs = HALF // LANES
    _fill_padded(kpad, k_ref, front)
    _fill_padded(vpad, v_ref, front)
    row = lax.broadcasted_iota(jnp.int32, (SB_ROWS, SB_WIN), 0)
    col = lax.broadcasted_iota(jnp.int32, (SB_ROWS, SB_WIN), 1)
    causal_col = jnp.where(col - row < front, col, -1)
    first = lax.broadcasted_iota(jnp.int32, (SB_ROWS, LANES), 1) < SB_DIM
    nt = (((1,), (1,)), ((), ()))

    def first_window(iq, u):
        start = pl.multiple_of(iq * SB_ROWS, SB_ROWS)
        mask = causal_col >= jnp.maximum(front - start, 0)
        tri = tri_ref[...]

        def heads_of(p):
            q = q_ref[pl.ds(start, SB_ROWS), p * LANES:(p + 1) * LANES]
            zero = jnp.zeros_like(q)
            return jnp.where(first, q, zero), jnp.where(first, zero, q)

        alive = None
        mask2 = jnp.concatenate([mask, mask], axis=0)
        lbs, l1s, vws = [], [], []
        for p in range(pairs):
            cols = slice(p * LANES, (p + 1) * LANES)
            kw = kpad[pl.ds(start, SB_WIN), cols]
            vws.append(vpad[pl.ds(start, SB_WIN), cols])
            qq = jnp.concatenate(heads_of(p), axis=0)
            z = lax.dot_general(qq, kw, nt, preferred_element_type=F32)
            log_beta, log_1mb = _sb_logs(z)
            lbs.append(log_beta)
            l1 = jnp.where(mask2, log_1mb, 0.0).astype(BF16)
            l1s += [l1[hh * SB_ROWS:(hh + 1) * SB_ROWS, c * SB_BLK:(c + 1) * SB_BLK]
                    for hh in range(2) for c in range(nblk)]
        cs = jnp.dot(jnp.concatenate(l1s, axis=0), tri, preferred_element_type=F32)
        for p in range(pairs):
            rows = []
            for hh in range(2):
                h = 2 * p + hh
                run = jnp.zeros((SB_ROWS, SB_BLK), F32)
                accs = [None] * nblk
                for c in reversed(range(nblk)):
                    blk = cs[(h * nblk + c) * SB_ROWS:(h * nblk + c + 1) * SB_ROWS]
                    accs[c] = blk[:, :SB_BLK] + run
                    run = run + blk[:, SB_BLK:]
                rows.append(jnp.concatenate(accs, axis=1))
                run_ref[u, h] = run
                alive = run if alive is None else jnp.maximum(alive, run)
            a = jnp.where(mask2, jnp.exp(lbs[p] + jnp.concatenate(rows, axis=0)), 0.0)
            o = jnp.dot(a.astype(BF16), vws[p], preferred_element_type=F32)
            acc_ref[u, 2 * p] = o[:SB_ROWS]
            acc_ref[u, 2 * p + 1] = o[SB_ROWS:]
        return jnp.max(alive)

    def older_keys(iq, u, alive0):
        start = pl.multiple_of(iq * SB_ROWS, SB_ROWS)
        tri = tri_ref[...]

        def heads_of(p):
            q = q_ref[pl.ds(start, SB_ROWS), p * LANES:(p + 1) * LANES]
            zero = jnp.zeros_like(q)
            return jnp.where(first, q, zero), jnp.where(first, zero, q)

        def cond(c):
            return jnp.logical_and(c[0] >= 0, c[1] > SB_EXIT)

        def body(c):
            j = c[0]
            kstart = pl.multiple_of(front + j * SB_ROWS, SB_ROWS)
            alive = None
            for p in range(pairs):
                cols = slice(p * LANES, (p + 1) * LANES)
                kb = kpad[pl.ds(kstart, SB_ROWS), cols]
                vb = vpad[pl.ds(kstart, SB_ROWS), cols]
                for hh, qm in enumerate(heads_of(p)):
                    z = lax.dot_general(qm, kb, nt, preferred_element_type=F32)
                    log_beta, log_1mb = _sb_logs(z)
                    cs = jnp.dot(log_1mb.astype(BF16), tri[:SB_ROWS], preferred_element_type=F32)
                    run = run_ref[u, 2 * p + hh]
                    a = jnp.exp(log_beta + run[:, :SB_ROWS] + cs[:, :SB_ROWS])
                    acc_ref[u, 2 * p + hh] += jnp.dot(a.astype(BF16), vb, preferred_element_type=F32)
                    run = run + cs[:, SB_BLK:]
                    run_ref[u, 2 * p + hh] = run
                    alive = run if alive is None else jnp.maximum(alive, run)
            return j - 1, jnp.max(alive)

        lax.while_loop(cond, body, (iq - SB_WIN // SB_ROWS, alive0))
        for p in range(pairs):
            o_ref[pl.ds(start, SB_ROWS), p * LANES:(p + 1) * LANES] = jnp.where(
                first, acc_ref[u, 2 * p], acc_ref[u, 2 * p + 1]).astype(BF16)

    def query_blocks(it, carry):
        alive = [first_window(it * SB_UNROLL + u, u) for u in range(SB_UNROLL)]
        for u in range(SB_UNROLL):
            older_keys(it * SB_UNROLL + u, u, alive[u])
        return carry

    lax.fori_loop(0, q_ref.shape[0] // (SB_ROWS * SB_UNROLL), query_blocks, 0)


def _stick_breaking(proj, bsz, seq):
    n = proj.shape[0]
    groups = SB_HEADS * SB_DIM // HALF
    front = SB_WIN - SB_ROWS
    j = np.arange(SB_BLK)
    tri = np.concatenate([(j[:, None] > j[None, :]).astype(np.float32),
                          np.ones((SB_BLK, SB_BLK), np.float32)], axis=1)
    return pl.pallas_call(
        _sb_kernel,
        grid=(bsz, groups),
        in_specs=[pl.BlockSpec((seq, HALF), lambda b, g: (b, g)),
                  pl.BlockSpec((seq, HALF), lambda b, g: (b, groups + g)),
                  pl.BlockSpec((seq, HALF), lambda b, g: (b, 2 * groups + g)),
                  pl.BlockSpec((SB_BLK, 2 * SB_BLK), lambda b, g: (0, 0))],
        out_specs=pl.BlockSpec((seq, HALF), lambda b, g: (b, g)),
        out_shape=jax.ShapeDtypeStruct((n, SB_HEADS * SB_DIM), BF16),
        scratch_shapes=[pltpu.VMEM((front + seq, HALF), BF16), pltpu.VMEM((front + seq, HALF), BF16),
                        pltpu.VMEM((SB_UNROLL, HALF // SB_DIM, SB_ROWS, SB_BLK), F32),
                        pltpu.VMEM((SB_UNROLL, HALF // SB_DIM, SB_ROWS, LANES), F32)],
        compiler_params=_cparams(("arbitrary", "arbitrary")), name="stick_breaking",
    )(proj, proj, proj, jnp.asarray(tri, BF16))


def _out_route_kernel(ma_ref, mb_ref, wa_ref, wb_ref, x_ref, g1_ref, g_ref, sc_ref, sh_ref,
                      rw_ref, rb_ref, tri_ref, xo_ref, he_ref, rt_ref, cnt_ref, carry_ref):
    @pl.when(pl.program_id(0) == 0)
    def _():
        carry_ref[...] = jnp.zeros_like(carry_ref)

    tm = x_ref.shape[0]
    mix = (jnp.dot(ma_ref[...], wa_ref[...], preferred_element_type=F32)
           + jnp.dot(mb_ref[...], wb_ref[...], preferred_element_type=F32))
    x = x_ref[...] + g1_ref[0] * mix
    xo_ref[...] = x
    h = _norm_mod(x, g_ref[...], sc_ref[0], sh_ref[0])
    for jt in range(ROW_SUB):
        he_ref[pl.ds(jt, tm, stride=ROW_SUB), :] = h[:, jt * LANES:(jt + 1) * LANES]

    nt = (((1,), (1,)), ((), ()))
    h_hi = h.astype(BF16)
    h_lo = (h - h_hi.astype(F32)).astype(BF16)
    w = rw_ref[...]
    w_hi = w.astype(BF16)
    w_lo = (w - w_hi.astype(F32)).astype(BF16)
    logits = (lax.dot_general(w_hi, h_hi, nt, preferred_element_type=F32)
              + lax.dot_general(w_hi, h_lo, nt, preferred_element_type=F32)
              + lax.dot_general(w_lo, h_hi, nt, preferred_element_type=F32))
    sel = jax.nn.sigmoid(logits) + rb_ref[...]

    best = None
    for g in range(N_GROUPS):
        for p in range(len(PAIR_A)):
            ea, eb = 4 * g + PAIR_A[p], 4 * g + PAIR_B[p]
            val = sel[ea:ea + 1, :] + sel[eb:eb + 1, :]
            cid = jnp.full((1, tm), float(len(PAIR_A) * g + p), F32)
            cand = (val, cid)
            if best is None:
                best = cand
            else:
                take = val > best[0]
                best = tuple(jnp.where(take, c_, b_) for c_, b_ in zip(cand, best))
    cls = best[1]

    crow = lax.broadcasted_iota(jnp.int32, (CLASS_ROWS, tm), 0).astype(F32)
    onehot = jnp.where(crow == cls, 1.0, 0.0)
    before = jnp.dot(onehot.astype(BF16), tri_ref[...], preferred_element_type=F32)
    carry = carry_ref[...]
    rank = jnp.sum(onehot * (before + carry[:, :1]), axis=0, keepdims=True)
    carry = carry + jnp.sum(onehot, axis=1, keepdims=True)
    carry_ref[...] = carry
    cnt_ref[0] = carry

    srow = lax.broadcasted_iota(jnp.int32, (8, tm), 0)
    rt_ref[0] = jnp.where(srow == 0, cls, jnp.where(srow == 1, rank, 0.0))


def _out_route(mix_a, mix_b, col_a, col_b, w_out, x, g1, gain, sc, sh, router_w, router_b, seq):
    n = x.shape[0]
    tm = ROW_TILE
    nt = n // tm
    tpb = seq // tm
    t = np.arange(tm)
    tri = jnp.asarray((t[:, None] < t[None, :]).astype(np.float32), BF16)
    row = pl.BlockSpec((tm, D_MODEL), lambda i: (i, 0))
    per_b = pl.BlockSpec((1, 1, D_MODEL), lambda i: (i // tpb, 0, 0))
    rb = jnp.broadcast_to(router_b.astype(F32)[:, None], (N_EXPERTS, tm))
    return pl.pallas_call(
        _out_route_kernel,
        grid=(nt,),
        in_specs=[pl.BlockSpec((tm, HALF), lambda i: (i, col_a)),
                  pl.BlockSpec((tm, HALF), lambda i: (i, col_b)),
                  pl.BlockSpec((HALF, D_MODEL), lambda i: (0, 0)),
                  pl.BlockSpec((HALF, D_MODEL), lambda i: (1, 0)),
                  row, per_b,
                  pl.BlockSpec((1, D_MODEL), lambda i: (0, 0)), per_b, per_b,
                  pl.BlockSpec((N_EXPERTS, D_MODEL), lambda i: (0, 0)),
                  pl.BlockSpec((N_EXPERTS, tm), lambda i: (0, 0)),
                  pl.BlockSpec((tm, tm), lambda i: (0, 0))],
        out_specs=[row,
                   pl.BlockSpec((tm * ROW_SUB, LANES), lambda i: (i, 0)),
                   pl.BlockSpec((1, 8, tm), lambda i: (i, 0, 0)),
                   pl.BlockSpec((1, CLASS_ROWS, LANES), lambda i: (i, 0, 0))],
        out_shape=[jax.ShapeDtypeStruct((n, D_MODEL), F32),
                   jax.ShapeDtypeStruct((n * ROW_SUB, LANES), F32),
                   jax.ShapeDtypeStruct((nt, 8, tm), F32),
                   jax.ShapeDtypeStruct((nt, CLASS_ROWS, LANES), F32)],
        scratch_shapes=[pltpu.VMEM((CLASS_ROWS, LANES), F32)],
        compiler_params=_cparams(("arbitrary",)), name="out_proj_route",
    )(mix_a, mix_b, w_out, w_out, x, g1, gain.reshape(1, D_MODEL), sc, sh,
      router_w.T.astype(F32), rb, tri)


def _fill_slots(pos_ref, nv_ref, inv_ref):
    batch = 16

    def fill_tile(j, c):
        def fill(g, c2):
            for u in range(batch):
                inv_ref[j * MOE_TILE + g * batch + u] = 0
            return c2

        return lax.fori_loop(nv_ref[j] // batch, MOE_TILE // batch, fill, c)

    lax.fori_loop(0, nv_ref.shape[0], fill_tile, 0)

    def place(g, c):
        slots = [pos_ref[g * batch + u] for u in range(batch)]
        for u in range(batch):
            inv_ref[slots[u]] = g * batch + u
        return c

    lax.fori_loop(0, pos_ref.shape[0] // batch, place, 0)


def _routing_plan(route, counts, n):
    nt, _, tm = route.shape
    cls = route[:, 0, :].reshape(n).astype(jnp.int32)
    rank = route[:, 1, :].reshape(n).astype(jnp.int32)
    cnt = counts[-1, :N_CLASS, 0].astype(jnp.int32)
    tiles_c = (cnt + MOE_TILE - 1) // MOE_TILE
    ends = jnp.cumsum(tiles_c)
    starts = ends - tiles_c
    pos = starts[cls] * MOE_TILE + rank
    max_tiles = n // MOE_TILE + N_CLASS
    total = ends[-1]
    j = jnp.arange(max_tiles, dtype=jnp.int32)
    jj = jnp.minimum(j, total - 1)
    tcls = jnp.sum((ends[None, :] <= jj[:, None]).astype(jnp.int32), axis=1)
    grp = tcls // len(PAIR_A)
    pair = tcls % len(PAIR_A)
    ea = 4 * grp + jnp.asarray(PAIR_A, jnp.int32)[pair]
    eb = 4 * grp + jnp.asarray(PAIR_B, jnp.int32)[pair]
    valid_rows = jnp.clip(cnt[tcls] - (jj - starts[tcls]) * MOE_TILE, 0, MOE_TILE)
    valid_rows = jnp.where(j < total, valid_rows, 0).astype(jnp.int32)
    return pos, ea, eb, valid_rows, total.reshape(1)


def _moe_kernel(pos_ref, ea_ref, eb_ref, nv_ref, tot_ref, h_hbm, rw_ref, wg_a, wu_a, wd_a,
                wg_b, wu_b, wd_b, y_hbm, gbuf, ybuf, inv_ref, gsem, ssem):
    j = pl.program_id(0)
    total = tot_ref[0]
    gslot = j % GATHER_BUFS
    yslot = j % 2
    unroll = 8

    def gather_row(tile, s, r, prio):
        src = pl.multiple_of(inv_ref[tile * MOE_TILE + r] * ROW_SUB, ROW_SUB)
        dst = pl.multiple_of(r * ROW_SUB, ROW_SUB)
        pltpu.make_async_copy(h_hbm.at[pl.ds(src, ROW_SUB), :], gbuf.at[s, pl.ds(dst, ROW_SUB), :],
                              gsem.at[s]).start(priority=prio)

    def scatter_row(tile, s, r, prio):
        src = pl.multiple_of(r * ROW_SUB, ROW_SUB)
        dst = pl.multiple_of(inv_ref[tile * MOE_TILE + r] * ROW_SUB, ROW_SUB)
        pltpu.make_async_copy(ybuf.at[s, pl.ds(src, ROW_SUB), :], y_hbm.at[pl.ds(dst, ROW_SUB), :],
                              ssem.at[s]).start(priority=prio)

    def gathered_rows(tile):
        return (nv_ref[tile] + unroll - 1) // unroll * unroll

    def gather_rows(tile, s):
        def group(g, c):
            for u in range(unroll):
                gather_row(tile, s, g * unroll + u, u % 2)
            return c

        lax.fori_loop(0, gathered_rows(tile) // unroll, group, 0)

    def wait_rows(count, make_copy):
        for bit in range(MOE_TILE.bit_length()):
            @pl.when((count >> bit) & 1 == 1)
            def _():
                make_copy((1 << bit) * ROW_SUB).wait()

    def gather_wait(tile, s):
        wait_rows(gathered_rows(tile), lambda size: pltpu.make_async_copy(
            h_hbm.at[pl.ds(0, size), :], gbuf.at[s, pl.ds(0, size), :], gsem.at[s]))

    def scatter_rows(tile, s):
        nv = nv_ref[tile]
        groups = nv // unroll

        def group(g, c):
            for u in range(unroll):
                scatter_row(tile, s, g * unroll + u, u % 2)
            return c

        lax.fori_loop(0, groups, group, 0)
        for u in range(unroll):
            @pl.when(groups * unroll + u < nv)
            def _():
                scatter_row(tile, s, groups * unroll + u, u % 2)

    def scatter_wait(tile, s):
        wait_rows(nv_ref[tile], lambda size: pltpu.make_async_copy(
            ybuf.at[s, pl.ds(0, size), :], y_hbm.at[pl.ds(0, size), :], ssem.at[s]))

    @pl.when(j == 0)
    def _():
        _fill_slots(pos_ref, nv_ref, inv_ref)
        gbuf[...] = jnp.zeros_like(gbuf)
        for t in range(GATHER_BUFS - 1):
            @pl.when(t < total)
            def _():
                gather_rows(t, t)

    @pl.when(j < total)
    def _():
        gather_wait(j, gslot)
        x = jnp.concatenate([gbuf[gslot, pl.ds(jt, MOE_TILE, stride=ROW_SUB), :]
                             for jt in range(ROW_SUB)], axis=1)
        xb = x.astype(BF16)
        scores = [jax.nn.sigmoid(jnp.sum(x * rw_ref[pl.ds(e_ref[j], 1), :], axis=-1, keepdims=True))
                  for e_ref in (ea_ref, eb_ref)]
        shares = [sc_ / (scores[0] + scores[1]) for sc_ in scores]
        y = None
        for lane, (wg, wu, wd) in enumerate(((wg_a, wu_a, wd_a), (wg_b, wu_b, wd_b))):
            g = jnp.dot(xb, wg[0].astype(BF16), preferred_element_type=F32)
            u = jnp.dot(xb, wu[0].astype(BF16), preferred_element_type=F32)
            he = ((g * jax.nn.sigmoid(g)) * u * shares[lane]).astype(BF16)
            part = jnp.dot(he, wd[0].astype(BF16), preferred_element_type=F32)
            y = part if y is None else y + part

        @pl.when(j >= 2)
        def _():
            scatter_wait(j - 2, yslot)

        for jt in range(ROW_SUB):
            chunk = y[:, jt * LANES:(jt + 1) * LANES]
            ybuf[yslot, pl.ds(jt, MOE_TILE, stride=ROW_SUB), :] = chunk

        scatter_rows(j, yslot)
        ahead = j + GATHER_BUFS - 1

        @pl.when(ahead < total)
        def _():
            gather_rows(ahead, ahead % GATHER_BUFS)

        @pl.when(j == total - 1)
        def _():
            @pl.when(j >= 1)
            def _():
                scatter_wait(j - 1, 1 - yslot)

            scatter_wait(j, yslot)


def _moe(hext, pos, ea, eb, valid_rows, total, router_wt, layer, w_gate, w_up, w_down):
    n = hext.shape[0] // ROW_SUB
    max_tiles = ea.shape[0]

    def wspec(shape, which):
        if which == 0:
            return pl.BlockSpec((None, 1) + shape,
                                lambda j, inv_, ea_, eb_, nv_, t_: (layer, ea_[j], 0, 0))
        return pl.BlockSpec((None, 1) + shape, lambda j, inv_, ea_, eb_, nv_, t_: (layer, eb_[j], 0, 0))

    up = (D_MODEL, D_EXPERT)
    down = (D_EXPERT, D_MODEL)
    grid_spec = pltpu.PrefetchScalarGridSpec(
        num_scalar_prefetch=5,
        grid=(max_tiles,),
        in_specs=[pl.BlockSpec(memory_space=pl.ANY),
                  pl.BlockSpec((N_EXPERTS, D_MODEL), lambda j, inv_, ea_, eb_, nv_, t_: (0, 0)),
                  wspec(up, 0), wspec(up, 0), wspec(down, 0),
                  wspec(up, 1), wspec(up, 1), wspec(down, 1)],
        out_specs=pl.BlockSpec(memory_space=pl.ANY),
        scratch_shapes=[pltpu.VMEM((GATHER_BUFS, MOE_TILE * ROW_SUB, LANES), F32),
                        pltpu.VMEM((2, MOE_TILE * ROW_SUB, LANES), F32),
                        pltpu.SMEM((max_tiles * MOE_TILE,), jnp.int32),
                        pltpu.SemaphoreType.DMA((GATHER_BUFS,)),
                        pltpu.SemaphoreType.DMA((2,))],
    )
    return pl.pallas_call(
        _moe_kernel,
        grid_spec=grid_spec,
        out_shape=jax.ShapeDtypeStruct((n * ROW_SUB, LANES), F32),
        compiler_params=_cparams(("arbitrary",)), name="moe_experts",
    )(pos, ea, eb, valid_rows, total, hext, router_wt, w_gate, w_up, w_down, w_gate, w_up, w_down)


def _final_kernel(x_ref, y_ref, g_ref, o_ref):
    o_ref[...] = x_ref[...] + g_ref[0] * _token_rows(y_ref)


def _final(x, y, gate, seq):
    n = x.shape[0]
    tpb = seq // ROW_TILE
    row = pl.BlockSpec((ROW_TILE, D_MODEL), lambda i: (i, 0))
    return pl.pallas_call(
        _final_kernel, grid=(n // ROW_TILE,),
        in_specs=[row, pl.BlockSpec((ROW_TILE * ROW_SUB, LANES), lambda i: (i, 0)),
                  pl.BlockSpec((1, 1, D_MODEL), lambda i: (i // tpb, 0, 0))],
        out_specs=row, out_shape=jax.ShapeDtypeStruct((n, D_MODEL), F32),
        compiler_params=_cparams(("arbitrary",)), name="final_residual",
    )(x, y, gate)


def kernel(x, c, ada_w, ada_b, norm1_g, norm2_g, even_w_in, even_w_out, att_q_norm_g, att_k_norm_g,
           att_rel_bias, odd_w_in, odd_w_out, router_w, router_b, exp_w_gate, exp_w_up, exp_w_down):
    bsz, seq, d = x.shape
    n = bsz * seq
    mod = _modulation(c, ada_w, ada_b)
    mods = [[mod[l, :, k * d:(k + 1) * d].reshape(bsz, 1, d) for k in range(6)] for l in range(2)]
    xf = x.reshape(n, d)
    w_gate, w_up, w_down = exp_w_gate, exp_w_up, exp_w_down

    def moe_layer(layer, mix_a, mix_b, col_a, col_b, w_out, xin):
        sh1, sc1, g1, sh2, sc2, g2 = mods[layer]
        x1, hext, route, counts = _out_route(mix_a, mix_b, col_a, col_b, w_out.astype(BF16), xin, g1,
                                             norm2_g[layer], sc2, sh2, router_w, router_b, seq)
        pos, ea, eb, valid_rows, total = _routing_plan(route, counts, n)
        y = _moe(hext, pos, ea, eb, valid_rows, total, router_w.T.astype(F32), layer,
                 w_gate, w_up, w_down)
        return x1, y, g2

    sh1, sc1, _, _, _, _ = mods[0]
    qk_gain = jnp.stack([att_q_norm_g[0], att_k_norm_g[0]])
    proj0 = _project(xf, norm1_g[0], sc1, sh1, even_w_in[0].astype(BF16), seq, qk_gain=qk_gain)
    ret = _retention(proj0, bsz, seq)
    att = _chunk_attention(proj0, att_rel_bias[0], bsz, seq)
    x1, y0, g2_0 = moe_layer(0, ret, att, 0, 0, even_w_out[0], xf)

    sh1, sc1, _, _, _, _ = mods[1]
    x2, proj1 = _project(x1, norm1_g[1], sc1, sh1, odd_w_in[0].astype(BF16), seq, y=y0, gate=g2_0)
    sbo = _stick_breaking(proj1, bsz, seq)
    x3, y1, g2_1 = moe_layer(1, sbo, sbo, 0, 1, odd_w_out[0], x2)
    return _final(x3, y1, g2_1, seq).reshape(bsz, seq, d)
```

```python
import functools

import numpy as np
import jax
import jax.numpy as jnp
from jax import lax
from jax.experimental import pallas as pl
from jax.experimental.pallas import tpu as pltpu

F32 = jnp.float32
BF16 = jnp.bfloat16

D_MODEL = 1024
CHUNK = 64
EPS = 1e-6
ROPE_BASE = 10000.0
RET_HEADS = 4
RET_DIM = 128
ATT_HEADS = 8
ATT_DIM = 64
ATT_LEFT_CHUNKS = 8
REL_CLIP = 128
SB_HEADS = 16
SB_DIM = 64
N_EXPERTS = 16
N_GROUPS = 4
D_EXPERT = 512
HALF = 512

VMEM_LIMIT = 48 * 1024 * 1024
LANES = 128

ROW_TILE = 1024
RET_TILE = 256
ATT_QB = 256
ATT_WIN = ATT_LEFT_CHUNKS * CHUNK + ATT_QB
ATT_STEP_BLOCKS = 4
SB_BLK = 128
SB_ROWS = 64
SB_UNROLL = 8
SB_WIN = 3 * SB_BLK
SB_EXIT = -110.0

PAIR_A = (0, 0, 0, 1, 1, 2)
PAIR_B = (1, 2, 3, 2, 3, 3)
N_CLASS = N_GROUPS * len(PAIR_A)
CLASS_ROWS = 32
MOE_TILE = 256
ROW_SUB = D_MODEL // LANES
GATHER_BUFS = 3


def _cparams(sem):
    return pltpu.CompilerParams(dimension_semantics=sem, vmem_limit_bytes=VMEM_LIMIT)


def _mod_kernel(c_ref, w_ref, b_ref, o_ref):
    c = c_ref[...]
    ca = c * jax.nn.sigmoid(c)
    o_ref[0] = jnp.dot(ca.astype(BF16), w_ref[0].astype(BF16), preferred_element_type=F32) + b_ref[0]


def _modulation(c, ada_w, ada_b):
    depth, _, width = ada_w.shape
    bsz = c.shape[0]
    rows = 8
    cp = jnp.zeros((rows, D_MODEL), F32).at[:bsz].set(c)
    tn = 1536
    out = pl.pallas_call(
        _mod_kernel,
        grid=(depth, width // tn),
        in_specs=[pl.BlockSpec((rows, D_MODEL), lambda l, j: (0, 0)),
                  pl.BlockSpec((1, D_MODEL, tn), lambda l, j: (l, 0, j)),
                  pl.BlockSpec((1, 1, tn), lambda l, j: (l, 0, j))],
        out_specs=pl.BlockSpec((1, rows, tn), lambda l, j: (l, 0, j)),
        out_shape=jax.ShapeDtypeStruct((depth, rows, width), F32),
        compiler_params=_cparams(("arbitrary", "arbitrary")),
        name="adaln_mod",
    )(cp, ada_w, ada_b.reshape(depth, 1, width))
    return out[:, :bsz]


def _norm_mod(x, g, sc, sh):
    ms = jnp.mean(x * x, axis=-1, keepdims=True)
    return (x * lax.rsqrt(ms + EPS) * g) * (1.0 + sc) + sh


def _project_chunks(h, w_ref, o_ref, qk_ref, bd_ref, q_scale):
    for ci, n0 in enumerate(range(0, o_ref.shape[1], HALF)):
        p = jnp.dot(h, w_ref[:, n0:n0 + HALF], preferred_element_type=F32)
        if qk_ref is not None and ci in (4, 5):
            ms = jnp.dot((p * p).astype(BF16), bd_ref[...], preferred_element_type=F32)
            p = p * lax.rsqrt(ms + EPS) * qk_ref[ci - 4:ci - 3, :]
            if ci == 4:
                p = p * q_scale
        elif qk_ref is None and n0 < SB_HEADS * SB_DIM:
            p = p * q_scale
        o_ref[:, n0:n0 + HALF] = p.astype(BF16)


def _proj_kernel(x_ref, g_ref, sc_ref, sh_ref, w_ref, qk_ref, bd_ref, o_ref):
    h = _norm_mod(x_ref[...], g_ref[...], sc_ref[0], sh_ref[0]).astype(BF16)
    _project_chunks(h, w_ref, o_ref, qk_ref, bd_ref, ATT_DIM ** -0.5)


def _token_rows(y_ref):
    rows = y_ref.shape[0] // ROW_SUB
    return jnp.concatenate([y_ref[pl.ds(jt, rows, stride=ROW_SUB), :] for jt in range(ROW_SUB)],
                           axis=1)


def _proj_res_kernel(x_ref, y_ref, gt_ref, g_ref, sc_ref, sh_ref, w_ref, xo_ref, o_ref):
    x = x_ref[...] + gt_ref[0] * _token_rows(y_ref)
    xo_ref[...] = x
    h = _norm_mod(x, g_ref[...], sc_ref[0], sh_ref[0]).astype(BF16)
    _project_chunks(h, w_ref, o_ref, None, None, SB_DIM ** -0.5)


def _project(x, gain, sc, sh, w, seq, qk_gain=None, y=None, gate=None):
    n = x.shape[0]
    nout = w.shape[1]
    tpb = seq // ROW_TILE
    row = pl.BlockSpec((ROW_TILE, D_MODEL), lambda i: (i, 0))
    per_b = pl.BlockSpec((1, 1, D_MODEL), lambda i: (i // tpb, 0, 0))
    gspec = pl.BlockSpec((1, D_MODEL), lambda i: (0, 0))
    wspec = pl.BlockSpec((D_MODEL, nout), lambda i: (0, 0))
    ospec = pl.BlockSpec((ROW_TILE, nout), lambda i: (i, 0))
    oshape = jax.ShapeDtypeStruct((n, nout), BF16)
    if y is None:
        head = np.arange(HALF) // ATT_DIM
        bd = jnp.asarray((head[:, None] == head[None, :]).astype(np.float32) / ATT_DIM, BF16)
        qk = jnp.tile(qk_gain.astype(F32), (1, ATT_HEADS))
        return pl.pallas_call(
            _proj_kernel, grid=(n // ROW_TILE,),
            in_specs=[row, gspec, per_b, per_b, wspec,
                      pl.BlockSpec((2, HALF), lambda i: (0, 0)),
                      pl.BlockSpec((HALF, HALF), lambda i: (0, 0))],
            out_specs=ospec, out_shape=oshape,
            compiler_params=_cparams(("arbitrary",)), name="norm_proj",
        )(x, gain.reshape(1, D_MODEL), sc, sh, w, qk, bd)
    return pl.pallas_call(
        _proj_res_kernel, grid=(n // ROW_TILE,),
        in_specs=[row, pl.BlockSpec((ROW_TILE * ROW_SUB, LANES), lambda i: (i, 0)), per_b, gspec,
                  per_b, per_b, wspec],
        out_specs=[row, ospec],
        out_shape=[jax.ShapeDtypeStruct((n, D_MODEL), F32), oshape],
        compiler_params=_cparams(("arbitrary",)), name="res_norm_proj",
    )(x, y, gate, gain.reshape(1, D_MODEL), sc, sh, w)


def _ret_tables(seq):
    inv = ROPE_BASE ** (-np.arange(0, RET_DIM, 2, dtype=np.float64) / RET_DIM)
    ang = np.arange(seq, dtype=np.float64)[:, None] * inv[None, :]
    cos = np.concatenate([np.cos(ang), np.cos(ang)], axis=1)
    sin = np.concatenate([-np.sin(ang), np.sin(ang)], axis=1)
    log_g = np.log(1.0 - 2.0 ** (-5.0 - np.arange(RET_HEADS, dtype=np.float64)))
    idx = np.arange(RET_TILE)
    same_or_earlier = (idx[None, :] // CHUNK) <= (idx[:, None] // CHUNK)
    dec = np.exp(log_g[:, None, None] * np.abs(idx[:, None] - idx[None, :])) * same_or_earlier
    loc = idx.astype(np.float64)
    qdec = np.exp(log_g[:, None] * (loc[None, :] + 1.0))
    kdec = np.exp(log_g[:, None] * (RET_TILE - 1.0 - loc[None, :]))
    tdec = np.exp(log_g * RET_TILE)
    qk = np.stack([qdec, kdec], axis=1)[..., None] * np.ones((1, 1, 1, RET_DIM))
    return (jnp.asarray(cos, F32), jnp.asarray(sin, F32), jnp.asarray(dec, F32),
            jnp.asarray(qk, F32), tuple(float(t) for t in tdec))


def _ret_kernel(tdec, q_ref, k_ref, v_ref, g_ref, cos_ref, sin_ref, dec_ref, qk_ref, o_ref, st_ref):
    @pl.when(pl.program_id(1) == 0)
    def _():
        st_ref[...] = jnp.zeros_like(st_ref)

    cos = cos_ref[...]
    sin = sin_ref[...]
    half = RET_DIM // 2
    for h in range(RET_HEADS):
        cols = slice(h * RET_DIM, (h + 1) * RET_DIM)
        q = q_ref[:, cols].astype(F32)
        k = k_ref[:, cols].astype(F32)
        q = q * cos + pltpu.roll(q, half, 1) * sin
        k = (k * cos + pltpu.roll(k, half, 1) * sin) * (RET_DIM ** -0.5)
        v = v_ref[:, cols]
        qb = q.astype(BF16)
        kb = k.astype(BF16)
        s = lax.dot_general(qb, kb, (((1,), (1,)), ((), ())), preferred_element_type=F32)
        s = s * dec_ref[h]
        o = jnp.dot(s.astype(BF16), v, preferred_element_type=F32)
        st = st_ref[h]
        o = o + jnp.dot((q * qk_ref[h, 0]).astype(BF16), st.astype(BF16), preferred_element_type=F32)
        kd = (k * qk_ref[h, 1]).astype(BF16)
        st_ref[h] = st * tdec[h] + lax.dot_general(kd, v, (((0,), (0,)), ((), ())),
                                                   preferred_element_type=F32)
        o = o * lax.rsqrt(jnp.mean(o * o, axis=-1, keepdims=True) + EPS)
        g = g_ref[:, cols].astype(F32)
        o_ref[:, cols] = (o * (g * jax.nn.sigmoid(g))).astype(BF16)


def _retention(proj, bsz, seq):
    n = proj.shape[0]
    nt = seq // RET_TILE
    cos, sin, dec, qk, tdec = _ret_tables(seq)

    def col(c):
        return pl.BlockSpec((RET_TILE, HALF), lambda b, i: (b * nt + i, c))

    pos = pl.BlockSpec((RET_TILE, RET_DIM), lambda b, i: (i, 0))
    return pl.pallas_call(
        functools.partial(_ret_kernel, tdec),
        grid=(bsz, nt),
        in_specs=[col(0), col(1), col(2), col(3), pos, pos,
                  pl.BlockSpec((RET_HEADS, RET_TILE, RET_TILE), lambda b, i: (0, 0, 0)),
                  pl.BlockSpec((RET_HEADS, 2, RET_TILE, RET_DIM), lambda b, i: (0, 0, 0, 0))],
        out_specs=pl.BlockSpec((RET_TILE, HALF), lambda b, i: (b * nt + i, 0)),
        out_shape=jax.ShapeDtypeStruct((n, HALF), BF16),
        scratch_shapes=[pltpu.VMEM((RET_HEADS, RET_DIM, RET_DIM), F32)],
        compiler_params=_cparams(("arbitrary", "arbitrary")), name="retention",
    )(proj, proj, proj, proj, cos, sin, dec, qk)


def _att_bias(rel_bias):
    span = ATT_WIN + ATT_QB
    m = np.concatenate([np.arange(0, ATT_WIN), np.zeros(1, np.int64), np.arange(-(ATT_QB - 1), 0)])
    idx = np.clip((ATT_WIN - ATT_QB) - m, -(CHUNK - 1), REL_CLIP) + (CHUNK - 1)
    vec = rel_bias[:, idx].astype(F32)
    heads = rel_bias.shape[0]
    toep = jnp.tile(vec, (1, ATT_QB))[:, :ATT_QB * (span - 1)].reshape(heads, ATT_QB, span - 1)
    qc = np.arange(ATT_QB)[:, None] // CHUNK
    kc = np.arange(ATT_WIN)[None, :] // CHUNK - ATT_LEFT_CHUNKS
    ok = (kc <= qc) & (kc >= qc - ATT_LEFT_CHUNKS)
    return jnp.where(jnp.asarray(ok)[None], toep[:, :, :ATT_WIN], -jnp.inf)


def _fill_padded(pad_ref, src_ref, front):
    pad_ref[:front, :] = jnp.zeros((front, pad_ref.shape[1]), pad_ref.dtype)
    pad_ref[front:, :] = src_ref[...]


def _att_kernel(q_ref, k_ref, v_ref, bias_ref, o_ref, kpad, vpad):
    i = pl.program_id(1)
    front = ATT_WIN - ATT_QB

    @pl.when(i == 0)
    def _():
        _fill_padded(kpad, k_ref, front)
        _fill_padded(vpad, v_ref, front)

    col = lax.broadcasted_iota(jnp.int32, (1, ATT_WIN), 1)
    first = lax.broadcasted_iota(jnp.int32, (ATT_QB, LANES), 1) < ATT_DIM
    nt = (((1,), (1,)), ((), ()))
    for sub in range(ATT_STEP_BLOCKS):
        rows = slice(sub * ATT_QB, (sub + 1) * ATT_QB)
        start = pl.multiple_of((i * ATT_STEP_BLOCKS + sub) * ATT_QB, ATT_QB)
        pen = jnp.where(col >= front - start, 0.0, -jnp.inf)
        for p in range(ATT_HEADS * ATT_DIM // LANES):
            cols = slice(p * LANES, (p + 1) * LANES)
            q = q_ref[rows, cols]
            kw = kpad[pl.ds(start, ATT_WIN), cols]
            vw = vpad[pl.ds(start, ATT_WIN), cols]
            zero = jnp.zeros_like(q)
            qq = jnp.concatenate([jnp.where(first, q, zero), jnp.where(first, zero, q)], axis=0)
            bias = jnp.concatenate([bias_ref[2 * p], bias_ref[2 * p + 1]], axis=0)
            s = lax.dot_general(qq, kw, nt, preferred_element_type=F32) + bias + pen
            e = jnp.exp(s - jnp.max(s, axis=-1, keepdims=True))
            den = jnp.sum(e, axis=-1, keepdims=True)
            o = jnp.dot(e.astype(BF16), vw, preferred_element_type=F32) / den
            o_ref[rows, cols] = jnp.where(first, o[:ATT_QB], o[ATT_QB:]).astype(BF16)


def _chunk_attention(proj, rel_bias, bsz, seq):
    n = proj.shape[0]
    rows = ATT_STEP_BLOCKS * ATT_QB
    nq = seq // rows
    front = ATT_WIN - ATT_QB
    return pl.pallas_call(
        _att_kernel,
        grid=(bsz, nq),
        in_specs=[pl.BlockSpec((rows, HALF), lambda b, i: (b * nq + i, 4)),
                  pl.BlockSpec((seq, HALF), lambda b, i: (b, 5)),
                  pl.BlockSpec((seq, HALF), lambda b, i: (b, 6)),
                  pl.BlockSpec((ATT_HEADS, ATT_QB, ATT_WIN), lambda b, i: (0, 0, 0))],
        out_specs=pl.BlockSpec((rows, HALF), lambda b, i: (b * nq + i, 0)),
        out_shape=jax.ShapeDtypeStruct((n, HALF), BF16),
        scratch_shapes=[pltpu.VMEM((front + seq, HALF), BF16), pltpu.VMEM((front + seq, HALF), BF16)],
        compiler_params=_cparams(("arbitrary", "arbitrary")), name="chunk_attention",
    )(proj, proj, proj, _att_bias(rel_bias))


def _sb_logs(z):
    log_beta = jnp.minimum(z, 0.0) - jnp.log(1.0 + jnp.exp(-jnp.abs(z)))
    return log_beta, log_beta - z


def _sb_kernel(q_ref, k_ref, v_ref, tri_ref, o_ref, kpad, vpad, run_ref, acc_ref):
    front = SB_WIN - SB_ROWS
    nblk = SB_WIN // SB_BLK
    pairs = HALF // LANES
    _fill_padded(kpad, k_ref, front)
    _fill_padded(vpad, v_ref, front)
    row = lax.broadcasted_iota(jnp.int32, (SB_ROWS, SB_WIN), 0)
    col = lax.broadcasted_iota(jnp.int32, (SB_ROWS, SB_WIN), 1)
    causal_col = jnp.where(col - row < front, col, -1)
    first = lax.broadcasted_iota(jnp.int32, (SB_ROWS, LANES), 1) < SB_DIM
    nt = (((1,), (1,)), ((), ()))

    def first_window(iq, u):
        start = pl.multiple_of(iq * SB_ROWS, SB_ROWS)
        mask = causal_col >= jnp.maximum(front - start, 0)
        tri = tri_ref[...]

        def heads_of(p):
            q = q_ref[pl.ds(start, SB_ROWS), p * LANES:(p + 1) * LANES]
            zero = jnp.zeros_like(q)
            return jnp.where(first, q, zero), jnp.where(first, zero, q)

        alive = None
        mask2 = jnp.concatenate([mask, mask], axis=0)
        lbs, l1s, vws = [], [], []
        for p in range(pairs):
            cols = slice(p * LANES, (p + 1) * LANES)
            kw = kpad[pl.ds(start, SB_WIN), cols]
            vws.append(vpad[pl.ds(start, SB_WIN), cols])
            qq = jnp.concatenate(heads_of(p), axis=0)
            z = lax.dot_general(qq, kw, nt, preferred_element_type=F32)
            log_beta, log_1mb = _sb_logs(z)
            lbs.append(log_beta)
            l1 = jnp.where(mask2, log_1mb, 0.0).astype(BF16)
            l1s += [l1[hh * SB_ROWS:(hh + 1) * SB_ROWS, c * SB_BLK:(c + 1) * SB_BLK]
                    for hh in range(2) for c in range(nblk)]
        cs = jnp.dot(jnp.concatenate(l1s, axis=0), tri, preferred_element_type=F32)
        for p in range(pairs):
            rows = []
            for hh in range(2):
                h = 2 * p + hh
                run = jnp.zeros((SB_ROWS, SB_BLK), F32)
                accs = [None] * nblk
                for c in reversed(range(nblk)):
                    blk = cs[(h * nblk + c) * SB_ROWS:(h * nblk + c + 1) * SB_ROWS]
                    accs[c] = blk[:, :SB_BLK] + run
                    run = run + blk[:, SB_BLK:]
                rows.append(jnp.concatenate(accs, axis=1))
                run_ref[u, h] = run
                alive = run if alive is None else jnp.maximum(alive, run)
            a = jnp.where(mask2, jnp.exp(lbs[p] + jnp.concatenate(rows, axis=0)), 0.0)
            o = jnp.dot(a.astype(BF16), vws[p], preferred_element_type=F32)
            acc_ref[u, 2 * p] = o[:SB_ROWS]
            acc_ref[u, 2 * p + 1] = o[SB_ROWS:]
        return jnp.max(alive)

    def older_keys(iq, u, alive0):
        start = pl.multiple_of(iq * SB_ROWS, SB_ROWS)
        tri = tri_ref[...]

        def heads_of(p):
            q = q_ref[pl.ds(start, SB_ROWS), p * LANES:(p + 1) * LANES]
            zero = jnp.zeros_like(q)
            return jnp.where(first, q, zero), jnp.where(first, zero, q)

        def cond(c):
            return jnp.logical_and(c[0] >= 0, c[1] > SB_EXIT)

        def body(c):
            j = c[0]
            kstart = pl.multiple_of(front + j * SB_ROWS, SB_ROWS)
            alive = None
            for p in range(pairs):
                cols = slice(p * LANES, (p + 1) * LANES)
                kb = kpad[pl.ds(kstart, SB_ROWS), cols]
                vb = vpad[pl.ds(kstart, SB_ROWS), cols]
                for hh, qm in enumerate(heads_of(p)):
                    z = lax.dot_general(qm, kb, nt, preferred_element_type=F32)
                    log_beta, log_1mb = _sb_logs(z)
                    cs = jnp.dot(log_1mb.astype(BF16), tri[:SB_ROWS], preferred_element_type=F32)
                    run = run_ref[u, 2 * p + hh]
                    a = jnp.exp(log_beta + run[:, :SB_ROWS] + cs[:, :SB_ROWS])
                    acc_ref[u, 2 * p + hh] += jnp.dot(a.astype(BF16), vb, preferred_element_type=F32)
                    run = run + cs[:, SB_BLK:]
                    run_ref[u, 2 * p + hh] = run
                    alive = run if alive is None else jnp.maximum(alive, run)
            return j - 1, jnp.max(alive)

        lax.while_loop(cond, body, (iq - SB_WIN // SB_ROWS, alive0))
        for p in range(pairs):
            o_ref[pl.ds(start, SB_ROWS), p * LANES:(p + 1) * LANES] = jnp.where(
                first, acc_ref[u, 2 * p], acc_ref[u, 2 * p + 1]).astype(BF16)

    def query_blocks(it, carry):
        alive = [first_window(it * SB_UNROLL + u, u) for u in range(SB_UNROLL)]
        for u in range(SB_UNROLL):
            older_keys(it * SB_UNROLL + u, u, alive[u])
        return carry

    lax.fori_loop(0, q_ref.shape[0] // (SB_ROWS * SB_UNROLL), query_blocks, 0)


def _stick_breaking(proj, bsz, seq):
    n = proj.shape[0]
    groups = SB_HEADS * SB_DIM // HALF
    front = SB_WIN - SB_ROWS
    j = np.arange(SB_BLK)
    tri = np.concatenate([(j[:, None] > j[None, :]).astype(np.float32),
                          np.ones((SB_BLK, SB_BLK), np.float32)], axis=1)
    return pl.pallas_call(
        _sb_kernel,
        grid=(bsz, groups),
        in_specs=[pl.BlockSpec((seq, HALF), lambda b, g: (b, g)),
                  pl.BlockSpec((seq, HALF), lambda b, g: (b, groups + g)),
                  pl.BlockSpec((seq, HALF), lambda b, g: (b, 2 * groups + g)),
                  pl.BlockSpec((SB_BLK, 2 * SB_BLK), lambda b, g: (0, 0))],
        out_specs=pl.BlockSpec((seq, HALF), lambda b, g: (b, g)),
        out_shape=jax.ShapeDtypeStruct((n, SB_HEADS * SB_DIM), BF16),
        scratch_shapes=[pltpu.VMEM((front + seq, HALF), BF16), pltpu.VMEM((front + seq, HALF), BF16),
                        pltpu.VMEM((SB_UNROLL, HALF // SB_DIM, SB_ROWS, SB_BLK), F32),
                        pltpu.VMEM((SB_UNROLL, HALF // SB_DIM, SB_ROWS, LANES), F32)],
        compiler_params=_cparams(("arbitrary", "arbitrary")), name="stick_breaking",
    )(proj, proj, proj, jnp.asarray(tri, BF16))


def _out_route_kernel(ma_ref, mb_ref, wa_ref, wb_ref, x_ref, g1_ref, g_ref, sc_ref, sh_ref,
                      rw_ref, rb_ref, tri_ref, xo_ref, he_ref, rt_ref, cnt_ref, carry_ref):
    @pl.when(pl.program_id(0) == 0)
    def _():
        carry_ref[...] = jnp.zeros_like(carry_ref)

    tm = x_ref.shape[0]
    mix = (jnp.dot(ma_ref[...], wa_ref[...], preferred_element_type=F32)
           + jnp.dot(mb_ref[...], wb_ref[...], preferred_element_type=F32))
    x = x_ref[...] + g1_ref[0] * mix
    xo_ref[...] = x
    h = _norm_mod(x, g_ref[...], sc_ref[0], sh_ref[0])
    for jt in range(ROW_SUB):
        he_ref[pl.ds(jt, tm, stride=ROW_SUB), :] = h[:, jt * LANES:(jt + 1) * LANES]

    nt = (((1,), (1,)), ((), ()))
    h_hi = h.astype(BF16)
    h_lo = (h - h_hi.astype(F32)).astype(BF16)
    w = rw_ref[...]
    w_hi = w.astype(BF16)
    w_lo = (w - w_hi.astype(F32)).astype(BF16)
    logits = (lax.dot_general(w_hi, h_hi, nt, preferred_element_type=F32)
              + lax.dot_general(w_hi, h_lo, nt, preferred_element_type=F32)
              + lax.dot_general(w_lo, h_hi, nt, preferred_element_type=F32))
    sel = jax.nn.sigmoid(logits) + rb_ref[...]

    best = None
    for g in range(N_GROUPS):
        for p in range(len(PAIR_A)):
            ea, eb = 4 * g + PAIR_A[p], 4 * g + PAIR_B[p]
            val = sel[ea:ea + 1, :] + sel[eb:eb + 1, :]
            cid = jnp.full((1, tm), float(len(PAIR_A) * g + p), F32)
            cand = (val, cid)
            if best is None:
                best = cand
            else:
                take = val > best[0]
                best = tuple(jnp.where(take, c_, b_) for c_, b_ in zip(cand, best))
    cls = best[1]

    crow = lax.broadcasted_iota(jnp.int32, (CLASS_ROWS, tm), 0).astype(F32)
    onehot = jnp.where(crow == cls, 1.0, 0.0)
    before = jnp.dot(onehot.astype(BF16), tri_ref[...], preferred_element_type=F32)
    carry = carry_ref[...]
    rank = jnp.sum(onehot * (before + carry[:, :1]), axis=0, keepdims=True)
    carry = carry + jnp.sum(onehot, axis=1, keepdims=True)
    carry_ref[...] = carry
    cnt_ref[0] = carry

    srow = lax.broadcasted_iota(jnp.int32, (8, tm), 0)
    rt_ref[0] = jnp.where(srow == 0, cls, jnp.where(srow == 1, rank, 0.0))


def _out_route(mix_a, mix_b, col_a, col_b, w_out, x, g1, gain, sc, sh, router_w, router_b, seq):
    n = x.shape[0]
    tm = ROW_TILE
    nt = n // tm
    tpb = seq // tm
    t = np.arange(tm)
    tri = jnp.asarray((t[:, None] < t[None, :]).astype(np.float32), BF16)
    row = pl.BlockSpec((tm, D_MODEL), lambda i: (i, 0))
    per_b = pl.BlockSpec((1, 1, D_MODEL), lambda i: (i // tpb, 0, 0))
    rb = jnp.broadcast_to(router_b.astype(F32)[:, None], (N_EXPERTS, tm))
    return pl.pallas_call(
        _out_route_kernel,
        grid=(nt,),
        in_specs=[pl.BlockSpec((tm, HALF), lambda i: (i, col_a)),
                  pl.BlockSpec((tm, HALF), lambda i: (i, col_b)),
                  pl.BlockSpec((HALF, D_MODEL), lambda i: (0, 0)),
                  pl.BlockSpec((HALF, D_MODEL), lambda i: (1, 0)),
                  row, per_b,
                  pl.BlockSpec((1, D_MODEL), lambda i: (0, 0)), per_b, per_b,
                  pl.BlockSpec((N_EXPERTS, D_MODEL), lambda i: (0, 0)),
                  pl.BlockSpec((N_EXPERTS, tm), lambda i: (0, 0)),
                  pl.BlockSpec((tm, tm), lambda i: (0, 0))],
        out_specs=[row,
                   pl.BlockSpec((tm * ROW_SUB, LANES), lambda i: (i, 0)),
                   pl.BlockSpec((1, 8, tm), lambda i: (i, 0, 0)),
                   pl.BlockSpec((1, CLASS_ROWS, LANES), lambda i: (i, 0, 0))],
        out_shape=[jax.ShapeDtypeStruct((n, D_MODEL), F32),
                   jax.ShapeDtypeStruct((n * ROW_SUB, LANES), F32),
                   jax.ShapeDtypeStruct((nt, 8, tm), F32),
                   jax.ShapeDtypeStruct((nt, CLASS_ROWS, LANES), F32)],
        scratch_shapes=[pltpu.VMEM((CLASS_ROWS, LANES), F32)],
        compiler_params=_cparams(("arbitrary",)), name="out_proj_route",
    )(mix_a, mix_b, w_out, w_out, x, g1, gain.reshape(1, D_MODEL), sc, sh,
      router_w.T.astype(F32), rb, tri)


def _fill_slots(pos_ref, nv_ref, inv_ref):
    batch = 16

    def fill_tile(j, c):
        def fill(g, c2):
            for u in range(batch):
                inv_ref[j * MOE_TILE + g * batch + u] = 0
            return c2

        return lax.fori_loop(nv_ref[j] // batch, MOE_TILE // batch, fill, c)

    lax.fori_loop(0, nv_ref.shape[0], fill_tile, 0)

    def place(g, c):
        slots = [pos_ref[g * batch + u] for u in range(batch)]
        for u in range(batch):
            inv_ref[slots[u]] = g * batch + u
        return c

    lax.fori_loop(0, pos_ref.shape[0] // batch, place, 0)


def _routing_plan(route, counts, n):
    nt, _, tm = route.shape
    cls = route[:, 0, :].reshape(n).astype(jnp.int32)
    rank = route[:, 1, :].reshape(n).astype(jnp.int32)
    cnt = counts[-1, :N_CLASS, 0].astype(jnp.int32)
    tiles_c = (cnt + MOE_TILE - 1) // MOE_TILE
    ends = jnp.cumsum(tiles_c)
    starts = ends - tiles_c
    pos = starts[cls] * MOE_TILE + rank
    max_tiles = n // MOE_TILE + N_CLASS
    total = ends[-1]
    j = jnp.arange(max_tiles, dtype=jnp.int32)
    jj = jnp.minimum(j, total - 1)
    tcls = jnp.sum((ends[None, :] <= jj[:, None]).astype(jnp.int32), axis=1)
    grp = tcls // len(PAIR_A)
    pair = tcls % len(PAIR_A)
    ea = 4 * grp + jnp.asarray(PAIR_A, jnp.int32)[pair]
    eb = 4 * grp + jnp.asarray(PAIR_B, jnp.int32)[pair]
    valid_rows = jnp.clip(cnt[tcls] - (jj - starts[tcls]) * MOE_TILE, 0, MOE_TILE)
    valid_rows = jnp.where(j < total, valid_rows, 0).astype(jnp.int32)
    return pos, ea, eb, valid_rows, total.reshape(1)


def _moe_kernel(pos_ref, ea_ref, eb_ref, nv_ref, tot_ref, h_hbm, rw_ref, wg_a, wu_a, wd_a,
                wg_b, wu_b, wd_b, y_hbm, gbuf, ybuf, inv_ref, gsem, ssem):
    j = pl.program_id(0)
    total = tot_ref[0]
    gslot = j % GATHER_BUFS
    yslot = j % 2
    unroll = 16

    def gather_row(tile, s, r, prio):
        src = pl.multiple_of(inv_ref[tile * MOE_TILE + r] * ROW_SUB, ROW_SUB)
        dst = pl.multiple_of(r * ROW_SUB, ROW_SUB)
        pltpu.make_async_copy(h_hbm.at[pl.ds(src, ROW_SUB), :], gbuf.at[s, pl.ds(dst, ROW_SUB), :],
                              gsem.at[s]).start(priority=prio)

    def scatter_row(tile, s, r, prio):
        src = pl.multiple_of(r * ROW_SUB, ROW_SUB)
        dst = pl.multiple_of(inv_ref[tile * MOE_TILE + r] * ROW_SUB, ROW_SUB)
        pltpu.make_async_copy(ybuf.at[s, pl.ds(src, ROW_SUB), :], y_hbm.at[pl.ds(dst, ROW_SUB), :],
                              ssem.at[s]).start(priority=prio)

    def gathered_rows(tile):
        return (nv_ref[tile] + unroll - 1) // unroll * unroll

    def gather_rows(tile, s):
        def group(g, c):
            for u in range(unroll):
                gather_row(tile, s, g * unroll + u, u % 2)
            return c

        lax.fori_loop(0, gathered_rows(tile) // unroll, group, 0)

    def wait_rows(count, make_copy):
        for bit in range(MOE_TILE.bit_length()):
            @pl.when((count >> bit) & 1 == 1)
            def _():
                make_copy((1 << bit) * ROW_SUB).wait()

    def gather_wait(tile, s):
        wait_rows(gathered_rows(tile), lambda size: pltpu.make_async_copy(
            h_hbm.at[pl.ds(0, size), :], gbuf.at[s, pl.ds(0, size), :], gsem.at[s]))

    def scatter_rows(tile, s):
        nv = nv_ref[tile]
        groups = nv // unroll

        def group(g, c):
            for u in range(unroll):
                scatter_row(tile, s, g * unroll + u, u % 2)
            return c

        lax.fori_loop(0, groups, group, 0)
        for u in range(unroll):
            @pl.when(groups * unroll + u < nv)
            def _():
                scatter_row(tile, s, groups * unroll + u, u % 2)

    def scatter_wait(tile, s):
        wait_rows(nv_ref[tile], lambda size: pltpu.make_async_copy(
            ybuf.at[s, pl.ds(0, size), :], y_hbm.at[pl.ds(0, size), :], ssem.at[s]))

    @pl.when(j == 0)
    def _():
        _fill_slots(pos_ref, nv_ref, inv_ref)
        gbuf[...] = jnp.zeros_like(gbuf)
        for t in range(GATHER_BUFS - 1):
            @pl.when(t < total)
            def _():
                gather_rows(t, t)

    @pl.when(j < total)
    def _():
        gather_wait(j, gslot)
        x = jnp.concatenate([gbuf[gslot, pl.ds(jt, MOE_TILE, stride=ROW_SUB), :]
                             for jt in range(ROW_SUB)], axis=1)
        xb = x.astype(BF16)
        scores = [jax.nn.sigmoid(jnp.sum(x * rw_ref[pl.ds(e_ref[j], 1), :], axis=-1, keepdims=True))
                  for e_ref in (ea_ref, eb_ref)]
        shares = [sc_ / (scores[0] + scores[1]) for sc_ in scores]
        y = None
        for lane, (wg, wu, wd) in enumerate(((wg_a, wu_a, wd_a), (wg_b, wu_b, wd_b))):
            g = jnp.dot(xb, wg[0].astype(BF16), preferred_element_type=F32)
            u = jnp.dot(xb, wu[0].astype(BF16), preferred_element_type=F32)
            he = ((g * jax.nn.sigmoid(g)) * u * shares[lane]).astype(BF16)
            part = jnp.dot(he, wd[0].astype(BF16), preferred_element_type=F32)
            y = part if y is None else y + part

        @pl.when(j >= 2)
        def _():
            scatter_wait(j - 2, yslot)

        for jt in range(ROW_SUB):
            chunk = y[:, jt * LANES:(jt + 1) * LANES]
            ybuf[yslot, pl.ds(jt, MOE_TILE, stride=ROW_SUB), :] = chunk

        scatter_rows(j, yslot)
        ahead = j + GATHER_BUFS - 1

        @pl.when(ahead < total)
        def _():
            gather_rows(ahead, ahead % GATHER_BUFS)

        @pl.when(j == total - 1)
        def _():
            @pl.when(j >= 1)
            def _():
                scatter_wait(j - 1, 1 - yslot)

            scatter_wait(j, yslot)


def _moe(hext, pos, ea, eb, valid_rows, total, router_wt, layer, w_gate, w_up, w_down):
    n = hext.shape[0] // ROW_SUB
    max_tiles = ea.shape[0]

    def wspec(shape, which):
        if which == 0:
            return pl.BlockSpec((None, 1) + shape,
                                lambda j, inv_, ea_, eb_, nv_, t_: (layer, ea_[j], 0, 0))
        return pl.BlockSpec((None, 1) + shape, lambda j, inv_, ea_, eb_, nv_, t_: (layer, eb_[j], 0, 0))

    up = (D_MODEL, D_EXPERT)
    down = (D_EXPERT, D_MODEL)
    grid_spec = pltpu.PrefetchScalarGridSpec(
        num_scalar_prefetch=5,
        grid=(max_tiles,),
        in_specs=[pl.BlockSpec(memory_space=pl.ANY),
                  pl.BlockSpec((N_EXPERTS, D_MODEL), lambda j, inv_, ea_, eb_, nv_, t_: (0, 0)),
                  wspec(up, 0), wspec(up, 0), wspec(down, 0),
                  wspec(up, 1), wspec(up, 1), wspec(down, 1)],
        out_specs=pl.BlockSpec(memory_space=pl.ANY),
        scratch_shapes=[pltpu.VMEM((GATHER_BUFS, MOE_TILE * ROW_SUB, LANES), F32),
                        pltpu.VMEM((2, MOE_TILE * ROW_SUB, LANES), F32),
                        pltpu.SMEM((max_tiles * MOE_TILE,), jnp.int32),
                        pltpu.SemaphoreType.DMA((GATHER_BUFS,)),
                        pltpu.SemaphoreType.DMA((2,))],
    )
    return pl.pallas_call(
        _moe_kernel,
        grid_spec=grid_spec,
        out_shape=jax.ShapeDtypeStruct((n * ROW_SUB, LANES), F32),
        compiler_params=_cparams(("arbitrary",)), name="moe_experts",
    )(pos, ea, eb, valid_rows, total, hext, router_wt, w_gate, w_up, w_down, w_gate, w_up, w_down)


def _final_kernel(x_ref, y_ref, g_ref, o_ref):
    o_ref[...] = x_ref[...] + g_ref[0] * _token_rows(y_ref)


def _final(x, y, gate, seq):
    n = x.shape[0]
    tpb = seq // ROW_TILE
    row = pl.BlockSpec((ROW_TILE, D_MODEL), lambda i: (i, 0))
    return pl.pallas_call(
        _final_kernel, grid=(n // ROW_TILE,),
        in_specs=[row, pl.BlockSpec((ROW_TILE * ROW_SUB, LANES), lambda i: (i, 0)),
                  pl.BlockSpec((1, 1, D_MODEL), lambda i: (i // tpb, 0, 0))],
        out_specs=row, out_shape=jax.ShapeDtypeStruct((n, D_MODEL), F32),
        compiler_params=_cparams(("arbitrary",)), name="final_residual",
    )(x, y, gate)


def kernel(x, c, ada_w, ada_b, norm1_g, norm2_g, even_w_in, even_w_out, att_q_norm_g, att_k_norm_g,
           att_rel_bias, odd_w_in, odd_w_out, router_w, router_b, exp_w_gate, exp_w_up, exp_w_down):
    bsz, seq, d = x.shape
    n = bsz * seq
    mod = _modulation(c, ada_w, ada_b)
    mods = [[mod[l, :, k * d:(k + 1) * d].reshape(bsz, 1, d) for k in range(6)] for l in range(2)]
    xf = x.reshape(n, d)
    w_gate, w_up, w_down = exp_w_gate, exp_w_up, exp_w_down

    def moe_layer(layer, mix_a, mix_b, col_a, col_b, w_out, xin):
        sh1, sc1, g1, sh2, sc2, g2 = mods[layer]
        x1, hext, route, counts = _out_route(mix_a, mix_b, col_a, col_b, w_out.astype(BF16), xin, g1,
                                             norm2_g[layer], sc2, sh2, router_w, router_b, seq)
        pos, ea, eb, valid_rows, total = _routing_plan(route, counts, n)
        y = _moe(hext, pos, ea, eb, valid_rows, total, router_w.T.astype(F32), layer,
                 w_gate, w_up, w_down)
        return x1, y, g2

    sh1, sc1, _, _, _, _ = mods[0]
    qk_gain = jnp.stack([att_q_norm_g[0], att_k_norm_g[0]])
    proj0 = _project(xf, norm1_g[0], sc1, sh1, even_w_in[0].astype(BF16), seq, qk_gain=qk_gain)
    ret = _retention(proj0, bsz, seq)
    att = _chunk_attention(proj0, att_rel_bias[0], bsz, seq)
    x1, y0, g2_0 = moe_layer(0, ret, att, 0, 0, even_w_out[0], xf)

    sh1, sc1, _, _, _, _ = mods[1]
    x2, proj1 = _project(x1, norm1_g[1], sc1, sh1, odd_w_in[0].astype(BF16), seq, y=y0, gate=g2_0)
    sbo = _stick_breaking(proj1, bsz, seq)
    x3, y1, g2_1 = moe_layer(1, sbo, sbo, 0, 1, odd_w_out[0], x2)
    return _final(x3, y1, g2_1, seq).reshape(bsz, seq, d)
```

```python
import functools

import numpy as np
import jax
import jax.numpy as jnp
from jax import lax
from jax.experimental import pallas as pl
from jax.experimental.pallas import tpu as pltpu

F32 = jnp.float32
BF16 = jnp.bfloat16

D_MODEL = 1024
CHUNK = 64
EPS = 1e-6
ROPE_BASE = 10000.0
RET_HEADS = 4
RET_DIM = 128
ATT_HEADS = 8
ATT_DIM = 64
ATT_LEFT_CHUNKS = 8
REL_CLIP = 128
SB_HEADS = 16
SB_DIM = 64
N_EXPERTS = 16
N_GROUPS = 4
D_EXPERT = 512
HALF = 512

VMEM_LIMIT = 48 * 1024 * 1024
LANES = 128

ROW_TILE = 1024
RET_TILE = 256
ATT_QB = 256
ATT_WIN = ATT_LEFT_CHUNKS * CHUNK + ATT_QB
ATT_STEP_BLOCKS = 4
SB_BLK = 128
SB_ROWS = 64
SB_UNROLL = 8
SB_WIN = 3 * SB_BLK
SB_EXIT = -110.0

PAIR_A = (0, 0, 0, 1, 1, 2)
PAIR_B = (1, 2, 3, 2, 3, 3)
N_CLASS = N_GROUPS * len(PAIR_A)
CLASS_ROWS = 32
MOE_TILE = 256
ROW_SUB = D_MODEL // LANES
GATHER_BUFS = 3


def _cparams(sem):
    return pltpu.CompilerParams(dimension_semantics=sem, vmem_limit_bytes=VMEM_LIMIT)


def _mod_kernel(c_ref, w_ref, b_ref, o_ref):
    c = c_ref[...]
    ca = c * jax.nn.sigmoid(c)
    o_ref[0] = jnp.dot(ca.astype(BF16), w_ref[0].astype(BF16), preferred_element_type=F32) + b_ref[0]


def _modulation(c, ada_w, ada_b):
    depth, _, width = ada_w.shape
    bsz = c.shape[0]
    rows = 8
    cp = jnp.zeros((rows, D_MODEL), F32).at[:bsz].set(c)
    tn = 1536
    out = pl.pallas_call(
        _mod_kernel,
        grid=(depth, width // tn),
        in_specs=[pl.BlockSpec((rows, D_MODEL), lambda l, j: (0, 0)),
                  pl.BlockSpec((1, D_MODEL, tn), lambda l, j: (l, 0, j)),
                  pl.BlockSpec((1, 1, tn), lambda l, j: (l, 0, j))],
        out_specs=pl.BlockSpec((1, rows, tn), lambda l, j: (l, 0, j)),
        out_shape=jax.ShapeDtypeStruct((depth, rows, width), F32),
        compiler_params=_cparams(("arbitrary", "arbitrary")),
        name="adaln_mod",
    )(cp, ada_w, ada_b.reshape(depth, 1, width))
    return out[:, :bsz]


def _norm_mod(x, g, sc, sh):
    ms = jnp.mean(x * x, axis=-1, keepdims=True)
    return (x * lax.rsqrt(ms + EPS) * g) * (1.0 + sc) + sh


def _project_chunks(h, w_ref, o_ref, qk_ref, bd_ref, q_scale):
    for ci, n0 in enumerate(range(0, o_ref.shape[1], HALF)):
        p = jnp.dot(h, w_ref[:, n0:n0 + HALF], preferred_element_type=F32)
        if qk_ref is not None and ci in (4, 5):
            ms = jnp.dot((p * p).astype(BF16), bd_ref[...], preferred_element_type=F32)
            p = p * lax.rsqrt(ms + EPS) * qk_ref[ci - 4:ci - 3, :]
            if ci == 4:
                p = p * q_scale
        elif qk_ref is None and n0 < SB_HEADS * SB_DIM:
            p = p * q_scale
        o_ref[:, n0:n0 + HALF] = p.astype(BF16)


def _proj_kernel(x_ref, g_ref, sc_ref, sh_ref, w_ref, qk_ref, bd_ref, o_ref):
    h = _norm_mod(x_ref[...], g_ref[...], sc_ref[0], sh_ref[0]).astype(BF16)
    _project_chunks(h, w_ref, o_ref, qk_ref, bd_ref, ATT_DIM ** -0.5)


def _token_rows(y_ref):
    rows = y_ref.shape[0] // ROW_SUB
    return jnp.concatenate([y_ref[pl.ds(jt, rows, stride=ROW_SUB), :] for jt in range(ROW_SUB)],
                           axis=1)


def _proj_res_kernel(x_ref, y_ref, gt_ref, g_ref, sc_ref, sh_ref, w_ref, xo_ref, o_ref):
    x = x_ref[...] + gt_ref[0] * _token_rows(y_ref)
    xo_ref[...] = x
    h = _norm_mod(x, g_ref[...], sc_ref[0], sh_ref[0]).astype(BF16)
    _project_chunks(h, w_ref, o_ref, None, None, SB_DIM ** -0.5)


def _project(x, gain, sc, sh, w, seq, qk_gain=None, y=None, gate=None):
    n = x.shape[0]
    nout = w.shape[1]
    tpb = seq // ROW_TILE
    row = pl.BlockSpec((ROW_TILE, D_MODEL), lambda i: (i, 0))
    per_b = pl.BlockSpec((1, 1, D_MODEL), lambda i: (i // tpb, 0, 0))
    gspec = pl.BlockSpec((1, D_MODEL), lambda i: (0, 0))
    wspec = pl.BlockSpec((D_MODEL, nout), lambda i: (0, 0))
    ospec = pl.BlockSpec((ROW_TILE, nout), lambda i: (i, 0))
    oshape = jax.ShapeDtypeStruct((n, nout), BF16)
    if y is None:
        head = np.arange(HALF) // ATT_DIM
        bd = jnp.asarray((head[:, None] == head[None, :]).astype(np.float32) / ATT_DIM, BF16)
        qk = jnp.tile(qk_gain.astype(F32), (1, ATT_HEADS))
        return pl.pallas_call(
            _proj_kernel, grid=(n // ROW_TILE,),
            in_specs=[row, gspec, per_b, per_b, wspec,
                      pl.BlockSpec((2, HALF), lambda i: (0, 0)),
                      pl.BlockSpec((HALF, HALF), lambda i: (0, 0))],
            out_specs=ospec, out_shape=oshape,
            compiler_params=_cparams(("arbitrary",)), name="norm_proj",
        )(x, gain.reshape(1, D_MODEL), sc, sh, w, qk, bd)
    return pl.pallas_call(
        _proj_res_kernel, grid=(n // ROW_TILE,),
        in_specs=[row, pl.BlockSpec((ROW_TILE * ROW_SUB, LANES), lambda i: (i, 0)), per_b, gspec,
                  per_b, per_b, wspec],
        out_specs=[row, ospec],
        out_shape=[jax.ShapeDtypeStruct((n, D_MODEL), F32), oshape],
        compiler_params=_cparams(("arbitrary",)), name="res_norm_proj",
    )(x, y, gate, gain.reshape(1, D_MODEL), sc, sh, w)


def _ret_tables(seq):
    inv = ROPE_BASE ** (-np.arange(0, RET_DIM, 2, dtype=np.float64) / RET_DIM)
    ang = np.arange(seq, dtype=np.float64)[:, None] * inv[None, :]
    cos = np.concatenate([np.cos(ang), np.cos(ang)], axis=1)
    sin = np.concatenate([-np.sin(ang), np.sin(ang)], axis=1)
    log_g = np.log(1.0 - 2.0 ** (-5.0 - np.arange(RET_HEADS, dtype=np.float64)))
    idx = np.arange(RET_TILE)
    same_or_earlier = (idx[None, :] // CHUNK) <= (idx[:, None] // CHUNK)
    dec = np.exp(log_g[:, None, None] * np.abs(idx[:, None] - idx[None, :])) * same_or_earlier
    loc = idx.astype(np.float64)
    qdec = np.exp(log_g[:, None] * (loc[None, :] + 1.0))
    kdec = np.exp(log_g[:, None] * (RET_TILE - 1.0 - loc[None, :]))
    tdec = np.exp(log_g * RET_TILE)
    qk = np.stack([qdec, kdec], axis=1)[..., None] * np.ones((1, 1, 1, RET_DIM))
    return (jnp.asarray(cos, F32), jnp.asarray(sin, F32), jnp.asarray(dec, F32),
            jnp.asarray(qk, F32), tuple(float(t) for t in tdec))


def _ret_kernel(tdec, q_ref, k_ref, v_ref, g_ref, cos_ref, sin_ref, dec_ref, qk_ref, o_ref, st_ref):
    @pl.when(pl.program_id(1) == 0)
    def _():
        st_ref[...] = jnp.zeros_like(st_ref)

    cos = cos_ref[...]
    sin = sin_ref[...]
    half = RET_DIM // 2
    for h in range(RET_HEADS):
        cols = slice(h * RET_DIM, (h + 1) * RET_DIM)
        q = q_ref[:, cols].astype(F32)
        k = k_ref[:, cols].astype(F32)
        q = q * cos + pltpu.roll(q, half, 1) * sin
        k = (k * cos + pltpu.roll(k, half, 1) * sin) * (RET_DIM ** -0.5)
        v = v_ref[:, cols]
        qb = q.astype(BF16)
        kb = k.astype(BF16)
        s = lax.dot_general(qb, kb, (((1,), (1,)), ((), ())), preferred_element_type=F32)
        s = s * dec_ref[h]
        o = jnp.dot(s.astype(BF16), v, preferred_element_type=F32)
        st = st_ref[h]
        o = o + jnp.dot((q * qk_ref[h, 0]).astype(BF16), st.astype(BF16), preferred_element_type=F32)
        kd = (k * qk_ref[h, 1]).astype(BF16)
        st_ref[h] = st * tdec[h] + lax.dot_general(kd, v, (((0,), (0,)), ((), ())),
                                                   preferred_element_type=F32)
        o = o * lax.rsqrt(jnp.mean(o * o, axis=-1, keepdims=True) + EPS)
        g = g_ref[:, cols].astype(F32)
        o_ref[:, cols] = (o * (g * jax.nn.sigmoid(g))).astype(BF16)


def _retention(proj, bsz, seq):
    n = proj.shape[0]
    nt = seq // RET_TILE
    cos, sin, dec, qk, tdec = _ret_tables(seq)

    def col(c):
        return pl.BlockSpec((RET_TILE, HALF), lambda b, i: (b * nt + i, c))

    pos = pl.BlockSpec((RET_TILE, RET_DIM), lambda b, i: (i, 0))
    return pl.pallas_call(
        functools.partial(_ret_kernel, tdec),
        grid=(bsz, nt),
        in_specs=[col(0), col(1), col(2), col(3), pos, pos,
                  pl.BlockSpec((RET_HEADS, RET_TILE, RET_TILE), lambda b, i: (0, 0, 0)),
                  pl.BlockSpec((RET_HEADS, 2, RET_TILE, RET_DIM), lambda b, i: (0, 0, 0, 0))],
        out_specs=pl.BlockSpec((RET_TILE, HALF), lambda b, i: (b * nt + i, 0)),
        out_shape=jax.ShapeDtypeStruct((n, HALF), BF16),
        scratch_shapes=[pltpu.VMEM((RET_HEADS, RET_DIM, RET_DIM), F32)],
        compiler_params=_cparams(("arbitrary", "arbitrary")), name="retention",
    )(proj, proj, proj, proj, cos, sin, dec, qk)


def _att_bias(rel_bias):
    span = ATT_WIN + ATT_QB
    m = np.concatenate([np.arange(0, ATT_WIN), np.zeros(1, np.int64), np.arange(-(ATT_QB - 1), 0)])
    idx = np.clip((ATT_WIN - ATT_QB) - m, -(CHUNK - 1), REL_CLIP) + (CHUNK - 1)
    vec = rel_bias[:, idx].astype(F32)
    heads = rel_bias.shape[0]
    toep = jnp.tile(vec, (1, ATT_QB))[:, :ATT_QB * (span - 1)].reshape(heads, ATT_QB, span - 1)
    qc = np.arange(ATT_QB)[:, None] // CHUNK
    kc = np.arange(ATT_WIN)[None, :] // CHUNK - ATT_LEFT_CHUNKS
    ok = (kc <= qc) & (kc >= qc - ATT_LEFT_CHUNKS)
    return jnp.where(jnp.asarray(ok)[None], toep[:, :, :ATT_WIN], -jnp.inf)


def _fill_padded(pad_ref, src_ref, front):
    pad_ref[:front, :] = jnp.zeros((front, pad_ref.shape[1]), pad_ref.dtype)
    pad_ref[front:, :] = src_ref[...]


def _att_kernel(q_ref, k_ref, v_ref, bias_ref, o_ref, kpad, vpad):
    i = pl.program_id(1)
    front = ATT_WIN - ATT_QB

    @pl.when(i == 0)
    def _():
        _fill_padded(kpad, k_ref, front)
        _fill_padded(vpad, v_ref, front)

    col = lax.broadcasted_iota(jnp.int32, (1, ATT_WIN), 1)
    first = lax.broadcasted_iota(jnp.int32, (ATT_QB, LANES), 1) < ATT_DIM
    nt = (((1,), (1,)), ((), ()))
    for sub in range(ATT_STEP_BLOCKS):
        rows = slice(sub * ATT_QB, (sub + 1) * ATT_QB)
        start = pl.multiple_of((i * ATT_STEP_BLOCKS + sub) * ATT_QB, ATT_QB)
        pen = jnp.where(col >= front - start, 0.0, -jnp.inf)
        for p in range(ATT_HEADS * ATT_DIM // LANES):
            cols = slice(p * LANES, (p + 1) * LANES)
            q = q_ref[rows, cols]
            kw = kpad[pl.ds(start, ATT_WIN), cols]
            vw = vpad[pl.ds(start, ATT_WIN), cols]
            zero = jnp.zeros_like(q)
            qq = jnp.concatenate([jnp.where(first, q, zero), jnp.where(first, zero, q)], axis=0)
            bias = jnp.concatenate([bias_ref[2 * p], bias_ref[2 * p + 1]], axis=0)
            s = lax.dot_general(qq, kw, nt, preferred_element_type=F32) + bias + pen
            e = jnp.exp(s - jnp.max(s, axis=-1, keepdims=True))
            den = jnp.sum(e, axis=-1, keepdims=True)
            o = jnp.dot(e.astype(BF16), vw, preferred_element_type=F32) / den
            o_ref[rows, cols] = jnp.where(first, o[:ATT_QB], o[ATT_QB:]).astype(BF16)


def _chunk_attention(proj, rel_bias, bsz, seq):
    n = proj.shape[0]
    rows = ATT_STEP_BLOCKS * ATT_QB
    nq = seq // rows
    front = ATT_WIN - ATT_QB
    return pl.pallas_call(
        _att_kernel,
        grid=(bsz, nq),
        in_specs=[pl.BlockSpec((rows, HALF), lambda b, i: (b * nq + i, 4)),
                  pl.BlockSpec((seq, HALF), lambda b, i: (b, 5)),
                  pl.BlockSpec((seq, HALF), lambda b, i: (b, 6)),
                  pl.BlockSpec((ATT_HEADS, ATT_QB, ATT_WIN), lambda b, i: (0, 0, 0))],
        out_specs=pl.BlockSpec((rows, HALF), lambda b, i: (b * nq + i, 0)),
        out_shape=jax.ShapeDtypeStruct((n, HALF), BF16),
        scratch_shapes=[pltpu.VMEM((front + seq, HALF), BF16), pltpu.VMEM((front + seq, HALF), BF16)],
        compiler_params=_cparams(("arbitrary", "arbitrary")), name="chunk_attention",
    )(proj, proj, proj, _att_bias(rel_bias))


def _sb_logs(z):
    log_beta = jnp.minimum(z, 0.0) - jnp.log(1.0 + jnp.exp(-jnp.abs(z)))
    return log_beta, log_beta - z


def _sb_kernel(q_ref, k_ref, v_ref, tri_ref, o_ref, kpad, vpad, run_ref, acc_ref):
    front = SB_WIN - SB_ROWS
    nblk = SB_WIN // SB_BLK
    pairs = HALF // LANES
    _fill_padded(kpad, k_ref, front)
    _fill_padded(vpad, v_ref, front)
    row = lax.broadcasted_iota(jnp.int32, (SB_ROWS, SB_WIN), 0)
    col = lax.broadcasted_iota(jnp.int32, (SB_ROWS, SB_WIN), 1)
    causal_col = jnp.where(col - row < front, col, -1)
    first = lax.broadcasted_iota(jnp.int32, (SB_ROWS, LANES), 1) < SB_DIM
    nt = (((1,), (1,)), ((), ()))

    def first_window(iq, u):
        start = pl.multiple_of(iq * SB_ROWS, SB_ROWS)
        mask = causal_col >= jnp.maximum(front - start, 0)
        tri = tri_ref[...]

        def heads_of(p):
            q = q_ref[pl.ds(start, SB_ROWS), p * LANES:(p + 1) * LANES]
            zero = jnp.zeros_like(q)
            return jnp.where(first, q, zero), jnp.where(first, zero, q)

        alive = None
        mask2 = jnp.concatenate([mask, mask], axis=0)
        lbs, l1s, vws = [], [], []
        for p in range(pairs):
            cols = slice(p * LANES, (p + 1) * LANES)
            kw = kpad[pl.ds(start, SB_WIN), cols]
            vws.append(vpad[pl.ds(start, SB_WIN), cols])
            qq = jnp.concatenate(heads_of(p), axis=0)
            z = lax.dot_general(qq, kw, nt, preferred_element_type=F32)
            log_beta, log_1mb = _sb_logs(z)
            lbs.append(log_beta)
            l1 = jnp.where(mask2, log_1mb, 0.0).astype(BF16)
            l1s += [l1[hh * SB_ROWS:(hh + 1) * SB_ROWS, c * SB_BLK:(c + 1) * SB_BLK]
                    for hh in range(2) for c in range(nblk)]
        cs = jnp.dot(jnp.concatenate(l1s, axis=0), tri, preferred_element_type=F32)
        for p in range(pairs):
            rows = []
            for hh in range(2):
                h = 2 * p + hh
                run = jnp.zeros((SB_ROWS, SB_BLK), F32)
                accs = [None] * nblk
                for c in reversed(range(nblk)):
                    blk = cs[(h * nblk + c) * SB_ROWS:(h * nblk + c + 1) * SB_ROWS]
                    accs[c] = blk[:, :SB_BLK] + run
                    run = run + blk[:, SB_BLK:]
                rows.append(jnp.concatenate(accs, axis=1))
                run_ref[u, h] = run
                alive = run if alive is None else jnp.maximum(alive, run)
            a = jnp.where(mask2, jnp.exp(lbs[p] + jnp.concatenate(rows, axis=0)), 0.0)
            o = jnp.dot(a.astype(BF16), vws[p], preferred_element_type=F32)
            acc_ref[u, 2 * p] = o[:SB_ROWS]
            acc_ref[u, 2 * p + 1] = o[SB_ROWS:]
        return jnp.max(alive)

    def older_keys(iq, u, alive0):
        start = pl.multiple_of(iq * SB_ROWS, SB_ROWS)
        tri = tri_ref[...]

        def heads_of(p):
            q = q_ref[pl.ds(start, SB_ROWS), p * LANES:(p + 1) * LANES]
            zero = jnp.zeros_like(q)
            return jnp.where(first, q, zero), jnp.where(first, zero, q)

        def cond(c):
            return jnp.logical_and(c[0] >= 0, c[1] > SB_EXIT)

        def body(c):
            j = c[0]
            kstart = pl.multiple_of(front + j * SB_ROWS, SB_ROWS)
            alive = None
            for p in range(pairs):
                cols = slice(p * LANES, (p + 1) * LANES)
                kb = kpad[pl.ds(kstart, SB_ROWS), cols]
                vb = vpad[pl.ds(kstart, SB_ROWS), cols]
                for hh, qm in enumerate(heads_of(p)):
                    z = lax.dot_general(qm, kb, nt, preferred_element_type=F32)
                    log_beta, log_1mb = _sb_logs(z)
                    cs = jnp.dot(log_1mb.astype(BF16), tri[:SB_ROWS], preferred_element_type=F32)
                    run = run_ref[u, 2 * p + hh]
                    a = jnp.exp(log_beta + run[:, :SB_ROWS] + cs[:, :SB_ROWS])
                    acc_ref[u, 2 * p + hh] += jnp.dot(a.astype(BF16), vb, preferred_element_type=F32)
                    run = run + cs[:, SB_BLK:]
                    run_ref[u, 2 * p + hh] = run
                    alive = run if alive is None else jnp.maximum(alive, run)
            return j - 1, jnp.max(alive)

        lax.while_loop(cond, body, (iq - SB_WIN // SB_ROWS, alive0))
        for p in range(pairs):
            o_ref[pl.ds(start, SB_ROWS), p * LANES:(p + 1) * LANES] = jnp.where(
                first, acc_ref[u, 2 * p], acc_ref[u, 2 * p + 1]).astype(BF16)

    def query_blocks(it, carry):
        alive = [first_window(it * SB_UNROLL + u, u) for u in range(SB_UNROLL)]
        for u in range(SB_UNROLL):
            older_keys(it * SB_UNROLL + u, u, alive[u])
        return carry

    lax.fori_loop(0, q_ref.shape[0] // (SB_ROWS * SB_UNROLL), query_blocks, 0)


def _stick_breaking(proj, bsz, seq):
    n = proj.shape[0]
    groups = SB_HEADS * SB_DIM // HALF
    front = SB_WIN - SB_ROWS
    j = np.arange(SB_BLK)
    tri = np.concatenate([(j[:, None] > j[None, :]).astype(np.float32),
                          np.ones((SB_BLK, SB_BLK), np.float32)], axis=1)
    return pl.pallas_call(
        _sb_kernel,
        grid=(bsz, groups),
        in_specs=[pl.BlockSpec((seq, HALF), lambda b, g: (b, g)),
                  pl.BlockSpec((seq, HALF), lambda b, g: (b, groups + g)),
                  pl.BlockSpec((seq, HALF), lambda b, g: (b, 2 * groups + g)),
                  pl.BlockSpec((SB_BLK, 2 * SB_BLK), lambda b, g: (0, 0))],
        out_specs=pl.BlockSpec((seq, HALF), lambda b, g: (b, g)),
        out_shape=jax.ShapeDtypeStruct((n, SB_HEADS * SB_DIM), BF16),
        scratch_shapes=[pltpu.VMEM((front + seq, HALF), BF16), pltpu.VMEM((front + seq, HALF), BF16),
                        pltpu.VMEM((SB_UNROLL, HALF // SB_DIM, SB_ROWS, SB_BLK), F32),
                        pltpu.VMEM((SB_UNROLL, HALF // SB_DIM, SB_ROWS, LANES), F32)],
        compiler_params=_cparams(("arbitrary", "arbitrary")), name="stick_breaking",
    )(proj, proj, proj, jnp.asarray(tri, BF16))


def _out_route_kernel(ma_ref, mb_ref, wa_ref, wb_ref, x_ref, g1_ref, g_ref, sc_ref, sh_ref,
                      rw_ref, rb_ref, tri_ref, xo_ref, he_ref, rt_ref, cnt_ref, carry_ref):
    @pl.when(pl.program_id(0) == 0)
    def _():
        carry_ref[...] = jnp.zeros_like(carry_ref)

    tm = x_ref.shape[0]
    mix = (jnp.dot(ma_ref[...], wa_ref[...], preferred_element_type=F32)
           + jnp.dot(mb_ref[...], wb_ref[...], preferred_element_type=F32))
    x = x_ref[...] + g1_ref[0] * mix
    xo_ref[...] = x
    h = _norm_mod(x, g_ref[...], sc_ref[0], sh_ref[0])
    for jt in range(ROW_SUB):
        he_ref[pl.ds(jt, tm, stride=ROW_SUB), :] = h[:, jt * LANES:(jt + 1) * LANES]

    nt = (((1,), (1,)), ((), ()))
    h_hi = h.astype(BF16)
    h_lo = (h - h_hi.astype(F32)).astype(BF16)
    w = rw_ref[...]
    w_hi = w.astype(BF16)
    w_lo = (w - w_hi.astype(F32)).astype(BF16)
    logits = (lax.dot_general(w_hi, h_hi, nt, preferred_element_type=F32)
              + lax.dot_general(w_hi, h_lo, nt, preferred_element_type=F32)
              + lax.dot_general(w_lo, h_hi, nt, preferred_element_type=F32))
    sel = jax.nn.sigmoid(logits) + rb_ref[...]

    best = None
    for g in range(N_GROUPS):
        for p in range(len(PAIR_A)):
            ea, eb = 4 * g + PAIR_A[p], 4 * g + PAIR_B[p]
            val = sel[ea:ea + 1, :] + sel[eb:eb + 1, :]
            cid = jnp.full((1, tm), float(len(PAIR_A) * g + p), F32)
            cand = (val, cid)
            if best is None:
                best = cand
            else:
                take = val > best[0]
                best = tuple(jnp.where(take, c_, b_) for c_, b_ in zip(cand, best))
    cls = best[1]

    crow = lax.broadcasted_iota(jnp.int32, (CLASS_ROWS, tm), 0).astype(F32)
    onehot = jnp.where(crow == cls, 1.0, 0.0)
    before = jnp.dot(onehot.astype(BF16), tri_ref[...], preferred_element_type=F32)
    carry = carry_ref[...]
    rank = jnp.sum(onehot * (before + carry[:, :1]), axis=0, keepdims=True)
    carry = carry + jnp.sum(onehot, axis=1, keepdims=True)
    carry_ref[...] = carry
    cnt_ref[0] = carry

    srow = lax.broadcasted_iota(jnp.int32, (8, tm), 0)
    rt_ref[0] = jnp.where(srow == 0, cls, jnp.where(srow == 1, rank, 0.0))


def _out_route(mix_a, mix_b, col_a, col_b, w_out, x, g1, gain, sc, sh, router_w, router_b, seq):
    n = x.shape[0]
    tm = ROW_TILE
    nt = n // tm
    tpb = seq // tm
    t = np.arange(tm)
    tri = jnp.asarray((t[:, None] < t[None, :]).astype(np.float32), BF16)
    row = pl.BlockSpec((tm, D_MODEL), lambda i: (i, 0))
    per_b = pl.BlockSpec((1, 1, D_MODEL), lambda i: (i // tpb, 0, 0))
    rb = jnp.broadcast_to(router_b.astype(F32)[:, None], (N_EXPERTS, tm))
    return pl.pallas_call(
        _out_route_kernel,
        grid=(nt,),
        in_specs=[pl.BlockSpec((tm, HALF), lambda i: (i, col_a)),
                  pl.BlockSpec((tm, HALF), lambda i: (i, col_b)),
                  pl.BlockSpec((HALF, D_MODEL), lambda i: (0, 0)),
                  pl.BlockSpec((HALF, D_MODEL), lambda i: (1, 0)),
                  row, per_b,
                  pl.BlockSpec((1, D_MODEL), lambda i: (0, 0)), per_b, per_b,
                  pl.BlockSpec((N_EXPERTS, D_MODEL), lambda i: (0, 0)),
                  pl.BlockSpec((N_EXPERTS, tm), lambda i: (0, 0)),
                  pl.BlockSpec((tm, tm), lambda i: (0, 0))],
        out_specs=[row,
                   pl.BlockSpec((tm * ROW_SUB, LANES), lambda i: (i, 0)),
                   pl.BlockSpec((1, 8, tm), lambda i: (i, 0, 0)),
                   pl.BlockSpec((1, CLASS_ROWS, LANES), lambda i: (i, 0, 0))],
        out_shape=[jax.ShapeDtypeStruct((n, D_MODEL), F32),
                   jax.ShapeDtypeStruct((n * ROW_SUB, LANES), F32),
                   jax.ShapeDtypeStruct((nt, 8, tm), F32),
                   jax.ShapeDtypeStruct((nt, CLASS_ROWS, LANES), F32)],
        scratch_shapes=[pltpu.VMEM((CLASS_ROWS, LANES), F32)],
        compiler_params=_cparams(("arbitrary",)), name="out_proj_route",
    )(mix_a, mix_b, w_out, w_out, x, g1, gain.reshape(1, D_MODEL), sc, sh,
      router_w.T.astype(F32), rb, tri)


def _fill_slots(pos_ref, nv_ref, inv_ref):
    batch = 16

    def fill_tile(j, c):
        def fill(g, c2):
            for u in range(batch):
                inv_ref[j * MOE_TILE + g * batch + u] = 0
            return c2

        return lax.fori_loop(nv_ref[j] // batch, MOE_TILE // batch, fill, c)

    lax.fori_loop(0, nv_ref.shape[0], fill_tile, 0)

    def place(g, c):
        slots = [pos_ref[g * batch + u] for u in range(batch)]
        for u in range(batch):
            inv_ref[slots[u]] = g * batch + u
        return c

    lax.fori_loop(0, pos_ref.shape[0] // batch, place, 0)


def _routing_plan(route, counts, n):
    nt, _, tm = route.shape
    cls = route[:, 0, :].reshape(n).astype(jnp.int32)
    rank = route[:, 1, :].reshape(n).astype(jnp.int32)
    cnt = counts[-1, :N_CLASS, 0].astype(jnp.int32)
    tiles_c = (cnt + MOE_TILE - 1) // MOE_TILE
    ends = jnp.cumsum(tiles_c)
    starts = ends - tiles_c
    pos = starts[cls] * MOE_TILE + rank
    max_tiles = n // MOE_TILE + N_CLASS
    total = ends[-1]
    j = jnp.arange(max_tiles, dtype=jnp.int32)
    jj = jnp.minimum(j, total - 1)
    tcls = jnp.sum((ends[None, :] <= jj[:, None]).astype(jnp.int32), axis=1)
    grp = tcls // len(PAIR_A)
    pair = tcls % len(PAIR_A)
    ea = 4 * grp + jnp.asarray(PAIR_A, jnp.int32)[pair]
    eb = 4 * grp + jnp.asarray(PAIR_B, jnp.int32)[pair]
    valid_rows = jnp.clip(cnt[tcls] - (jj - starts[tcls]) * MOE_TILE, 0, MOE_TILE)
    valid_rows = jnp.where(j < total, valid_rows, 0).astype(jnp.int32)
    return pos, ea, eb, valid_rows, total.reshape(1)


def _moe_kernel(pos_ref, ea_ref, eb_ref, nv_ref, tot_ref, h_hbm, rw_ref, wg_a, wu_a, wd_a,
                wg_b, wu_b, wd_b, y_hbm, gbuf, ybuf, inv_ref, gsem, ssem):
    j = pl.program_id(0)
    total = tot_ref[0]
    gslot = j % GATHER_BUFS
    yslot = j % 2
    unroll = 16

    def gather_row(tile, s, r, prio):
        src = pl.multiple_of(inv_ref[tile * MOE_TILE + r] * ROW_SUB, ROW_SUB)
        dst = pl.multiple_of(r * ROW_SUB, ROW_SUB)
        pltpu.make_async_copy(h_hbm.at[pl.ds(src, ROW_SUB), :], gbuf.at[s, pl.ds(dst, ROW_SUB), :],
                              gsem.at[s]).start(priority=prio)

    def scatter_row(tile, s, r, prio):
        src = pl.multiple_of(r * ROW_SUB, ROW_SUB)
        dst = pl.multiple_of(inv_ref[tile * MOE_TILE + r] * ROW_SUB, ROW_SUB)
        pltpu.make_async_copy(ybuf.at[s, pl.ds(src, ROW_SUB), :], y_hbm.at[pl.ds(dst, ROW_SUB), :],
                              ssem.at[s]).start(priority=prio)

    def gathered_rows(tile):
        return (nv_ref[tile] + unroll - 1) // unroll * unroll

    def gather_rows(tile, s):
        def group(g, c):
            for u in range(unroll):
                gather_row(tile, s, g * unroll + u, u % 2)
            return c

        lax.fori_loop(0, gathered_rows(tile) // unroll, group, 0)

    def wait_rows(count, make_copy):
        for bit in range(MOE_TILE.bit_length()):
            @pl.when((count >> bit) & 1 == 1)
            def _():
                make_copy((1 << bit) * ROW_SUB).wait()

    def gather_wait(tile, s):
        wait_rows(gathered_rows(tile), lambda size: pltpu.make_async_copy(
            h_hbm.at[pl.ds(0, size), :], gbuf.at[s, pl.ds(0, size), :], gsem.at[s]))

    def scatter_rows(tile, s):
        nv = nv_ref[tile]
        groups = nv // unroll

        def group(g, c):
            for u in range(unroll):
                scatter_row(tile, s, g * unroll + u, u % 2)
            return c

        lax.fori_loop(0, groups, group, 0)
        for u in range(unroll):
            @pl.when(groups * unroll + u < nv)
            def _():
                scatter_row(tile, s, groups * unroll + u, u % 2)

    def scatter_wait(tile, s):
        wait_rows(nv_ref[tile], lambda size: pltpu.make_async_copy(
            ybuf.at[s, pl.ds(0, size), :], y_hbm.at[pl.ds(0, size), :], ssem.at[s]))

    @pl.when(j == 0)
    def _():
        _fill_slots(pos_ref, nv_ref, inv_ref)
        gbuf[...] = jnp.zeros_like(gbuf)
        for t in range(GATHER_BUFS - 1):
            @pl.when(t < total)
            def _():
                gather_rows(t, t)

    @pl.when(j < total)
    def _():
        gather_wait(j, gslot)
        x = jnp.concatenate([gbuf[gslot, pl.ds(jt, MOE_TILE, stride=ROW_SUB), :]
                             for jt in range(ROW_SUB)], axis=1)
        xb = x.astype(BF16)
        scores = [jax.nn.sigmoid(jnp.sum(x * rw_ref[pl.ds(e_ref[j], 1), :], axis=-1, keepdims=True))
                  for e_ref in (ea_ref, eb_ref)]
        shares = [sc_ / (scores[0] + scores[1]) for sc_ in scores]
        y = None
        for lane, (wg, wu, wd) in enumerate(((wg_a, wu_a, wd_a), (wg_b, wu_b, wd_b))):
            wgu = jnp.concatenate([wg[0].astype(BF16), wu[0].astype(BF16)], axis=1)
            gu = jnp.dot(xb, wgu, preferred_element_type=F32)
            g, u = gu[:, :D_EXPERT], gu[:, D_EXPERT:]
            he = ((g * jax.nn.sigmoid(g)) * u * shares[lane]).astype(BF16)
            part = jnp.dot(he, wd[0].astype(BF16), preferred_element_type=F32)
            y = part if y is None else y + part

        @pl.when(j >= 2)
        def _():
            scatter_wait(j - 2, yslot)

        for jt in range(ROW_SUB):
            chunk = y[:, jt * LANES:(jt + 1) * LANES]
            ybuf[yslot, pl.ds(jt, MOE_TILE, stride=ROW_SUB), :] = chunk

        scatter_rows(j, yslot)
        ahead = j + GATHER_BUFS - 1

        @pl.when(ahead < total)
        def _():
            gather_rows(ahead, ahead % GATHER_BUFS)

        @pl.when(j == total - 1)
        def _():
            @pl.when(j >= 1)
            def _():
                scatter_wait(j - 1, 1 - yslot)

            scatter_wait(j, yslot)


def _moe(hext, pos, ea, eb, valid_rows, total, router_wt, layer, w_gate, w_up, w_down):
    n = hext.shape[0] // ROW_SUB
    max_tiles = ea.shape[0]

    def wspec(shape, which):
        if which == 0:
            return pl.BlockSpec((None, 1) + shape,
                                lambda j, inv_, ea_, eb_, nv_, t_: (layer, ea_[j], 0, 0))
        return pl.BlockSpec((None, 1) + shape, lambda j, inv_, ea_, eb_, nv_, t_: (layer, eb_[j], 0, 0))

    up = (D_MODEL, D_EXPERT)
    down = (D_EXPERT, D_MODEL)
    grid_spec = pltpu.PrefetchScalarGridSpec(
        num_scalar_prefetch=5,
        grid=(max_tiles,),
        in_specs=[pl.BlockSpec(memory_space=pl.ANY),
                  pl.BlockSpec((N_EXPERTS, D_MODEL), lambda j, inv_, ea_, eb_, nv_, t_: (0, 0)),
                  wspec(up, 0), wspec(up, 0), wspec(down, 0),
                  wspec(up, 1), wspec(up, 1), wspec(down, 1)],
        out_specs=pl.BlockSpec(memory_space=pl.ANY),
        scratch_shapes=[pltpu.VMEM((GATHER_BUFS, MOE_TILE * ROW_SUB, LANES), F32),
                        pltpu.VMEM((2, MOE_TILE * ROW_SUB, LANES), F32),
                        pltpu.SMEM((max_tiles * MOE_TILE,), jnp.int32),
                        pltpu.SemaphoreType.DMA((GATHER_BUFS,)),
                        pltpu.SemaphoreType.DMA((2,))],
    )
    return pl.pallas_call(
        _moe_kernel,
        grid_spec=grid_spec,
        out_shape=jax.ShapeDtypeStruct((n * ROW_SUB, LANES), F32),
        compiler_params=_cparams(("arbitrary",)), name="moe_experts",
    )(pos, ea, eb, valid_rows, total, hext, router_wt, w_gate, w_up, w_down, w_gate, w_up, w_down)


def _final_kernel(x_ref, y_ref, g_ref, o_ref):
    o_ref[...] = x_ref[...] + g_ref[0] * _token_rows(y_ref)


def _final(x, y, gate, seq):
    n = x.shape[0]
    tpb = seq // ROW_TILE
    row = pl.BlockSpec((ROW_TILE, D_MODEL), lambda i: (i, 0))
    return pl.pallas_call(
        _final_kernel, grid=(n // ROW_TILE,),
        in_specs=[row, pl.BlockSpec((ROW_TILE * ROW_SUB, LANES), lambda i: (i, 0)),
                  pl.BlockSpec((1, 1, D_MODEL), lambda i: (i // tpb, 0, 0))],
        out_specs=row, out_shape=jax.ShapeDtypeStruct((n, D_MODEL), F32),
        compiler_params=_cparams(("arbitrary",)), name="final_residual",
    )(x, y, gate)


def kernel(x, c, ada_w, ada_b, norm1_g, norm2_g, even_w_in, even_w_out, att_q_norm_g, att_k_norm_g,
           att_rel_bias, odd_w_in, odd_w_out, router_w, router_b, exp_w_gate, exp_w_up, exp_w_down):
    bsz, seq, d = x.shape
    n = bsz * seq
    mod = _modulation(c, ada_w, ada_b)
    mods = [[mod[l, :, k * d:(k + 1) * d].reshape(bsz, 1, d) for k in range(6)] for l in range(2)]
    xf = x.reshape(n, d)
    w_gate, w_up, w_down = exp_w_gate, exp_w_up, exp_w_down

    def moe_layer(layer, mix_a, mix_b, col_a, col_b, w_out, xin):
        sh1, sc1, g1, sh2, sc2, g2 = mods[layer]
        x1, hext, route, counts = _out_route(mix_a, mix_b, col_a, col_b, w_out.astype(BF16), xin, g1,
                                             norm2_g[layer], sc2, sh2, router_w, router_b, seq)
        pos, ea, eb, valid_rows, total = _routing_plan(route, counts, n)
        y = _moe(hext, pos, ea, eb, valid_rows, total, router_w.T.astype(F32), layer,
                 w_gate, w_up, w_down)
        return x1, y, g2

    sh1, sc1, _, _, _, _ = mods[0]
    qk_gain = jnp.stack([att_q_norm_g[0], att_k_norm_g[0]])
    proj0 = _project(xf, norm1_g[0], sc1, sh1, even_w_in[0].astype(BF16), seq, qk_gain=qk_gain)
    ret = _retention(proj0, bsz, seq)
    att = _chunk_attention(proj0, att_rel_bias[0], bsz, seq)
    x1, y0, g2_0 = moe_layer(0, ret, att, 0, 0, even_w_out[0], xf)

    sh1, sc1, _, _, _, _ = mods[1]
    x2, proj1 = _project(x1, norm1_g[1], sc1, sh1, odd_w_in[0].astype(BF16), seq, y=y0, gate=g2_0)
    sbo = _stick_breaking(proj1, bsz, seq)
    x3, y1, g2_1 = moe_layer(1, sbo, sbo, 0, 1, odd_w_out[0], x2)
    return _final(x3, y1, g2_1, seq).reshape(bsz, seq, d)
```
